```python
import math
import jax, jax.numpy as jnp
from jax import lax
import numpy as np

D_MODEL = 1024
BATCH = 16
SEQ = 2048
DEPTH = 2

HEAD_DIM = 64
FOX_HEADS = 4
MLA_HEADS = 6
DIL_HEADS = 6
MLA_Q_RANK = 384
MLA_KV_RANK = 256
MLA_NOPE_DIM = 64
MLA_ROPE_DIM = 32
MLA_QK_DIM = MLA_NOPE_DIM + MLA_ROPE_DIM
MLA_V_DIM = 64
ROT_DIM = HEAD_DIM // 4
ROPE_THETA = 500000.0
DILATED_PATTERNS = ((128, 1), (512, 4), (2048, 16))
Q_BLOCK = 128
FOX_WIDTH = FOX_HEADS * HEAD_DIM
MLA_WIDTH = MLA_HEADS * MLA_V_DIM
DIL_WIDTH = DIL_HEADS * HEAD_DIM
MIX_WIDTH = FOX_WIDTH + MLA_WIDTH + DIL_WIDTH
MEM_TOKENS = 256
MEM_HEADS = 4
MEM_WIDTH = MEM_HEADS * HEAD_DIM
N_EXPERTS = 32
TOP_K = 4
D_EXPERT = 1024
SWIGLU_ALPHA = 1.702
SWIGLU_LIMIT = 7.0
MOE_CHUNK = 512
RMS_EPS = 1e-6
IN_SPLIT_SIZES = (FOX_WIDTH, FOX_WIDTH, FOX_WIDTH, FOX_HEADS,
                  MLA_Q_RANK, MLA_KV_RANK, MLA_ROPE_DIM,
                  DIL_WIDTH, DIL_WIDTH, DIL_WIDTH)
IN_SPLIT_POINTS = tuple(int(v) for v in np.cumsum(IN_SPLIT_SIZES)[:-1])
IN_WIDTH = int(sum(IN_SPLIT_SIZES))

kernel_name = "hybrid_fox_mla_dilated_moe_block"


def rms_norm(x, g):
    xf = x.astype(jnp.float32)
    y = xf * lax.rsqrt(jnp.mean(xf * xf, axis=-1, keepdims=True) + RMS_EPS)
    return (y * g.astype(jnp.float32)).astype(x.dtype)


def rope_tables(positions, rot_dim):
    inv = ROPE_THETA ** (-jnp.arange(0, rot_dim, 2, dtype=jnp.float32) / rot_dim)
    ang = positions.astype(jnp.float32)[..., None] * inv
    return jnp.cos(ang)[:, :, None, :], jnp.sin(ang)[:, :, None, :]


def apply_rope(x, cos, sin):
    x1, x2 = jnp.split(x.astype(jnp.float32), 2, axis=-1)
    return jnp.concatenate([x1 * cos - x2 * sin, x2 * cos + x1 * sin], axis=-1).astype(x.dtype)


def causal_dense_attention(q, k, v, scale, cum_log_forget=None):
    B, S, H, _ = q.shape
    nb = S // Q_BLOCK
    q_blocks = q.reshape(B, nb, Q_BLOCK, H, -1).swapaxes(0, 1)
    key_pos = jnp.arange(S)
    if cum_log_forget is None:
        c_blocks, c_keys = None, None
    else:
        c_keys = cum_log_forget.transpose(0, 2, 1)
        c_blocks = cum_log_forget.reshape(B, nb, Q_BLOCK, H).transpose(1, 0, 3, 2)

    def block(args):
        i, q_i, c_i = args
        s = jnp.einsum("bqhd,bkhd->bhqk", q_i, k, preferred_element_type=jnp.float32) * scale
        if c_i is not None:
            s = s + (c_i[..., None] - c_keys[:, :, None, :])
        q_pos = i * Q_BLOCK + jnp.arange(Q_BLOCK)
        s = jnp.where(key_pos[None, :] <= q_pos[:, None], s, -jnp.inf)
        p = jax.nn.softmax(s, axis=-1).astype(v.dtype)
        return jnp.einsum("bhqk,bkhd->bqhd", p, v)

    out = lax.map(block, (jnp.arange(nb), q_blocks, c_blocks))
    return out.swapaxes(0, 1).reshape(B, S, H, -1)


def dilated_window_attention(q, k, v, window, dilation, scale):
    B, S, H, Dh = q.shape
    n = window // dilation
    L = S // dilation
    nb = -(-L // n)
    Lp = nb * n

    def to_blocks(t):
        t = t.reshape(B, L, dilation, H, Dh).transpose(0, 2, 1, 3, 4)
        t = jnp.pad(t, ((0, 0), (0, 0), (0, Lp - L), (0, 0), (0, 0)))
        return t.reshape(B, dilation, nb, n, H, Dh)

    def with_prev(t):
        prev = jnp.pad(t, ((0, 0), (0, 0), (1, 0), (0, 0), (0, 0), (0, 0)))[:, :, :-1]
        return jnp.concatenate([prev, t], axis=3)

    qb = to_blocks(q)
    kc = with_prev(to_blocks(k))
    vc = with_prev(to_blocks(v))
    s = jnp.einsum("bgcqhd,bgckhd->bgchqk", qb, kc, preferred_element_type=jnp.float32) * scale
    qq = jnp.arange(n)[:, None]
    kk = jnp.arange(2 * n)[None, :]
    rel = qq + n - kk
    blk = jnp.arange(nb)[:, None, None]
    valid = (rel >= 0) & (rel <= n) & (blk * n - n + kk >= 0)
    s = jnp.where(valid[:, None], s, -jnp.inf)
    lse = jax.nn.logsumexp(s, axis=-1)
    p = jnp.exp(s - lse[..., None]).astype(v.dtype)
    o = jnp.einsum("bgchqk,bgckhd->bgcqhd", p, vc)

    def from_blocks(t):
        t = t.reshape(B, dilation, Lp, H, -1)[:, :, :L]
        return t.transpose(0, 2, 1, 3, 4).reshape(B, S, H, -1)

    lse_seq = from_blocks(lse.transpose(0, 1, 2, 4, 3)[..., None])[..., 0]
    return from_blocks(o), lse_seq


def parallel_head_groups(h, positions, w_in, b_forget, g_qa, g_ka, g_cq, g_ckv, w_uq, w_ukv,
                         g_qb, g_kb, g_qc, g_kc, g_group, w_out):
    B, S, _ = h.shape
    proj = h @ w_in
    qa, ka, va, fa, cq, ckv, kr, qc, kc, vc = jnp.split(proj, IN_SPLIT_POINTS, axis=-1)

    qa = rms_norm(qa.reshape(B, S, FOX_HEADS, HEAD_DIM), g_qa)
    ka = rms_norm(ka.reshape(B, S, FOX_HEADS, HEAD_DIM), g_ka)
    va = va.reshape(B, S, FOX_HEADS, HEAD_DIM)
    log_f = jax.nn.log_sigmoid((fa + b_forget).astype(jnp.float32))
    cum = jnp.cumsum(log_f, axis=1)
    oa = causal_dense_attention(qa, ka, va, HEAD_DIM ** -0.5, cum)

    cq = rms_norm(cq, g_cq)
    ckv = rms_norm(ckv, g_ckv)
    qb = (cq @ w_uq).reshape(B, S, MLA_HEADS, MLA_QK_DIM)
    kvb = (ckv @ w_ukv).reshape(B, S, MLA_HEADS, MLA_NOPE_DIM + MLA_V_DIM)
    k_nope, vb = kvb[..., :MLA_NOPE_DIM], kvb[..., MLA_NOPE_DIM:]
    k_rope = jnp.broadcast_to(kr[:, :, None, :], (B, S, MLA_HEADS, MLA_ROPE_DIM))
    kb = jnp.concatenate([k_nope, k_rope], axis=-1)
    qb = rms_norm(qb, g_qb)
    kb = rms_norm(kb, g_kb)
    cos_b, sin_b = rope_tables(positions, MLA_ROPE_DIM)
    qb = jnp.concatenate([qb[..., :MLA_NOPE_DIM], apply_rope(qb[..., MLA_NOPE_DIM:], cos_b, sin_b)], axis=-1)
    kb = jnp.concatenate([kb[..., :MLA_NOPE_DIM], apply_rope(kb[..., MLA_NOPE_DIM:], cos_b, sin_b)], axis=-1)
    ob = causal_dense_attention(qb, kb, vb, MLA_QK_DIM ** -0.5)

    qc = rms_norm(qc.reshape(B, S, DIL_HEADS, HEAD_DIM), g_qc)
    kc = rms_norm(kc.reshape(B, S, DIL_HEADS, HEAD_DIM), g_kc)
    vc = vc.reshape(B, S, DIL_HEADS, HEAD_DIM)
    cos_c, sin_c = rope_tables(positions, ROT_DIM)
    qc = jnp.concatenate([apply_rope(qc[..., :ROT_DIM], cos_c, sin_c), qc[..., ROT_DIM:]], axis=-1)
    kc = jnp.concatenate([apply_rope(kc[..., :ROT_DIM], cos_c, sin_c), kc[..., ROT_DIM:]], axis=-1)
    outs, lses = [], []
    for window, dilation in DILATED_PATTERNS:
        o_i, lse_i = dilated_window_attention(qc, kc, vc, window, dilation, HEAD_DIM ** -0.5)
        outs.append(o_i)
        lses.append(lse_i)
    mix_w = jax.nn.softmax(jnp.stack(lses, axis=0), axis=0)
    oc = jnp.sum(mix_w[..., None] * jnp.stack(outs, axis=0).astype(jnp.float32), axis=0).astype(h.dtype)

    ga, gb, gc = jnp.split(g_group, [FOX_WIDTH, FOX_WIDTH + MLA_WIDTH])
    merged = jnp.concatenate([rms_norm(oa.reshape(B, S, FOX_WIDTH), ga),
                              rms_norm(ob.reshape(B, S, MLA_WIDTH), gb),
                              rms_norm(oc.reshape(B, S, DIL_WIDTH), gc)], axis=-1)
    return merged @ w_out


def memory_cross_attention(h, mem_n, w_mq, w_mkv, g_qm, g_km, w_mo):
    B, S, _ = h.shape
    M = mem_n.shape[1]
    q = rms_norm((h @ w_mq).reshape(B, S, MEM_HEADS, HEAD_DIM), g_qm)
    k, v = jnp.split(mem_n @ w_mkv, 2, axis=-1)
    k = rms_norm(k.reshape(B, M, MEM_HEADS, HEAD_DIM), g_km)
    v = v.reshape(B, M, MEM_HEADS, HEAD_DIM)
    s = jnp.einsum("bshd,bmhd->bhsm", q, k, preferred_element_type=jnp.float32) * HEAD_DIM ** -0.5
    p = jax.nn.softmax(s, axis=-1).astype(v.dtype)
    o = jnp.einsum("bhsm,bmhd->bshd", p, v).reshape(B, S, MEM_WIDTH)
    return o @ w_mo


def routed_expert_ffn(h, w_router, b_router, w_e1, b_e1, w_e2, b_e2):
    B, S, D = h.shape
    T = B * S
    ht = h.reshape(T, D)
    logits = (ht @ w_router + b_router).astype(jnp.float32)
    top_vals, top_idx = lax.top_k(logits, TOP_K)
    top_w = jax.nn.softmax(top_vals, axis=-1)
    gate = jnp.einsum("tk,tke->te", top_w, jax.nn.one_hot(top_idx, N_EXPERTS, dtype=jnp.float32))
    chunk = math.gcd(T, MOE_CHUNK)
    nc = T // chunk

    def expert_block(args):
        xc, gc = args
        hh = jnp.einsum("td,edf->tef", xc, w_e1) + b_e1
        glu = jnp.minimum(hh[..., 0::2], SWIGLU_LIMIT)
        lin = jnp.clip(hh[..., 1::2], -SWIGLU_LIMIT, SWIGLU_LIMIT)
        act = glu * jax.nn.sigmoid(SWIGLU_ALPHA * glu) * (lin + 1.0)
        act = act * gc[..., None].astype(act.dtype)
        return jnp.einsum("tef,efd->td", act, w_e2) + gc.astype(b_e2.dtype) @ b_e2

    y = lax.map(expert_block, (ht.reshape(nc, chunk, D), gate.reshape(nc, chunk, N_EXPERTS)))
    return y.reshape(B, S, D).astype(h.dtype)


def setup_inputs(seed: int = 0) -> dict:
    key = jax.random.key(seed)
    ks = iter(jax.random.split(key, 48))
    f32 = jnp.float32
    L, D = DEPTH, D_MODEL

    def w(shape, fan_in):
        return jax.random.normal(next(ks), shape, f32) * (fan_in ** -0.5)

    def gain(shape):
        return 1.0 + 0.02 * jax.random.normal(next(ks), shape, f32)

    def small(shape, scale=0.01):
        return scale * jax.random.normal(next(ks), shape, f32)

    x = jax.random.normal(next(ks), (BATCH, SEQ, D), f32)
    mem = jax.random.normal(next(ks), (BATCH, MEM_TOKENS, D), f32)
    positions = (jnp.arange(SEQ, dtype=jnp.int32)[None, :]
                 + jax.random.randint(next(ks), (BATCH, 1), 0, 1024, dtype=jnp.int32))
    return {
        "x": x, "mem": mem, "positions": positions,
        "g_mix": gain((L, D)),
        "w_in": w((L, D, IN_WIDTH), D),
        "b_forget": 3.0 + 0.1 * jax.random.normal(next(ks), (L, FOX_HEADS), f32),
        "g_qa": gain((L, HEAD_DIM)), "g_ka": gain((L, HEAD_DIM)),
        "g_cq": gain((L, MLA_Q_RANK)), "g_ckv": gain((L, MLA_KV_RANK)),
        "w_uq": w((L, MLA_Q_RANK, MLA_HEADS * MLA_QK_DIM), MLA_Q_RANK),
        "w_ukv": w((L, MLA_KV_RANK, MLA_HEADS * (MLA_NOPE_DIM + MLA_V_DIM)), MLA_KV_RANK),
        "g_qb": gain((L, MLA_QK_DIM)), "g_kb": gain((L, MLA_QK_DIM)),
        "g_qc": gain((L, HEAD_DIM)), "g_kc": gain((L, HEAD_DIM)),
        "g_group": gain((L, MIX_WIDTH)),
        "w_out": w((L, MIX_WIDTH, D), MIX_WIDTH),
        "g_xmem": gain((L, D)), "g_mem": gain((L, D)),
        "w_mq": w((L, D, MEM_WIDTH), D),
        "w_mkv": w((L, D, 2 * MEM_WIDTH), D),
        "g_qm": gain((L, HEAD_DIM)), "g_km": gain((L, HEAD_DIM)),
        "w_mo": w((L, MEM_WIDTH, D), MEM_WIDTH),
        "g_ffn": gain((L, D)),
        "w_router": w((L, D, N_EXPERTS), D),
        "b_router": small((L, N_EXPERTS)),
        "w_e1": w((L, N_EXPERTS, D, 2 * D_EXPERT), D),
        "b_e1": small((L, N_EXPERTS, 2 * D_EXPERT)),
        "w_e2": w((L, N_EXPERTS, D_EXPERT, D), D_EXPERT),
        "b_e2": small((L, N_EXPERTS, D)),
    }


def reference(x, mem, positions, g_mix, w_in, b_forget, g_qa, g_ka, g_cq, g_ckv, w_uq, w_ukv,
              g_qb, g_kb, g_qc, g_kc, g_group, w_out, g_xmem, g_mem, w_mq, w_mkv, g_qm, g_km,
              w_mo, g_ffn, w_router, b_router, w_e1, b_e1, w_e2, b_e2):
    for l in range(DEPTH):
        x = x + parallel_head_groups(rms_norm(x, g_mix[l]), positions, w_in[l], b_forget[l],
                                     g_qa[l], g_ka[l], g_cq[l], g_ckv[l], w_uq[l], w_ukv[l],
                                     g_qb[l], g_kb[l], g_qc[l], g_kc[l], g_group[l], w_out[l])
        x = x + memory_cross_attention(rms_norm(x, g_xmem[l]), rms_norm(mem, g_mem[l]),
                                       w_mq[l], w_mkv[l], g_qm[l], g_km[l], w_mo[l])
        x = x + routed_expert_ffn(rms_norm(x, g_ffn[l]), w_router[l], b_router[l],
                                  w_e1[l], b_e1[l], w_e2[l], b_e2[l])
    return x
```

```python
import functools
import math

import jax
import jax.numpy as jnp
import numpy as np
from jax import lax
from jax.experimental import pallas as pl
from jax.experimental.pallas import tpu as pltpu

HEAD_DIM = 64
FOX_HEADS = 4
MLA_HEADS = 6
DIL_HEADS = 6
MLA_Q_RANK = 384
MLA_KV_RANK = 256
MLA_NOPE_DIM = 64
MLA_ROPE_DIM = 32
MLA_QK_DIM = MLA_NOPE_DIM + MLA_ROPE_DIM
MLA_V_DIM = 64
ROT_DIM = HEAD_DIM // 4
ROPE_THETA = 500000.0
DILATED_PATTERNS = ((128, 1), (512, 4), (2048, 16))
FOX_WIDTH = FOX_HEADS * HEAD_DIM
MLA_WIDTH = MLA_HEADS * MLA_V_DIM
DIL_WIDTH = DIL_HEADS * HEAD_DIM
MEM_HEADS = 4
MEM_WIDTH = MEM_HEADS * HEAD_DIM
TOP_K = 4
SWIGLU_ALPHA = 1.702
SWIGLU_LIMIT = 7.0
RMS_EPS = 1e-6

LANES = 128
VMEM_LIMIT_BYTES = 56 * 1024 * 1024

PROJ_TILE = 512
ATTN_TILE = 256
EXPERT_ROW_TILE = 256
MOE_TOKEN_TILE = 256

NEG_BIG = -1e30
MLA_PAD = 128

F32 = jnp.float32
BF16 = jnp.bfloat16


def _cparams(sem):
    return pltpu.CompilerParams(dimension_semantics=sem, vmem_limit_bytes=VMEM_LIMIT_BYTES)


def _full(shape):
    nd = len(shape)
    return pl.BlockSpec(shape, lambda *_: (0,) * nd)


def _dot(a, b):
    return jnp.dot(a, b, preferred_element_type=F32)


def _dot_nt(a, b):
    return lax.dot_general(a, b, (((1,), (1,)), ((), ())), preferred_element_type=F32)


def _rms_rows(x, g):
    return x * lax.rsqrt(jnp.mean(x * x, axis=-1, keepdims=True) + RMS_EPS) * g


def _group_norm_tiles(y, ones_mat, inv_n):
    outs = []
    for j in range(y.shape[1] // LANES):
        yj = y[:, j * LANES:(j + 1) * LANES]
        ss = _dot((yj * yj).astype(BF16), ones_mat)
        outs.append(yj * lax.rsqrt(ss * inv_n + RMS_EPS))
    return outs


def _pack_bf16_pairs(x):
    w = x.shape[1] // 2
    hi = lax.bitcast_convert_type(x[:, :w].astype(jnp.bfloat16).astype(F32), jnp.uint32)
    lo = lax.bitcast_convert_type(x[:, w:].astype(jnp.bfloat16).astype(F32), jnp.uint32)
    return hi | (lo >> 16)


def _unpack_bf16_pairs(p):
    hi = lax.bitcast_convert_type(p & jnp.uint32(0xFFFF0000), F32)
    lo = lax.bitcast_convert_type(p << 16, F32)
    return jnp.concatenate([hi, lo], axis=1)


def _rope_tile(y, half, c, s_dn, s_up):
    return y * c + pltpu.roll(y, half, 1) * s_dn + pltpu.roll(y, LANES - half, 1) * s_up


def _inproj_kernel(x_ref, gmix_ref, w_ref, bmisc_ref, tri_ref, bd64_ref, ones_ref,
                   gqa_ref, gka_ref, gcq_ref, gckv_ref, wuq_ref, wuk_ref, wuv_ref, rep_ref,
                   gqb_ref, gkb_ref, gqc_ref, gkc_ref,
                   cm_ref, sdm_ref, sum_ref, cd_ref, sdd_ref, sud_ref,
                   qa_ref, ka_ref, va_ref, c_ref, qb_ref, kb_ref, vb_ref, qc_ref, kc_ref, vc_ref,
                   carry_ref, *, tiles_per_batch):
    i = pl.program_id(0)
    x = x_ref[...]
    h = _rms_rows(x, gmix_ref[...]).astype(BF16)
    bd64 = bd64_ref[...]
    ones = ones_ref[...]

    def seg(a, b):
        return _dot(h, w_ref[:, a:b])

    o = 0
    qa = seg(o, o + FOX_WIDTH); o += FOX_WIDTH
    ka = seg(o, o + FOX_WIDTH); o += FOX_WIDTH
    va = seg(o, o + FOX_WIDTH); o += FOX_WIDTH
    cq = seg(o, o + MLA_Q_RANK); o += MLA_Q_RANK
    ckv = seg(o, o + MLA_KV_RANK); o += MLA_KV_RANK
    qc = seg(o, o + DIL_WIDTH); o += DIL_WIDTH
    kc = seg(o, o + DIL_WIDTH); o += DIL_WIDTH
    vc = seg(o, o + DIL_WIDTH); o += DIL_WIDTH
    misc = seg(o, o + LANES)

    qa_n = jnp.concatenate(_group_norm_tiles(qa, bd64, 1.0 / HEAD_DIM), axis=1) * gqa_ref[...]
    ka_n = jnp.concatenate(_group_norm_tiles(ka, bd64, 1.0 / HEAD_DIM), axis=1) * gka_ref[...]
    qa_ref[...] = qa_n.astype(BF16)
    ka_ref[...] = ka_n.astype(BF16)
    va_ref[...] = va.astype(BF16)

    z = misc + bmisc_ref[...]
    logf = jnp.minimum(z, 0.0) - jnp.log(1.0 + jnp.exp(-jnp.abs(z)))
    lane = lax.broadcasted_iota(jnp.int32, logf.shape, 1)
    logf = jnp.where(lane < FOX_HEADS, logf, 0.0)
    p0 = logf.astype(BF16)
    r0 = logf - p0.astype(F32)
    p1 = r0.astype(BF16)
    p2 = (r0 - p1.astype(F32)).astype(BF16)
    tri = tri_ref[...]

    @pl.when(i % tiles_per_batch == 0)
    def _():
        carry_ref[...] = jnp.zeros_like(carry_ref)

    csum = _dot(tri, p0) + _dot(tri, p1) + _dot(tri, p2) + carry_ref[...]
    c_ref[...] = csum
    carry_ref[...] = csum[csum.shape[0] - 1:, :]

    cqn = _rms_rows(cq, gcq_ref[...]).astype(BF16)
    ckvn = _rms_rows(ckv, gckv_ref[...]).astype(BF16)
    qb_pre = _dot(cqn, wuq_ref[...])
    kb_pre = _dot(ckvn, wuk_ref[...]) + _dot(misc.astype(BF16), rep_ref[...])
    vb_ref[...] = _dot(ckvn, wuv_ref[...]).astype(BF16)
    cm, sdm, sum_ = cm_ref[...], sdm_ref[...], sum_ref[...]
    half_b = MLA_ROPE_DIM // 2
    qb_t = _group_norm_tiles(qb_pre, ones, 1.0 / MLA_QK_DIM)
    kb_t = _group_norm_tiles(kb_pre, ones, 1.0 / MLA_QK_DIM)
    for j in range(MLA_HEADS):
        sl = slice(j * LANES, (j + 1) * LANES)
        qj = qb_t[j] * gqb_ref[:, sl]
        kj = kb_t[j] * gkb_ref[:, sl]
        qb_ref[:, sl] = _rope_tile(qj, half_b, cm, sdm, sum_).astype(BF16)
        kb_ref[:, sl] = _rope_tile(kj, half_b, cm, sdm, sum_).astype(BF16)

    cd, sdd, sud = cd_ref[...], sdd_ref[...], sud_ref[...]
    half_c = ROT_DIM // 2
    qc_t = _group_norm_tiles(qc, bd64, 1.0 / HEAD_DIM)
    kc_t = _group_norm_tiles(kc, bd64, 1.0 / HEAD_DIM)
    for j in range(DIL_WIDTH // LANES):
        sl = slice(j * LANES, (j + 1) * LANES)
        qc_ref[:, sl] = _rope_tile(qc_t[j] * gqc_ref[:, sl], half_c, cd, sdd, sud)
        kc_ref[:, sl] = _rope_tile(kc_t[j] * gkc_ref[:, sl], half_c, cd, sdd, sud)
    vc_ref[...] = vc


def _inproj(x2d, prm, tabs, seq, tm):
    T, D = x2d.shape
    nm = prm["w_main"].shape[1]
    row = lambda w: pl.BlockSpec((tm, w), lambda i: (i, 0))
    in_specs = [row(D), _full((1, D)), _full((D, nm)), _full((1, LANES)), _full((tm, tm)),
                _full((LANES, LANES)), _full((LANES, LANES)),
                _full((1, FOX_WIDTH)), _full((1, FOX_WIDTH)), _full((1, MLA_Q_RANK)), _full((1, MLA_KV_RANK)),
                _full((MLA_Q_RANK, MLA_HEADS * MLA_PAD)), _full((MLA_KV_RANK, MLA_HEADS * MLA_PAD)),
                _full((MLA_KV_RANK, MLA_WIDTH)), _full((LANES, MLA_HEADS * MLA_PAD)),
                _full((1, MLA_HEADS * MLA_PAD)), _full((1, MLA_HEADS * MLA_PAD)),
                _full((1, DIL_WIDTH)), _full((1, DIL_WIDTH))] + [row(LANES)] * 6
    outs = [(FOX_WIDTH, BF16), (FOX_WIDTH, BF16), (FOX_WIDTH, BF16), (LANES, F32),
            (MLA_HEADS * MLA_PAD, BF16), (MLA_HEADS * MLA_PAD, BF16), (MLA_WIDTH, BF16),
            (DIL_WIDTH, F32), (DIL_WIDTH, F32), (DIL_WIDTH, F32)]
    return pl.pallas_call(
        functools.partial(_inproj_kernel, tiles_per_batch=seq // tm),
        out_shape=[jax.ShapeDtypeStruct((T, w), dt) for w, dt in outs],
        grid=(T // tm,),
        in_specs=in_specs,
        out_specs=[row(w) for w, _ in outs],
        scratch_shapes=[pltpu.VMEM((1, LANES), F32)],
        compiler_params=_cparams(("arbitrary",)),
        name="inproj",
    )(x2d, prm["g_mix"], prm["w_main"], prm["b_misc"], tabs["tri_incl"], tabs["bd64"], tabs["ones"],
      prm["g_qa"], prm["g_ka"], prm["g_cq"], prm["g_ckv"], prm["w_uq"], prm["w_uk"], prm["w_uv"], tabs["rep"],
      prm["g_qb"], prm["g_kb"], prm["g_qc"], prm["g_kc"],
      tabs["cm"], tabs["sdm"], tabs["sum"], tabs["cd"], tabs["sdd"], tabs["sud"])


def _causal_kernel(*refs, tq, split_heads, with_bias):
    if with_bias:
        q_ref, k_ref, v_ref, b_ref, o_ref = refs
    else:
        q_ref, k_ref, v_ref, o_ref = refs
        b_ref = None
    qi = pl.program_id(2)
    tk = tq
    q = q_ref[0]
    lane = lax.broadcasted_iota(jnp.int32, (tq, LANES), 1)
    lo_half = lane < HEAD_DIM
    if split_heads:
        qh = (q[:, :LANES], q[:, LANES:])
    else:
        zero = jnp.zeros_like(q)
        qh = (jnp.where(lo_half, q, zero), jnp.where(lo_half, zero, q))

    def chunk(j, carry, masked):
        m0, l0, m1, l1, acc = carry
        start = pl.multiple_of(j * tk, tk)
        v = v_ref[0, pl.ds(start, tk), :]
        if masked:
            r = lax.broadcasted_iota(jnp.int32, (tq, tk), 0)
            c = lax.broadcasted_iota(jnp.int32, (tq, tk), 1)
            keep = c <= r
        new = []
        pvs = []
        for hh, (m, l) in enumerate(((m0, l0), (m1, l1))):
            if split_heads:
                k = k_ref[0, pl.ds(start, tk), hh * LANES:(hh + 1) * LANES]
            else:
                k = k_ref[0, pl.ds(start, tk), :]
            s = _dot_nt(qh[hh], k)
            if with_bias:
                s = s + b_ref[0, 0, hh, pl.ds(j, 1), :]
            if masked:
                s = jnp.where(keep, s, NEG_BIG)
            m_new = jnp.maximum(m, jnp.max(s, axis=-1, keepdims=True))
            p = jnp.exp(s - m_new)
            alpha = jnp.exp(m - m_new)
            l_new = alpha * l + jnp.sum(p, axis=-1, keepdims=True)
            pvs.append((alpha, _dot(p.astype(BF16), v)))
            new.append((m_new, l_new))
        alpha_f = jnp.where(lo_half, pvs[0][0], pvs[1][0])
        acc = alpha_f * acc + jnp.where(lo_half, pvs[0][1], pvs[1][1])
        return new[0][0], new[0][1], new[1][0], new[1][1], acc

    init = (jnp.full((tq, 1), NEG_BIG, F32), jnp.zeros((tq, 1), F32),
            jnp.full((tq, 1), NEG_BIG, F32), jnp.zeros((tq, 1), F32),
            jnp.zeros((tq, LANES), F32))
    carry = lax.fori_loop(0, qi, lambda j, c: chunk(j, c, False), init)
    m0, l0, m1, l1, acc = chunk(qi, carry, True)
    l_f = jnp.where(lo_half, l0, l1)
    o_ref[0] = (acc / l_f).astype(o_ref.dtype)


def _causal_attention(q, k, v, kbias, *, tq, split_heads):
    B, S, _ = q.shape
    P = v.shape[2] // LANES
    wq = 2 * LANES if split_heads else LANES
    in_specs = [pl.BlockSpec((1, tq, wq), lambda b, p, i: (b, i, p)),
                pl.BlockSpec((1, S, wq), lambda b, p, i: (b, 0, p)),
                pl.BlockSpec((1, S, LANES), lambda b, p, i: (b, 0, p))]
    args = [q, k, v]
    if kbias is not None:
        in_specs.append(pl.BlockSpec((1, 1, 2, S // tq, tq), lambda b, p, i: (b, p, 0, 0, 0)))
        args.append(kbias)
    return pl.pallas_call(
        functools.partial(_causal_kernel, tq=tq, split_heads=split_heads, with_bias=kbias is not None),
        out_shape=jax.ShapeDtypeStruct((B, S, P * LANES), BF16),
        grid=(B, P, S // tq),
        in_specs=in_specs,
        out_specs=pl.BlockSpec((1, tq, LANES), lambda b, p, i: (b, i, p)),
        compiler_params=_cparams(("parallel", "parallel", "arbitrary")),
        name="causal_split" if split_heads else "causal_shared",
    )(*args)


def _dilated_kernel(q_ref, k_ref, v_ref, o_ref, acc_ref, m_ref, l_ref, *, seq, n, dilations):
    lane = lax.broadcasted_iota(jnp.int32, (1, 1, LANES), 2)
    lo_half = lane < HEAD_DIM
    row = lax.broadcasted_iota(jnp.int32, (n, n), 0)
    col = lax.broadcasted_iota(jnp.int32, (n, n), 1)
    own_keep = (col <= row)[None]
    prev_keep = (col >= row)[None]

    for pi, d in enumerate(dilations):
        L = seq // d
        nb = L // n

        def band(r, _, pi=pi, d=d, L=L, nb=nb):
            if d == 1:
                q = q_ref[0]
                k = k_ref[0]
                v = v_ref[0]
            else:
                q = q_ref[0, pl.ds(r, L, stride=d), :]
                k = k_ref[0, pl.ds(r, L, stride=d), :]
                v = v_ref[0, pl.ds(r, L, stride=d), :]
            qb = q.reshape(nb, n, LANES).astype(BF16)
            kb = k.reshape(nb, n, LANES).astype(BF16)
            vb = v.reshape(nb, n, LANES).astype(BF16)
            if nb > 1:
                zero = jnp.zeros((1, n, LANES), BF16)
                kp = jnp.concatenate([zero, kb[:nb - 1]], axis=0)
                vp = jnp.concatenate([zero, vb[:nb - 1]], axis=0)
                blk = lax.broadcasted_iota(jnp.int32, (nb, n, n), 0)
                prev_ok = jnp.logical_and(prev_keep, blk > 0)
            zq = jnp.zeros_like(qb)
            heads = []
            for hh in range(2):
                qh = jnp.where(lo_half, qb, zq) if hh == 0 else jnp.where(lo_half, zq, qb)
                s_own = jnp.einsum("bqd,bkd->bqk", qh, kb, preferred_element_type=F32)
                s_own = jnp.where(own_keep, s_own, NEG_BIG)
                m = jnp.max(s_own, axis=-1, keepdims=True)
                if nb > 1:
                    s_prev = jnp.einsum("bqd,bkd->bqk", qh, kp, preferred_element_type=F32)
                    s_prev = jnp.where(prev_ok, s_prev, NEG_BIG)
                    m = jnp.maximum(m, jnp.max(s_prev, axis=-1, keepdims=True))
                p_own = jnp.exp(s_own - m)
                l = jnp.sum(p_own, axis=-1, keepdims=True)
                pv = jnp.einsum("bqk,bkd->bqd", p_own.astype(BF16), vb, preferred_element_type=F32)
                if nb > 1:
                    p_prev = jnp.exp(s_prev - m)
                    l = l + jnp.sum(p_prev, axis=-1, keepdims=True)
                    pv = pv + jnp.einsum("bqk,bkd->bqd", p_prev.astype(BF16), vp, preferred_element_type=F32)
                heads.append((m, l, pv))
            m_f = jnp.where(lo_half, heads[0][0], heads[1][0]).reshape(L, LANES)
            l_f = jnp.where(lo_half, heads[0][1], heads[1][1]).reshape(L, LANES)
            a_f = jnp.where(lo_half, heads[0][2], heads[1][2]).reshape(L, LANES)
            if d == 1:
                acc_ref[pi] = a_f
                m_ref[pi] = m_f
                l_ref[pi] = l_f
            else:
                acc_ref[pi, pl.ds(r, L, stride=d), :] = a_f
                m_ref[pi, pl.ds(r, L, stride=d), :] = m_f
                l_ref[pi, pl.ds(r, L, stride=d), :] = l_f
            return 0

        if d == 1:
            band(0, 0)
        else:
            lax.fori_loop(0, d, band, 0)

    np_ = len(dilations)
    m_all = m_ref[0]
    for pi in range(1, np_):
        m_all = jnp.maximum(m_all, m_ref[pi])
    num = jnp.zeros((seq, LANES), F32)
    den = jnp.zeros((seq, LANES), F32)
    for pi in range(np_):
        w = jnp.exp(m_ref[pi] - m_all)
        num = num + w * acc_ref[pi]
        den = den + w * l_ref[pi]
    o_ref[0] = num / den


def _dilated_attention(qc, kc, vc, seq):
    B = qc.shape[0]
    n = DILATED_PATTERNS[0][0] // DILATED_PATTERNS[0][1]
    dil = tuple(d for _, d in DILATED_PATTERNS)
    for w, d in DILATED_PATTERNS:
        assert w // d == n and seq % (d * n) == 0
    spec = pl.BlockSpec((1, seq, LANES), lambda b, p: (b, 0, p))
    np_ = len(dil)
    return pl.pallas_call(
        functools.partial(_dilated_kernel, seq=seq, n=n, dilations=dil),
        out_shape=jax.ShapeDtypeStruct((B, seq, DIL_WIDTH), F32),
        grid=(B, DIL_WIDTH // LANES),
        in_specs=[spec, spec, spec],
        out_specs=spec,
        scratch_shapes=[pltpu.VMEM((np_, seq, LANES), F32)] * 3,
        compiler_params=_cparams(("parallel", "parallel")),
        name="dilated",
    )(qc, kc, vc)


def _memkv_kernel(mem_ref, g_ref, w_ref, gk_ref, bd64_ref, k_ref, v_ref):
    mn = _rms_rows(mem_ref[0], g_ref[...]).astype(BF16)
    kv = _dot(mn, w_ref[...])
    k = jnp.concatenate(_group_norm_tiles(kv[:, :MEM_WIDTH], bd64_ref[...], 1.0 / HEAD_DIM), axis=1)
    k_ref[0] = (k * gk_ref[...]).astype(BF16)
    v_ref[0] = kv[:, MEM_WIDTH:].astype(BF16)


def _memkv(mem, prm, tabs):
    B, M, D = mem.shape
    return pl.pallas_call(
        _memkv_kernel,
        out_shape=[jax.ShapeDtypeStruct((B, M, MEM_WIDTH), BF16)] * 2,
        grid=(B,),
        in_specs=[pl.BlockSpec((1, M, D), lambda b: (b, 0, 0)), _full((1, D)), _full((D, 2 * MEM_WIDTH)),
                  _full((1, MEM_WIDTH)), _full((LANES, LANES))],
        out_specs=[pl.BlockSpec((1, M, MEM_WIDTH), lambda b: (b, 0, 0))] * 2,
        compiler_params=_cparams(("parallel",)),
        name="memkv",
    )(mem, prm["g_mem"], prm["w_mkv"], prm["g_km"], tabs["bd64"])


def _postattn_kernel(x_ref, oa_ref, ob_ref, oc_ref, gg_ref, wout_ref, gx_ref, wmq_ref, gqm_ref, bd64_ref,
                     km_ref, vm_ref, wmo_ref, gf_ref, wrh_ref, wrl_ref, br_ref, tri_ref,
                     x2_ref, h3_ref, ti_ref, tw_ref, rk_ref, cnt_ref, carry_ref, *, n_experts):
    i = pl.program_id(0)
    tm = x_ref.shape[0]
    x = x_ref[...]
    a0, a1 = FOX_WIDTH, FOX_WIDTH + MLA_WIDTH
    ma = _rms_rows(oa_ref[...].astype(F32), gg_ref[:, :a0]).astype(BF16)
    mb = _rms_rows(ob_ref[...].astype(F32), gg_ref[:, a0:a1]).astype(BF16)
    mc = _rms_rows(oc_ref[...], gg_ref[:, a1:]).astype(BF16)
    x1 = x + _dot(ma, wout_ref[:a0, :]) + _dot(mb, wout_ref[a0:a1, :]) + _dot(mc, wout_ref[a1:, :])

    h2 = _rms_rows(x1, gx_ref[...]).astype(BF16)
    qm = _dot(h2, wmq_ref[...])
    qm_t = _group_norm_tiles(qm, bd64_ref[...], 1.0 / HEAD_DIM)
    lane = lax.broadcasted_iota(jnp.int32, (tm, LANES), 1)
    lo_half = lane < HEAD_DIM
    outs = []
    for p in range(MEM_WIDTH // LANES):
        sl = slice(p * LANES, (p + 1) * LANES)
        q2 = (qm_t[p] * gqm_ref[:, sl]).astype(BF16)
        k2 = km_ref[0, :, sl]
        v2 = vm_ref[0, :, sl]
        zero = jnp.zeros_like(q2)
        res = []
        for hh in range(2):
            qh = jnp.where(lo_half, q2, zero) if hh == 0 else jnp.where(lo_half, zero, q2)
            s = _dot_nt(qh, k2)
            pr = jnp.exp(s - jnp.max(s, axis=-1, keepdims=True))
            l = jnp.sum(pr, axis=-1, keepdims=True)
            res.append(_dot(pr.astype(BF16), v2) / l)
        outs.append(jnp.where(lo_half, res[0], res[1]))
    om = jnp.concatenate(outs, axis=1).astype(BF16)
    x2 = x1 + _dot(om, wmo_ref[...])
    x2_ref[...] = x2

    h3 = _rms_rows(x2, gf_ref[...])
    h3_hi = h3.astype(BF16)
    h3_lo = (h3 - h3_hi.astype(F32)).astype(BF16)
    h3_ref[...] = _pack_bf16_pairs(h3)
    logits = _dot(h3_hi, wrh_ref[...]) + _dot(h3_hi, wrl_ref[...]) + _dot(h3_lo, wrh_ref[...]) + br_ref[...]
    logits = jnp.where(lane < n_experts, logits, NEG_BIG)
    work = logits
    lane_f = lane.astype(F32)
    sel = jnp.zeros((tm, LANES), F32)
    vals, idxs = [], []
    for _ in range(TOP_K):
        mv = jnp.max(work, axis=-1, keepdims=True)
        ix = jnp.min(jnp.where(work == mv, lane_f, float(LANES)), axis=-1, keepdims=True)
        hit = lane_f == ix
        work = jnp.where(hit, NEG_BIG, work)
        sel = jnp.where(hit, 1.0, sel)
        vals.append(mv)
        idxs.append(ix)
    es = [jnp.exp(v - vals[0]) for v in vals]
    den = es[0] + es[1] + es[2] + es[3]

    @pl.when(i == 0)
    def _():
        carry_ref[...] = jnp.zeros_like(carry_ref)

    before = _dot(tri_ref[...], sel.astype(BF16)) + carry_ref[...]
    total = before[tm - 1:, :] + sel[tm - 1:, :]
    carry_ref[...] = total
    cnt_ref[...] = total

    ti = jnp.zeros((tm, LANES), F32)
    tw = jnp.zeros((tm, LANES), F32)
    rk = jnp.zeros((tm, LANES), F32)
    for kk in range(TOP_K):
        at = lane == kk
        ti = jnp.where(at, idxs[kk], ti)
        tw = jnp.where(at, es[kk] / den, tw)
        r_k = jnp.sum(jnp.where(lane_f == idxs[kk], before, 0.0), axis=-1, keepdims=True)
        rk = jnp.where(at, r_k, rk)
    ti_ref[...] = ti.astype(jnp.int32)
    tw_ref[...] = tw
    rk_ref[...] = rk.astype(jnp.int32)


def _postattn(x2d, oa, ob, oc, kmem, vmem, prm, tabs, seq, tm, n_experts):
    T, D = x2d.shape
    M = kmem.shape[1]
    tpb = seq // tm
    row = lambda w: pl.BlockSpec((tm, w), lambda i: (i, 0))
    memspec = pl.BlockSpec((1, M, MEM_WIDTH), lambda i: (i // tpb, 0, 0))
    mix = FOX_WIDTH + MLA_WIDTH + DIL_WIDTH
    return pl.pallas_call(
        functools.partial(_postattn_kernel, n_experts=n_experts),
        out_shape=[jax.ShapeDtypeStruct((T, D), F32), jax.ShapeDtypeStruct((T, D // 2), jnp.uint32),
                   jax.ShapeDtypeStruct((T, LANES), jnp.int32), jax.ShapeDtypeStruct((T, LANES), F32),
                   jax.ShapeDtypeStruct((T, LANES), jnp.int32), jax.ShapeDtypeStruct((1, LANES), F32)],
        grid=(T // tm,),
        in_specs=[row(D), row(FOX_WIDTH), row(MLA_WIDTH), row(DIL_WIDTH), _full((1, mix)), _full((mix, D)),
                  _full((1, D)), _full((D, MEM_WIDTH)), _full((1, MEM_WIDTH)), _full((LANES, LANES)),
                  memspec, memspec, _full((MEM_WIDTH, D)), _full((1, D)),
                  _full((D, LANES)), _full((D, LANES)), _full((1, LANES)), _full((tm, tm))],
        out_specs=[row(D), row(D // 2), row(LANES), row(LANES), row(LANES), _full((1, LANES))],
        scratch_shapes=[pltpu.VMEM((1, LANES), F32)],
        compiler_params=_cparams(("arbitrary",)),
        name="postattn",
    )(x2d, oa, ob, oc, prm["g_group"], prm["w_out"], prm["g_xmem"], prm["w_mq"], prm["g_qm"], tabs["bd64"],
      kmem, vmem, prm["w_mo"], prm["g_ffn"], prm["w_r_hi"], prm["w_r_lo"], prm["b_router"], tabs["tri_strict"])


def _row_copy(src_ref, src_row, dst_ref, dst_row, sem):
    return pltpu.make_async_copy(src_ref.at[pl.ds(src_row, 1), :], dst_ref.at[pl.ds(dst_row, 1), :], sem)


def _dispatch_kernel(pos_ref, h_ref, xs_in_ref, xs_ref, sem, *, tm):
    del xs_in_ref

    def issue(r, _):
        for kk in range(TOP_K):
            _row_copy(h_ref, r, xs_ref, pos_ref[0, 0, r * TOP_K + kk], sem).start()
        return 0
    lax.fori_loop(0, tm, issue, 0)

    def drain(r, _):
        _row_copy(h_ref, 0, xs_ref, 0, sem).wait()
        return 0
    lax.fori_loop(0, tm * TOP_K, drain, 0)


def _dispatch(h3p, pos, n_rows, tm):
    T, W = h3p.shape
    xs0 = jnp.zeros((n_rows, W), h3p.dtype)
    return pl.pallas_call(
        functools.partial(_dispatch_kernel, tm=tm),
        out_shape=jax.ShapeDtypeStruct((n_rows, W), h3p.dtype),
        grid=(T // tm,),
        in_specs=[pl.BlockSpec((1, 1, tm * TOP_K), lambda i: (i, 0, 0), memory_space=pltpu.SMEM),
                  pl.BlockSpec((tm, W), lambda i: (i, 0)),
                  pl.BlockSpec(memory_space=pl.ANY)],
        out_specs=pl.BlockSpec(memory_space=pl.ANY),
        input_output_aliases={2: 0},
        scratch_shapes=[pltpu.SemaphoreType.DMA],
        compiler_params=_cparams(("arbitrary",)),
        name="dispatch",
    )(pos.reshape(T // tm, 1, tm * TOP_K), h3p, xs0)


def _experts_kernel(te_ref, xs_ref, w1g_ref, w1l_ref, b1g_ref, b1l_ref, w2_ref, b2_ref, ys_ref):
    del te_ref
    x = _unpack_bf16_pairs(xs_ref[...]).astype(BF16)
    glu = _dot(x, w1g_ref[0]) + b1g_ref[0]
    lin = _dot(x, w1l_ref[0]) + b1l_ref[0]
    glu = jnp.minimum(glu, SWIGLU_LIMIT)
    lin = jnp.clip(lin, -SWIGLU_LIMIT, SWIGLU_LIMIT)
    act = glu * (1.0 / (1.0 + jnp.exp(-SWIGLU_ALPHA * glu))) * (lin + 1.0)
    ys_ref[...] = _pack_bf16_pairs(_dot(act.astype(BF16), w2_ref[0]) + b2_ref[0])


def _experts(tile_expert, xs, prm, tr):
    n_rows, W = xs.shape
    D, F = prm["w1g"].shape[1:]
    n_tiles = n_rows // tr
    wspec = lambda s: pl.BlockSpec((1,) + s, lambda i, te: (te[i], 0, 0))
    return pl.pallas_call(
        _experts_kernel,
        out_shape=jax.ShapeDtypeStruct((n_rows, W), jnp.uint32),
        grid_spec=pltpu.PrefetchScalarGridSpec(
            num_scalar_prefetch=1,
            grid=(n_tiles,),
            in_specs=[pl.BlockSpec((tr, W), lambda i, te: (i, 0)),
                      wspec((D, F)), wspec((D, F)), wspec((1, F)), wspec((1, F)), wspec((F, D)), wspec((1, D))],
            out_specs=pl.BlockSpec((tr, W), lambda i, te: (i, 0)),
        ),
        compiler_params=_cparams(("arbitrary",)),
        name="experts",
    )(tile_expert, xs, prm["w1g"], prm["w1l"], prm["b1g"], prm["b1l"], prm["w2"], prm["b2"])


def _combine_kernel(pos_ref, x_ref, tw_ref, ys_ref, o_ref, buf, sem, *, tm):
    def issue(r, _):
        for kk in range(TOP_K):
            _row_copy(ys_ref, pos_ref[0, 0, r * TOP_K + kk], buf.at[kk], r, sem).start()
        return 0
    lax.fori_loop(0, tm, issue, 0)

    def drain(r, _):
        _row_copy(ys_ref, 0, buf.at[0], 0, sem).wait()
        return 0
    lax.fori_loop(0, tm * TOP_K, drain, 0)

    tw = tw_ref[...]
    acc = x_ref[...]
    for kk in range(TOP_K):
        acc = acc + tw[:, kk:kk + 1] * _unpack_bf16_pairs(buf[kk])
    o_ref[...] = acc


def _combine(x2, tw, ys, pos, tm):
    T, D = x2.shape
    W = ys.shape[1]
    return pl.pallas_call(
        functools.partial(_combine_kernel, tm=tm),
        out_shape=jax.ShapeDtypeStruct((T, D), F32),
        grid=(T // tm,),
        in_specs=[pl.BlockSpec((1, 1, tm * TOP_K), lambda i: (i, 0, 0), memory_space=pltpu.SMEM),
                  pl.BlockSpec((tm, D), lambda i: (i, 0)),
                  pl.BlockSpec((tm, LANES), lambda i: (i, 0)),
                  pl.BlockSpec(memory_space=pl.ANY)],
        out_specs=pl.BlockSpec((tm, D), lambda i: (i, 0)),
        scratch_shapes=[pltpu.VMEM((TOP_K, tm, W), ys.dtype), pltpu.SemaphoreType.DMA],
        compiler_params=_cparams(("arbitrary",)),
        name="combine",
    )(pos.reshape(T // tm, 1, tm * TOP_K), x2, tw, ys)


def _tile_gain(g, reps, scale=1.0):
    return (jnp.tile(g.astype(F32), reps) * scale)[None, :]


def _prep_layer(l, D, w_in, b_forget, g_mix, g_qa, g_ka, g_cq, g_ckv, w_uq, w_ukv, g_qb, g_kb, g_qc, g_kc,
                g_group, w_out, g_xmem, g_mem, w_mq, w_mkv, g_qm, g_km, w_mo, g_ffn, w_router, b_router,
                w_e1, b_e1, w_e2, b_e2):
    wi = w_in[l]
    sizes = (FOX_WIDTH, FOX_WIDTH, FOX_WIDTH, FOX_HEADS, MLA_Q_RANK, MLA_KV_RANK, MLA_ROPE_DIM,
             DIL_WIDTH, DIL_WIDTH, DIL_WIDTH)
    pts = np.cumsum((0,) + sizes)
    qa, ka, va, fa, cq, ckv, kr, qc, kc, vc = [wi[:, pts[j]:pts[j + 1]] for j in range(len(sizes))]
    misc = jnp.zeros((D, LANES), F32).at[:, :FOX_HEADS].set(fa).at[:, 32:32 + MLA_ROPE_DIM].set(kr)
    w_main = jnp.concatenate([qa, ka, va, cq, ckv, qc, kc, vc, misc], axis=1).astype(BF16)
    b_misc = jnp.zeros((1, LANES), F32).at[0, :FOX_HEADS].set(b_forget[l])

    uq = w_uq[l].reshape(MLA_Q_RANK, MLA_HEADS, MLA_QK_DIM)
    uq = jnp.pad(uq, ((0, 0), (0, 0), (0, MLA_PAD - MLA_QK_DIM))).reshape(MLA_Q_RANK, MLA_HEADS * MLA_PAD)
    ukv = w_ukv[l].reshape(MLA_KV_RANK, MLA_HEADS, MLA_NOPE_DIM + MLA_V_DIM)
    uk = jnp.pad(ukv[:, :, :MLA_NOPE_DIM], ((0, 0), (0, 0), (0, MLA_PAD - MLA_NOPE_DIM)))
    uk = uk.reshape(MLA_KV_RANK, MLA_HEADS * MLA_PAD)
    uv = ukv[:, :, MLA_NOPE_DIM:].reshape(MLA_KV_RANK, MLA_WIDTH)
    pad_gain = lambda g: jnp.pad(g.astype(F32), (0, MLA_PAD - MLA_QK_DIM))

    F = w_e2.shape[2]
    E = w_e1.shape[1]
    wr = jnp.zeros((D, LANES), F32).at[:, :E].set(w_router[l])
    wr_hi = wr.astype(BF16)
    wr_lo = (wr - wr_hi.astype(F32)).astype(BF16)
    return dict(
        g_mix=g_mix[l][None], w_main=w_main, b_misc=b_misc,
        g_qa=_tile_gain(g_qa[l], FOX_HEADS, HEAD_DIM ** -0.5), g_ka=_tile_gain(g_ka[l], FOX_HEADS),
        g_cq=g_cq[l][None], g_ckv=g_ckv[l][None],
        w_uq=uq.astype(BF16), w_uk=uk.astype(BF16), w_uv=uv.astype(BF16),
        g_qb=_tile_gain(pad_gain(g_qb[l]), MLA_HEADS, MLA_QK_DIM ** -0.5),
        g_kb=_tile_gain(pad_gain(g_kb[l]), MLA_HEADS),
        g_qc=_tile_gain(g_qc[l], DIL_HEADS, HEAD_DIM ** -0.5), g_kc=_tile_gain(g_kc[l], DIL_HEADS),
        g_group=g_group[l][None], w_out=w_out[l].astype(BF16),
        g_xmem=g_xmem[l][None], g_mem=g_mem[l][None],
        w_mq=w_mq[l].astype(BF16), w_mkv=w_mkv[l].astype(BF16),
        g_qm=_tile_gain(g_qm[l], MEM_HEADS, HEAD_DIM ** -0.5), g_km=_tile_gain(g_km[l], MEM_HEADS),
        w_mo=w_mo[l].astype(BF16), g_ffn=g_ffn[l][None],
        w_r_hi=wr_hi, w_r_lo=wr_lo,
        b_router=jnp.zeros((1, LANES), F32).at[0, :E].set(b_router[l]),
        w1g=w_e1[l][:, :, 0::2].astype(BF16), w1l=w_e1[l][:, :, 1::2].astype(BF16),
        b1g=b_e1[l][:, None, 0::2], b1l=b_e1[l][:, None, 1::2],
        w2=w_e2[l].astype(BF16), b2=b_e2[l][:, None, :],
    )


def _rope_lane_tables(positions, rot_dim, period, offset):
    half = rot_dim // 2
    inv = ROPE_THETA ** (-jnp.arange(0, rot_dim, 2, dtype=F32) / rot_dim)
    ang = positions.astype(F32).reshape(-1)[:, None] * inv
    cos, sin = jnp.cos(ang), jnp.sin(ang)
    lane = np.arange(LANES) % period - offset
    idx = np.clip(lane % half if half else lane, 0, half - 1)
    first = (lane >= 0) & (lane < half)
    second = (lane >= half) & (lane < rot_dim)
    c = jnp.where(first | second, cos[:, idx], 1.0)
    s_dn = jnp.where(second, sin[:, idx], 0.0)
    s_up = jnp.where(first, -sin[:, idx], 0.0)
    return c, s_dn, s_up


def _tables(positions, tm):
    r = np.arange(tm)
    bd = (np.arange(LANES)[:, None] // HEAD_DIM) == (np.arange(LANES)[None, :] // HEAD_DIM)
    rep = np.zeros((LANES, MLA_HEADS * MLA_PAD), np.float32)
    for h in range(MLA_HEADS):
        for j in range(MLA_ROPE_DIM):
            rep[32 + j, h * MLA_PAD + MLA_NOPE_DIM + j] = 1.0
    cm, sdm, sum_ = _rope_lane_tables(positions, MLA_ROPE_DIM, LANES, MLA_NOPE_DIM)
    cd, sdd, sud = _rope_lane_tables(positions, ROT_DIM, HEAD_DIM, 0)
    return dict(
        tri_incl=jnp.asarray(r[:, None] >= r[None, :], BF16),
        tri_strict=jnp.asarray(r[:, None] > r[None, :], BF16),
        bd64=jnp.asarray(bd, BF16), ones=jnp.ones((LANES, LANES), BF16), rep=jnp.asarray(rep, BF16),
        cm=cm, sdm=sdm, sum=sum_, cd=cd, sdd=sdd, sud=sud,
    )


def _routing_tables(ti, rk, cnt, n_experts, tr, n_tiles):
    counts = cnt[0, :n_experts].astype(jnp.int32)
    padded = ((counts + tr - 1) // tr) * tr
    ends = jnp.cumsum(padded)
    starts = ends - padded
    idx = ti[:, :TOP_K]
    pos = starts[idx] + rk[:, :TOP_K]
    tile_start = jnp.arange(n_tiles, dtype=jnp.int32) * tr
    tile_expert = jnp.minimum(jnp.searchsorted(ends, tile_start, side="right"), n_experts - 1)
    return pos.astype(jnp.int32), tile_expert.astype(jnp.int32)


def _pick_tile(seq, want):
    t = min(want, seq)
    while seq % t:
        t //= 2
    return t


def kernel(x, mem, positions, g_mix, w_in, b_forget, g_qa, g_ka, g_cq, g_ckv, w_uq, w_ukv, g_qb, g_kb, g_qc, g_kc, g_group, w_out, g_xmem, g_mem, w_mq, w_mkv, g_qm, g_km, w_mo, g_ffn, w_router, b_router, w_e1, b_e1, w_e2, b_e2):
    B, S, D = x.shape
    T = B * S
    depth = w_in.shape[0]
    E = w_e1.shape[1]
    tm = _pick_tile(S, PROJ_TILE)
    tq = _pick_tile(S, ATTN_TILE)
    tr = EXPERT_ROW_TILE
    tmoe = _pick_tile(S, MOE_TOKEN_TILE)
    n_rows = T * TOP_K + E * tr
    n_tiles = n_rows // tr

    tabs = _tables(positions, tm)
    x2d = x.reshape(T, D)
    for l in range(depth):
        prm = _prep_layer(l, D, w_in, b_forget, g_mix, g_qa, g_ka, g_cq, g_ckv, w_uq, w_ukv, g_qb, g_kb,
                          g_qc, g_kc, g_group, w_out, g_xmem, g_mem, w_mq, w_mkv, g_qm, g_km, w_mo, g_ffn,
                          w_router, b_router, w_e1, b_e1, w_e2, b_e2)
        qa, ka, va, csum, qb, kb, vb, qc, kc, vc = _inproj(x2d, prm, tabs, S, tm)
        kbias = -csum[:, :FOX_HEADS].reshape(B, S, FOX_HEADS // 2, 2).transpose(0, 2, 3, 1)
        kbias = kbias.reshape(B, FOX_HEADS // 2, 2, S // tq, tq)
        r3 = lambda a: a.reshape(B, S, a.shape[-1])
        oa = _causal_attention(r3(qa), r3(ka), r3(va), kbias, tq=tq, split_heads=False)
        ob = _causal_attention(r3(qb), r3(kb), r3(vb), None, tq=tq, split_heads=True)
        oc = _dilated_attention(r3(qc), r3(kc), r3(vc), S)
        kmem, vmem = _memkv(mem, prm, tabs)
        x2, h3, ti, tw, rk, cnt = _postattn(x2d, oa.reshape(T, -1), ob.reshape(T, -1), oc.reshape(T, -1),
                                            kmem, vmem, prm, tabs, S, tm, E)
        pos, tile_expert = _routing_tables(ti, rk, cnt, E, tr, n_tiles)
        xs = _dispatch(h3, pos, n_rows, tmoe)
        ys = _experts(tile_expert, xs, prm, tr)
        x2d = _combine(x2, tw, ys, pos, tmoe)
    return x2d.reshape(B, S, D)
```

```python
import functools
import math

import jax
import jax.numpy as jnp
import numpy as np
from jax import lax
from jax.experimental import pallas as pl
from jax.experimental.pallas import tpu as pltpu

HEAD_DIM = 64
FOX_HEADS = 4
MLA_HEADS = 6
DIL_HEADS = 6
MLA_Q_RANK = 384
MLA_KV_RANK = 256
MLA_NOPE_DIM = 64
MLA_ROPE_DIM = 32
MLA_QK_DIM = MLA_NOPE_DIM + MLA_ROPE_DIM
MLA_V_DIM = 64
ROT_DIM = HEAD_DIM // 4
ROPE_THETA = 500000.0
DILATED_PATTERNS = ((128, 1), (512, 4), (2048, 16))
FOX_WIDTH = FOX_HEADS * HEAD_DIM
MLA_WIDTH = MLA_HEADS * MLA_V_DIM
DIL_WIDTH = DIL_HEADS * HEAD_DIM
MEM_HEADS = 4
MEM_WIDTH = MEM_HEADS * HEAD_DIM
TOP_K = 4
SWIGLU_ALPHA = 1.702
SWIGLU_LIMIT = 7.0
RMS_EPS = 1e-6

LANES = 128
VMEM_LIMIT_BYTES = 56 * 1024 * 1024

PROJ_TILE = 512
ATTN_TILE = 256
EXPERT_ROW_TILE = 256
MOE_TOKEN_TILE = 256
ROW_DMA_UNROLL = 8

NEG_BIG = -1e30
MLA_PAD = 128
DEINT_BLOCK = 256

F32 = jnp.float32
BF16 = jnp.bfloat16


def _cparams(sem):
    return pltpu.CompilerParams(dimension_semantics=sem, vmem_limit_bytes=VMEM_LIMIT_BYTES)


def _full(shape):
    nd = len(shape)
    return pl.BlockSpec(shape, lambda *_: (0,) * nd)


def _dot(a, b):
    return jnp.dot(a, b, preferred_element_type=F32)


def _dot_nt(a, b):
    return lax.dot_general(a, b, (((1,), (1,)), ((), ())), preferred_element_type=F32)


def _rms_rows(x, g):
    return x * lax.rsqrt(jnp.mean(x * x, axis=-1, keepdims=True) + RMS_EPS) * g


def _group_norm_tiles(y, ones_mat, inv_n):
    outs = []
    for j in range(y.shape[1] // LANES):
        yj = y[:, j * LANES:(j + 1) * LANES]
        ss = _dot((yj * yj).astype(BF16), ones_mat)
        outs.append(yj * lax.rsqrt(ss * inv_n + RMS_EPS))
    return outs


def _pack_bf16_pairs(x):
    w = x.shape[1] // 2
    hi = lax.bitcast_convert_type(x[:, :w].astype(jnp.bfloat16).astype(F32), jnp.uint32)
    lo = lax.bitcast_convert_type(x[:, w:].astype(jnp.bfloat16).astype(F32), jnp.uint32)
    return hi | (lo >> 16)


def _unpack_bf16_pairs(p):
    hi = lax.bitcast_convert_type(p & jnp.uint32(0xFFFF0000), F32)
    lo = lax.bitcast_convert_type(p << 16, F32)
    return jnp.concatenate([hi, lo], axis=1)


def _rope_tile(y, half, c, s_dn, s_up):
    return y * c + pltpu.roll(y, half, 1) * s_dn + pltpu.roll(y, LANES - half, 1) * s_up


def _inproj_kernel(x_ref, gmix_ref, w_ref, bmisc_ref, tri_ref, bd64_ref, ones_ref,
                   gqa_ref, gka_ref, gcq_ref, gckv_ref, wuq_ref, wuk_ref, wuv_ref, rep_ref,
                   gqb_ref, gkb_ref, gqc_ref, gkc_ref,
                   cm_ref, sdm_ref, sum_ref, cd_ref, sdd_ref, sud_ref,
                   qa_ref, ka_ref, va_ref, c_ref, qb_ref, kb_ref, vb_ref, qc_ref, kc_ref, vc_ref,
                   carry_ref, *, tiles_per_batch):
    i = pl.program_id(0)
    x = x_ref[...]
    h = _rms_rows(x, gmix_ref[...]).astype(BF16)
    bd64 = bd64_ref[...]
    ones = ones_ref[...]

    def seg(a, b):
        return _dot(h, w_ref[:, a:b])

    o = 0
    qa = seg(o, o + FOX_WIDTH); o += FOX_WIDTH
    ka = seg(o, o + FOX_WIDTH); o += FOX_WIDTH
    va = seg(o, o + FOX_WIDTH); o += FOX_WIDTH
    cq = seg(o, o + MLA_Q_RANK); o += MLA_Q_RANK
    ckv = seg(o, o + MLA_KV_RANK); o += MLA_KV_RANK
    qc = seg(o, o + DIL_WIDTH); o += DIL_WIDTH
    kc = seg(o, o + DIL_WIDTH); o += DIL_WIDTH
    vc = seg(o, o + DIL_WIDTH); o += DIL_WIDTH
    misc = seg(o, o + LANES)

    qa_n = jnp.concatenate(_group_norm_tiles(qa, bd64, 1.0 / HEAD_DIM), axis=1) * gqa_ref[...]
    ka_n = jnp.concatenate(_group_norm_tiles(ka, bd64, 1.0 / HEAD_DIM), axis=1) * gka_ref[...]
    qa_ref[...] = qa_n.astype(BF16)
    ka_ref[...] = ka_n.astype(BF16)
    va_ref[...] = va.astype(BF16)

    z = misc + bmisc_ref[...]
    logf = jnp.minimum(z, 0.0) - jnp.log(1.0 + jnp.exp(-jnp.abs(z)))
    lane = lax.broadcasted_iota(jnp.int32, logf.shape, 1)
    logf = jnp.where(lane < FOX_HEADS, logf, 0.0)
    p0 = logf.astype(BF16)
    r0 = logf - p0.astype(F32)
    p1 = r0.astype(BF16)
    p2 = (r0 - p1.astype(F32)).astype(BF16)
    tri = tri_ref[...]

    @pl.when(i % tiles_per_batch == 0)
    def _():
        carry_ref[...] = jnp.zeros_like(carry_ref)

    csum = _dot(tri, p0) + _dot(tri, p1) + _dot(tri, p2) + carry_ref[...]
    c_ref[...] = csum
    carry_ref[...] = csum[csum.shape[0] - 1:, :]

    cqn = _rms_rows(cq, gcq_ref[...]).astype(BF16)
    ckvn = _rms_rows(ckv, gckv_ref[...]).astype(BF16)
    qb_pre = _dot(cqn, wuq_ref[...])
    kb_pre = _dot(ckvn, wuk_ref[...]) + _dot(misc.astype(BF16), rep_ref[...])
    vb_ref[...] = _dot(ckvn, wuv_ref[...]).astype(BF16)
    cm, sdm, sum_ = cm_ref[...], sdm_ref[...], sum_ref[...]
    half_b = MLA_ROPE_DIM // 2
    qb_t = _group_norm_tiles(qb_pre, ones, 1.0 / MLA_QK_DIM)
    kb_t = _group_norm_tiles(kb_pre, ones, 1.0 / MLA_QK_DIM)
    for j in range(MLA_HEADS):
        sl = slice(j * LANES, (j + 1) * LANES)
        qj = qb_t[j] * gqb_ref[:, sl]
        kj = kb_t[j] * gkb_ref[:, sl]
        qb_ref[:, sl] = _rope_tile(qj, half_b, cm, sdm, sum_).astype(BF16)
        kb_ref[:, sl] = _rope_tile(kj, half_b, cm, sdm, sum_).astype(BF16)

    cd, sdd, sud = cd_ref[...], sdd_ref[...], sud_ref[...]
    half_c = ROT_DIM // 2
    qc_t = _group_norm_tiles(qc, bd64, 1.0 / HEAD_DIM)
    kc_t = _group_norm_tiles(kc, bd64, 1.0 / HEAD_DIM)
    for j in range(DIL_WIDTH // LANES):
        sl = slice(j * LANES, (j + 1) * LANES)
        qc_ref[:, sl] = _rope_tile(qc_t[j] * gqc_ref[:, sl], half_c, cd, sdd, sud)
        kc_ref[:, sl] = _rope_tile(kc_t[j] * gkc_ref[:, sl], half_c, cd, sdd, sud)
    vc_ref[...] = vc


def _inproj(x2d, prm, tabs, seq, tm):
    T, D = x2d.shape
    nm = prm["w_main"].shape[1]
    row = lambda w: pl.BlockSpec((tm, w), lambda i: (i, 0))
    in_specs = [row(D), _full((1, D)), _full((D, nm)), _full((1, LANES)), _full((tm, tm)),
                _full((LANES, LANES)), _full((LANES, LANES)),
                _full((1, FOX_WIDTH)), _full((1, FOX_WIDTH)), _full((1, MLA_Q_RANK)), _full((1, MLA_KV_RANK)),
                _full((MLA_Q_RANK, MLA_HEADS * MLA_PAD)), _full((MLA_KV_RANK, MLA_HEADS * MLA_PAD)),
                _full((MLA_KV_RANK, MLA_WIDTH)), _full((LANES, MLA_HEADS * MLA_PAD)),
                _full((1, MLA_HEADS * MLA_PAD)), _full((1, MLA_HEADS * MLA_PAD)),
                _full((1, DIL_WIDTH)), _full((1, DIL_WIDTH))] + [row(LANES)] * 6
    outs = [(FOX_WIDTH, BF16), (FOX_WIDTH, BF16), (FOX_WIDTH, BF16), (LANES, F32),
            (MLA_HEADS * MLA_PAD, BF16), (MLA_HEADS * MLA_PAD, BF16), (MLA_WIDTH, BF16),
            (DIL_WIDTH, F32), (DIL_WIDTH, F32), (DIL_WIDTH, F32)]
    return pl.pallas_call(
        functools.partial(_inproj_kernel, tiles_per_batch=seq // tm),
        out_shape=[jax.ShapeDtypeStruct((T, w), dt) for w, dt in outs],
        grid=(T // tm,),
        in_specs=in_specs,
        out_specs=[row(w) for w, _ in outs],
        scratch_shapes=[pltpu.VMEM((1, LANES), F32)],
        compiler_params=_cparams(("arbitrary",)),
        name="inproj",
    )(x2d, prm["g_mix"], prm["w_main"], prm["b_misc"], tabs["tri_incl"], tabs["bd64"], tabs["ones"],
      prm["g_qa"], prm["g_ka"], prm["g_cq"], prm["g_ckv"], prm["w_uq"], prm["w_uk"], prm["w_uv"], tabs["rep"],
      prm["g_qb"], prm["g_kb"], prm["g_qc"], prm["g_kc"],
      tabs["cm"], tabs["sdm"], tabs["sum"], tabs["cd"], tabs["sdd"], tabs["sud"])


def _causal_kernel(*refs, seq, tq, split_heads, with_bias):
    if with_bias:
        q_ref, k_ref, v_ref, b_ref, o_ref = refs
    else:
        q_ref, k_ref, v_ref, o_ref = refs
        b_ref = None
    lane = lax.broadcasted_iota(jnp.int32, (tq, LANES), 1)
    lo_half = lane < HEAD_DIM
    r = lax.broadcasted_iota(jnp.int32, (tq, tq), 0)
    c = lax.broadcasted_iota(jnp.int32, (tq, tq), 1)
    keep = c <= r
    for qi in range(seq // tq):
        q = q_ref[0, qi * tq:(qi + 1) * tq, :]
        n_keys = (qi + 1) * tq
        v = v_ref[0, :n_keys, :]
        res = []
        for hh in range(2):
            if split_heads:
                qh = q[:, hh * LANES:(hh + 1) * LANES]
                k = k_ref[0, :n_keys, hh * LANES:(hh + 1) * LANES]
            else:
                zero = jnp.zeros_like(q)
                qh = jnp.where(lo_half, q, zero) if hh == 0 else jnp.where(lo_half, zero, q)
                k = k_ref[0, :n_keys, :]
            s = _dot_nt(qh, k)
            if with_bias:
                s = s + b_ref[0, 0, hh:hh + 1, :n_keys]
            s_diag = jnp.where(keep, s[:, n_keys - tq:], NEG_BIG)
            s = jnp.concatenate([s[:, :n_keys - tq], s_diag], axis=1) if qi else s_diag
            p = jnp.exp(s - jnp.max(s, axis=-1, keepdims=True))
            l = jnp.sum(p, axis=-1, keepdims=True)
            res.append(_dot(p.astype(BF16), v) / l)
        o_ref[0, qi * tq:(qi + 1) * tq, :] = jnp.where(lo_half, res[0], res[1]).astype(o_ref.dtype)


def _causal_attention(q, k, v, kbias, *, tq, split_heads):
    B, S, _ = q.shape
    P = v.shape[2] // LANES
    wq = 2 * LANES if split_heads else LANES
    in_specs = [pl.BlockSpec((1, S, wq), lambda b, p: (b, 0, p)),
                pl.BlockSpec((1, S, wq), lambda b, p: (b, 0, p)),
                pl.BlockSpec((1, S, LANES), lambda b, p: (b, 0, p))]
    args = [q, k, v]
    if kbias is not None:
        in_specs.append(pl.BlockSpec((1, 1, 2, S), lambda b, p: (b, p, 0, 0)))
        args.append(kbias)
    return pl.pallas_call(
        functools.partial(_causal_kernel, seq=S, tq=tq, split_heads=split_heads, with_bias=kbias is not None),
        out_shape=jax.ShapeDtypeStruct((B, S, P * LANES), BF16),
        grid=(B, P),
        in_specs=in_specs,
        out_specs=pl.BlockSpec((1, S, LANES), lambda b, p: (b, 0, p)),
        compiler_params=_cparams(("parallel", "parallel")),
        name="causal_split" if split_heads else "causal_shared",
    )(*args)


def _dilated_kernel(q_ref, k_ref, v_ref, o_ref, acc_ref, m_ref, l_ref, *, seq, n, dilations):
    lane = lax.broadcasted_iota(jnp.int32, (1, 1, LANES), 2)
    lo_half = lane < HEAD_DIM
    row = lax.broadcasted_iota(jnp.int32, (n, n), 0)
    col = lax.broadcasted_iota(jnp.int32, (n, n), 1)
    own_keep = (col <= row)[None]
    prev_keep = (col >= row)[None]

    for pi, d in enumerate(dilations):
        L = seq // d
        nb = L // n

        def band(r, _, pi=pi, d=d, L=L, nb=nb):
            if d == 1:
                q = q_ref[0]
                k = k_ref[0]
                v = v_ref[0]
            else:
                q = q_ref[0, pl.ds(r, L, stride=d), :]
                k = k_ref[0, pl.ds(r, L, stride=d), :]
                v = v_ref[0, pl.ds(r, L, stride=d), :]
            qb = q.reshape(nb, n, LANES).astype(BF16)
            kb = k.reshape(nb, n, LANES).astype(BF16)
            vb = v.reshape(nb, n, LANES).astype(BF16)
            if nb > 1:
                zero = jnp.zeros((1, n, LANES), BF16)
                kp = jnp.concatenate([zero, kb[:nb - 1]], axis=0)
                vp = jnp.concatenate([zero, vb[:nb - 1]], axis=0)
                blk = lax.broadcasted_iota(jnp.int32, (nb, n, n), 0)
                prev_ok = jnp.logical_and(prev_keep, blk > 0)
            zq = jnp.zeros_like(qb)
            heads = []
            for hh in range(2):
                qh = jnp.where(lo_half, qb, zq) if hh == 0 else jnp.where(lo_half, zq, qb)
                s_own = jnp.einsum("bqd,bkd->bqk", qh, kb, preferred_element_type=F32)
                s_own = jnp.where(own_keep, s_own, NEG_BIG)
                m = jnp.max(s_own, axis=-1, keepdims=True)
                if nb > 1:
                    s_prev = jnp.einsum("bqd,bkd->bqk", qh, kp, preferred_element_type=F32)
                    s_prev = jnp.where(prev_ok, s_prev, NEG_BIG)
                    m = jnp.maximum(m, jnp.max(s_prev, axis=-1, keepdims=True))
                p_own = jnp.exp(s_own - m)
                l = jnp.sum(p_own, axis=-1, keepdims=True)
                pv = jnp.einsum("bqk,bkd->bqd", p_own.astype(BF16), vb, preferred_element_type=F32)
                if nb > 1:
                    p_prev = jnp.exp(s_prev - m)
                    l = l + jnp.sum(p_prev, axis=-1, keepdims=True)
                    pv = pv + jnp.einsum("bqk,bkd->bqd", p_prev.astype(BF16), vp, preferred_element_type=F32)
                heads.append((m, l, pv))
            m_f = jnp.where(lo_half, heads[0][0], heads[1][0]).reshape(L, LANES)
            l_f = jnp.where(lo_half, heads[0][1], heads[1][1]).reshape(L, LANES)
            a_f = jnp.where(lo_half, heads[0][2], heads[1][2]).reshape(L, LANES)
            if d == 1:
                acc_ref[pi] = a_f
                m_ref[pi] = m_f
                l_ref[pi] = l_f
            else:
                acc_ref[pi, pl.ds(r, L, stride=d), :] = a_f
                m_ref[pi, pl.ds(r, L, stride=d), :] = m_f
                l_ref[pi, pl.ds(r, L, stride=d), :] = l_f
            return 0

        if d == 1:
            band(0, 0)
        else:
            lax.fori_loop(0, d, band, 0)

    np_ = len(dilations)
    m_all = m_ref[0]
    for pi in range(1, np_):
        m_all = jnp.maximum(m_all, m_ref[pi])
    num = jnp.zeros((seq, LANES), F32)
    den = jnp.zeros((seq, LANES), F32)
    for pi in range(np_):
        w = jnp.exp(m_ref[pi] - m_all)
        num = num + w * acc_ref[pi]
        den = den + w * l_ref[pi]
    o_ref[0] = num / den


def _dilated_attention(qc, kc, vc, seq):
    B = qc.shape[0]
    n = DILATED_PATTERNS[0][0] // DILATED_PATTERNS[0][1]
    dil = tuple(d for _, d in DILATED_PATTERNS)
    for w, d in DILATED_PATTERNS:
        assert w // d == n and seq % (d * n) == 0
    spec = pl.BlockSpec((1, seq, LANES), lambda b, p: (b, 0, p))
    np_ = len(dil)
    return pl.pallas_call(
        functools.partial(_dilated_kernel, seq=seq, n=n, dilations=dil),
        out_shape=jax.ShapeDtypeStruct((B, seq, DIL_WIDTH), F32),
        grid=(B, DIL_WIDTH // LANES),
        in_specs=[spec, spec, spec],
        out_specs=spec,
        scratch_shapes=[pltpu.VMEM((np_, seq, LANES), F32)] * 3,
        compiler_params=_cparams(("parallel", "parallel")),
        name="dilated",
    )(qc, kc, vc)


def _memkv_kernel(mem_ref, g_ref, w_ref, gk_ref, bd64_ref, k_ref, v_ref):
    mn = _rms_rows(mem_ref[0], g_ref[...]).astype(BF16)
    kv = _dot(mn, w_ref[...])
    k = jnp.concatenate(_group_norm_tiles(kv[:, :MEM_WIDTH], bd64_ref[...], 1.0 / HEAD_DIM), axis=1)
    k_ref[0] = (k * gk_ref[...]).astype(BF16)
    v_ref[0] = kv[:, MEM_WIDTH:].astype(BF16)


def _memkv(mem, prm, tabs):
    B, M, D = mem.shape
    return pl.pallas_call(
        _memkv_kernel,
        out_shape=[jax.ShapeDtypeStruct((B, M, MEM_WIDTH), BF16)] * 2,
        grid=(B,),
        in_specs=[pl.BlockSpec((1, M, D), lambda b: (b, 0, 0)), _full((1, D)), _full((D, 2 * MEM_WIDTH)),
                  _full((1, MEM_WIDTH)), _full((LANES, LANES))],
        out_specs=[pl.BlockSpec((1, M, MEM_WIDTH), lambda b: (b, 0, 0))] * 2,
        compiler_params=_cparams(("parallel",)),
        name="memkv",
    )(mem, prm["g_mem"], prm["w_mkv"], prm["g_km"], tabs["bd64"])


def _postattn_kernel(x_ref, oa_ref, ob_ref, oc_ref, gg_ref, wout_ref, gx_ref, wmq_ref, gqm_ref, bd64_ref,
                     km_ref, vm_ref, wmo_ref, gf_ref, wrh_ref, wrl_ref, br_ref, tri_ref,
                     x2_ref, h3_ref, ti_ref, tw_ref, rk_ref, cnt_ref, carry_ref, *, n_experts):
    i = pl.program_id(0)
    tm = x_ref.shape[0]
    x = x_ref[...]
    a0, a1 = FOX_WIDTH, FOX_WIDTH + MLA_WIDTH
    ma = _rms_rows(oa_ref[...].astype(F32), gg_ref[:, :a0]).astype(BF16)
    mb = _rms_rows(ob_ref[...].astype(F32), gg_ref[:, a0:a1]).astype(BF16)
    mc = _rms_rows(oc_ref[...], gg_ref[:, a1:]).astype(BF16)
    x1 = x + _dot(ma, wout_ref[:a0, :]) + _dot(mb, wout_ref[a0:a1, :]) + _dot(mc, wout_ref[a1:, :])

    h2 = _rms_rows(x1, gx_ref[...]).astype(BF16)
    qm = _dot(h2, wmq_ref[...])
    qm_t = _group_norm_tiles(qm, bd64_ref[...], 1.0 / HEAD_DIM)
    lane = lax.broadcasted_iota(jnp.int32, (tm, LANES), 1)
    lo_half = lane < HEAD_DIM
    outs = []
    for p in range(MEM_WIDTH // LANES):
        sl = slice(p * LANES, (p + 1) * LANES)
        q2 = (qm_t[p] * gqm_ref[:, sl]).astype(BF16)
        k2 = km_ref[0, :, sl]
        v2 = vm_ref[0, :, sl]
        zero = jnp.zeros_like(q2)
        res = []
        for hh in range(2):
            qh = jnp.where(lo_half, q2, zero) if hh == 0 else jnp.where(lo_half, zero, q2)
            s = _dot_nt(qh, k2)
            pr = jnp.exp(s - jnp.max(s, axis=-1, keepdims=True))
            l = jnp.sum(pr, axis=-1, keepdims=True)
            res.append(_dot(pr.astype(BF16), v2) / l)
        outs.append(jnp.where(lo_half, res[0], res[1]))
    om = jnp.concatenate(outs, axis=1).astype(BF16)
    x2 = x1 + _dot(om, wmo_ref[...])
    x2_ref[...] = x2

    h3 = _rms_rows(x2, gf_ref[...])
    h3_hi = h3.astype(BF16)
    h3_lo = (h3 - h3_hi.astype(F32)).astype(BF16)
    h3_ref[...] = _pack_bf16_pairs(h3)
    logits = _dot(h3_hi, wrh_ref[...]) + _dot(h3_hi, wrl_ref[...]) + _dot(h3_lo, wrh_ref[...]) + br_ref[...]
    logits = jnp.where(lane < n_experts, logits, NEG_BIG)
    work = logits
    lane_f = lane.astype(F32)
    sel = jnp.zeros((tm, LANES), F32)
    vals, idxs = [], []
    for _ in range(TOP_K):
        mv = jnp.max(work, axis=-1, keepdims=True)
        ix = jnp.min(jnp.where(work == mv, lane_f, float(LANES)), axis=-1, keepdims=True)
        hit = lane_f == ix
        work = jnp.where(hit, NEG_BIG, work)
        sel = jnp.where(hit, 1.0, sel)
        vals.append(mv)
        idxs.append(ix)
    es = [jnp.exp(v - vals[0]) for v in vals]
    den = es[0] + es[1] + es[2] + es[3]

    @pl.when(i == 0)
    def _():
        carry_ref[...] = jnp.zeros_like(carry_ref)

    before = _dot(tri_ref[...], sel.astype(BF16)) + carry_ref[...]
    total = before[tm - 1:, :] + sel[tm - 1:, :]
    carry_ref[...] = total
    cnt_ref[...] = total

    ti = jnp.zeros((tm, LANES), F32)
    tw = jnp.zeros((tm, LANES), F32)
    rk = jnp.zeros((tm, LANES), F32)
    for kk in range(TOP_K):
        at = lane == kk
        ti = jnp.where(at, idxs[kk], ti)
        tw = jnp.where(at, es[kk] / den, tw)
        r_k = jnp.sum(jnp.where(lane_f == idxs[kk], before, 0.0), axis=-1, keepdims=True)
        rk = jnp.where(at, r_k, rk)
    ti_ref[...] = ti.astype(jnp.int32)
    tw_ref[...] = tw
    rk_ref[...] = rk.astype(jnp.int32)


def _postattn(x2d, oa, ob, oc, kmem, vmem, prm, tabs, seq, tm, n_experts):
    T, D = x2d.shape
    M = kmem.shape[1]
    tpb = seq // tm
    row = lambda w: pl.BlockSpec((tm, w), lambda i: (i, 0))
    memspec = pl.BlockSpec((1, M, MEM_WIDTH), lambda i: (i // tpb, 0, 0))
    mix = FOX_WIDTH + MLA_WIDTH + DIL_WIDTH
    return pl.pallas_call(
        functools.partial(_postattn_kernel, n_experts=n_experts),
        out_shape=[jax.ShapeDtypeStruct((T, D), F32), jax.ShapeDtypeStruct((T, D // 2), jnp.uint32),
                   jax.ShapeDtypeStruct((T, LANES), jnp.int32), jax.ShapeDtypeStruct((T, LANES), F32),
                   jax.ShapeDtypeStruct((T, LANES), jnp.int32), jax.ShapeDtypeStruct((1, LANES), F32)],
        grid=(T // tm,),
        in_specs=[row(D), row(FOX_WIDTH), row(MLA_WIDTH), row(DIL_WIDTH), _full((1, mix)), _full((mix, D)),
                  _full((1, D)), _full((D, MEM_WIDTH)), _full((1, MEM_WIDTH)), _full((LANES, LANES)),
                  memspec, memspec, _full((MEM_WIDTH, D)), _full((1, D)),
                  _full((D, LANES)), _full((D, LANES)), _full((1, LANES)), _full((tm, tm))],
        out_specs=[row(D), row(D // 2), row(LANES), row(LANES), row(LANES), _full((1, LANES))],
        scratch_shapes=[pltpu.VMEM((1, LANES), F32)],
        compiler_params=_cparams(("arbitrary",)),
        name="postattn",
    )(x2d, oa, ob, oc, prm["g_group"], prm["w_out"], prm["g_xmem"], prm["w_mq"], prm["g_qm"], tabs["bd64"],
      kmem, vmem, prm["w_mo"], prm["g_ffn"], prm["w_r_hi"], prm["w_r_lo"], prm["b_router"], tabs["tri_strict"])


def _row_copy(src_ref, src_row, dst_ref, dst_row, sem):
    return pltpu.make_async_copy(src_ref.at[pl.ds(src_row, 1), :], dst_ref.at[pl.ds(dst_row, 1), :], sem)


def _dispatch_kernel(pos_ref, h_ref, xs_in_ref, xs_ref, sem, *, tm):
    del xs_in_ref

    def issue(r, _):
        for kk in range(TOP_K):
            _row_copy(h_ref, r, xs_ref, pos_ref[0, 0, r * TOP_K + kk], sem).start()
        return 0
    lax.fori_loop(0, tm, issue, 0, unroll=ROW_DMA_UNROLL)
    n = tm * TOP_K
    pltpu.make_async_copy(xs_ref.at[pl.ds(0, n), :], xs_ref.at[pl.ds(0, n), :], sem).wait()


def _dispatch(h3p, pos, n_rows, tm):
    T, W = h3p.shape
    xs0 = jnp.zeros((n_rows, W), h3p.dtype)
    return pl.pallas_call(
        functools.partial(_dispatch_kernel, tm=tm),
        out_shape=jax.ShapeDtypeStruct((n_rows, W), h3p.dtype),
        grid=(T // tm,),
        in_specs=[pl.BlockSpec((1, 1, tm * TOP_K), lambda i: (i, 0, 0), memory_space=pltpu.SMEM),
                  pl.BlockSpec((tm, W), lambda i: (i, 0)),
                  pl.BlockSpec(memory_space=pl.ANY)],
        out_specs=pl.BlockSpec(memory_space=pl.ANY),
        input_output_aliases={2: 0},
        scratch_shapes=[pltpu.SemaphoreType.DMA],
        compiler_params=_cparams(("arbitrary",)),
        name="dispatch",
    )(pos.reshape(T // tm, 1, tm * TOP_K), h3p, xs0)


def _experts_kernel(te_ref, xs_ref, w1_ref, b1g_ref, b1l_ref, w2_ref, b2_ref, perm_ref, ys_ref, w1s_ref, w2s_ref):
    i = pl.program_id(0)
    two_f = w1_ref.shape[3]
    n_blk = two_f // DEINT_BLOCK
    half = DEINT_BLOCK // 2

    @pl.when(jnp.logical_or(i == 0, te_ref[i] != te_ref[jnp.maximum(i - 1, 0)]))
    def _():
        perm = perm_ref[...]
        for b in range(n_blk):
            sl = slice(b * DEINT_BLOCK, (b + 1) * DEINT_BLOCK)
            w1s_ref[:, sl] = _dot(w1_ref[0, 0, :, sl].astype(BF16), perm).astype(BF16)
        w2s_ref[...] = w2_ref[0, 0].astype(BF16)

    x = _unpack_bf16_pairs(xs_ref[...]).astype(BF16)
    h = _dot(x, w1s_ref[...])
    acts = []
    for b in range(n_blk):
        glu = h[:, b * DEINT_BLOCK:b * DEINT_BLOCK + half] + b1g_ref[0, :, b * half:(b + 1) * half]
        lin = h[:, b * DEINT_BLOCK + half:(b + 1) * DEINT_BLOCK] + b1l_ref[0, :, b * half:(b + 1) * half]
        glu = jnp.minimum(glu, SWIGLU_LIMIT)
        lin = jnp.clip(lin, -SWIGLU_LIMIT, SWIGLU_LIMIT)
        acts.append((glu * (1.0 / (1.0 + jnp.exp(-SWIGLU_ALPHA * glu))) * (lin + 1.0)).astype(BF16))
    act = jnp.concatenate(acts, axis=1) if n_blk > 1 else acts[0]
    ys_ref[...] = _pack_bf16_pairs(_dot(act, w2s_ref[...]) + b2_ref[0])


def _experts(tile_expert, xs, w_e1, w_e2, layer, prm, tabs, tr):
    n_rows, W = xs.shape
    _, _, D, two_f = w_e1.shape
    F = two_f // 2
    assert two_f % DEINT_BLOCK == 0
    n_tiles = n_rows // tr
    wspec = lambda s: pl.BlockSpec((1,) + s, lambda i, te: (te[i], 0, 0))
    lspec = lambda s: pl.BlockSpec((1, 1) + s, lambda i, te: (layer, te[i], 0, 0))
    return pl.pallas_call(
        _experts_kernel,
        out_shape=jax.ShapeDtypeStruct((n_rows, W), jnp.uint32),
        grid_spec=pltpu.PrefetchScalarGridSpec(
            num_scalar_prefetch=1,
            grid=(n_tiles,),
            in_specs=[pl.BlockSpec((tr, W), lambda i, te: (i, 0)),
                      lspec((D, two_f)), wspec((1, F)), wspec((1, F)), lspec((F, D)), wspec((1, D)),
                      pl.BlockSpec((DEINT_BLOCK, DEINT_BLOCK), lambda i, te: (0, 0))],
            out_specs=pl.BlockSpec((tr, W), lambda i, te: (i, 0)),
            scratch_shapes=[pltpu.VMEM((D, two_f), BF16), pltpu.VMEM((F, D), BF16)],
        ),
        compiler_params=_cparams(("arbitrary",)),
        name="experts",
    )(tile_expert, xs, w_e1, prm["b1g"], prm["b1l"], w_e2, prm["b2"], tabs["deint"])


def _combine_kernel(pos_ref, x_ref, tw_ref, ys_ref, o_ref, buf, sem, *, tm):
    def issue(r, _):
        for kk in range(TOP_K):
            _row_copy(ys_ref, pos_ref[0, 0, r * TOP_K + kk], buf.at[kk], r, sem).start()
        return 0
    lax.fori_loop(0, tm, issue, 0, unroll=ROW_DMA_UNROLL)
    pltpu.make_async_copy(buf, buf, sem).wait()

    tw = tw_ref[...]
    acc = x_ref[...]
    for kk in range(TOP_K):
        acc = acc + tw[:, kk:kk + 1] * _unpack_bf16_pairs(buf[kk])
    o_ref[...] = acc


def _combine(x2, tw, ys, pos, tm):
    T, D = x2.shape
    W = ys.shape[1]
    return pl.pallas_call(
        functools.partial(_combine_kernel, tm=tm),
        out_shape=jax.ShapeDtypeStruct((T, D), F32),
        grid=(T // tm,),
        in_specs=[pl.BlockSpec((1, 1, tm * TOP_K), lambda i: (i, 0, 0), memory_space=pltpu.SMEM),
                  pl.BlockSpec((tm, D), lambda i: (i, 0)),
                  pl.BlockSpec((tm, LANES), lambda i: (i, 0)),
                  pl.BlockSpec(memory_space=pl.ANY)],
        out_specs=pl.BlockSpec((tm, D), lambda i: (i, 0)),
        scratch_shapes=[pltpu.VMEM((TOP_K, tm, W), ys.dtype), pltpu.SemaphoreType.DMA],
        compiler_params=_cparams(("arbitrary",)),
        name="combine",
    )(pos.reshape(T // tm, 1, tm * TOP_K), x2, tw, ys)


def _tile_gain(g, reps, scale=1.0):
    return (jnp.tile(g.astype(F32), reps) * scale)[None, :]


def _prep_layer(l, D, w_in, b_forget, g_mix, g_qa, g_ka, g_cq, g_ckv, w_uq, w_ukv, g_qb, g_kb, g_qc, g_kc,
                g_group, w_out, g_xmem, g_mem, w_mq, w_mkv, g_qm, g_km, w_mo, g_ffn, w_router, b_router,
                w_e1, b_e1, w_e2, b_e2):
    wi = w_in[l]
    sizes = (FOX_WIDTH, FOX_WIDTH, FOX_WIDTH, FOX_HEADS, MLA_Q_RANK, MLA_KV_RANK, MLA_ROPE_DIM,
             DIL_WIDTH, DIL_WIDTH, DIL_WIDTH)
    pts = np.cumsum((0,) + sizes)
    qa, ka, va, fa, cq, ckv, kr, qc, kc, vc = [wi[:, pts[j]:pts[j + 1]] for j in range(len(sizes))]
    misc = jnp.zeros((D, LANES), F32).at[:, :FOX_HEADS].set(fa).at[:, 32:32 + MLA_ROPE_DIM].set(kr)
    w_main = jnp.concatenate([qa, ka, va, cq, ckv, qc, kc, vc, misc], axis=1).astype(BF16)
    b_misc = jnp.zeros((1, LANES), F32).at[0, :FOX_HEADS].set(b_forget[l])

    uq = w_uq[l].reshape(MLA_Q_RANK, MLA_HEADS, MLA_QK_DIM)
    uq = jnp.pad(uq, ((0, 0), (0, 0), (0, MLA_PAD - MLA_QK_DIM))).reshape(MLA_Q_RANK, MLA_HEADS * MLA_PAD)
    ukv = w_ukv[l].reshape(MLA_KV_RANK, MLA_HEADS, MLA_NOPE_DIM + MLA_V_DIM)
    uk = jnp.pad(ukv[:, :, :MLA_NOPE_DIM], ((0, 0), (0, 0), (0, MLA_PAD - MLA_NOPE_DIM)))
    uk = uk.reshape(MLA_KV_RANK, MLA_HEADS * MLA_PAD)
    uv = ukv[:, :, MLA_NOPE_DIM:].reshape(MLA_KV_RANK, MLA_WIDTH)
    pad_gain = lambda g: jnp.pad(g.astype(F32), (0, MLA_PAD - MLA_QK_DIM))

    E = w_e1.shape[1]
    wr = jnp.zeros((D, LANES), F32).at[:, :E].set(w_router[l])
    wr_hi = wr.astype(BF16)
    wr_lo = (wr - wr_hi.astype(F32)).astype(BF16)
    return dict(
        g_mix=g_mix[l][None], w_main=w_main, b_misc=b_misc,
        g_qa=_tile_gain(g_qa[l], FOX_HEADS, HEAD_DIM ** -0.5), g_ka=_tile_gain(g_ka[l], FOX_HEADS),
        g_cq=g_cq[l][None], g_ckv=g_ckv[l][None],
        w_uq=uq.astype(BF16), w_uk=uk.astype(BF16), w_uv=uv.astype(BF16),
        g_qb=_tile_gain(pad_gain(g_qb[l]), MLA_HEADS, MLA_QK_DIM ** -0.5),
        g_kb=_tile_gain(pad_gain(g_kb[l]), MLA_HEADS),
        g_qc=_tile_gain(g_qc[l], DIL_HEADS, HEAD_DIM ** -0.5), g_kc=_tile_gain(g_kc[l], DIL_HEADS),
        g_group=g_group[l][None], w_out=w_out[l].astype(BF16),
        g_xmem=g_xmem[l][None], g_mem=g_mem[l][None],
        w_mq=w_mq[l].astype(BF16), w_mkv=w_mkv[l].astype(BF16),
        g_qm=_tile_gain(g_qm[l], MEM_HEADS, HEAD_DIM ** -0.5), g_km=_tile_gain(g_km[l], MEM_HEADS),
        w_mo=w_mo[l].astype(BF16), g_ffn=g_ffn[l][None],
        w_r_hi=wr_hi, w_r_lo=wr_lo,
        b_router=jnp.zeros((1, LANES), F32).at[0, :E].set(b_router[l]),
        b1g=b_e1[l][:, None, 0::2], b1l=b_e1[l][:, None, 1::2], b2=b_e2[l][:, None, :],
    )


def _rope_lane_tables(positions, rot_dim, period, offset):
    half = rot_dim // 2
    inv = ROPE_THETA ** (-jnp.arange(0, rot_dim, 2, dtype=F32) / rot_dim)
    ang = positions.astype(F32).reshape(-1)[:, None] * inv
    cos, sin = jnp.cos(ang), jnp.sin(ang)
    lane = np.arange(LANES) % period - offset
    idx = np.clip(lane % half if half else lane, 0, half - 1)
    first = (lane >= 0) & (lane < half)
    second = (lane >= half) & (lane < rot_dim)
    c = jnp.where(first | second, cos[:, idx], 1.0)
    s_dn = jnp.where(second, sin[:, idx], 0.0)
    s_up = jnp.where(first, -sin[:, idx], 0.0)
    return c, s_dn, s_up


def _tables(positions, tm):
    r = np.arange(tm)
    bd = (np.arange(LANES)[:, None] // HEAD_DIM) == (np.arange(LANES)[None, :] // HEAD_DIM)
    rep = np.zeros((LANES, MLA_HEADS * MLA_PAD), np.float32)
    for h in range(MLA_HEADS):
        for j in range(MLA_ROPE_DIM):
            rep[32 + j, h * MLA_PAD + MLA_NOPE_DIM + j] = 1.0
    half = DEINT_BLOCK // 2
    deint = np.zeros((DEINT_BLOCK, DEINT_BLOCK), np.float32)
    deint[2 * np.arange(half), np.arange(half)] = 1.0
    deint[2 * np.arange(half) + 1, half + np.arange(half)] = 1.0
    cm, sdm, sum_ = _rope_lane_tables(positions, MLA_ROPE_DIM, LANES, MLA_NOPE_DIM)
    cd, sdd, sud = _rope_lane_tables(positions, ROT_DIM, HEAD_DIM, 0)
    return dict(
        tri_incl=jnp.asarray(r[:, None] >= r[None, :], BF16),
        tri_strict=jnp.asarray(r[:, None] > r[None, :], BF16),
        bd64=jnp.asarray(bd, BF16), ones=jnp.ones((LANES, LANES), BF16), rep=jnp.asarray(rep, BF16),
        deint=jnp.asarray(deint, BF16),
        cm=cm, sdm=sdm, sum=sum_, cd=cd, sdd=sdd, sud=sud,
    )


def _routing_tables(ti, rk, cnt, n_experts, tr, n_tiles):
    counts = cnt[0, :n_experts].astype(jnp.int32)
    padded = ((counts + tr - 1) // tr) * tr
    ends = jnp.cumsum(padded)
    starts = ends - padded
    idx = ti[:, :TOP_K]
    pos = starts[idx] + rk[:, :TOP_K]
    tile_start = jnp.arange(n_tiles, dtype=jnp.int32) * tr
    tile_expert = jnp.sum((ends[None, :] <= tile_start[:, None]).astype(jnp.int32), axis=1)
    tile_expert = jnp.minimum(tile_expert, n_experts - 1)
    return pos.astype(jnp.int32), tile_expert.astype(jnp.int32)


def _pick_tile(seq, want):
    t = min(want, seq)
    while seq % t:
        t //= 2
    return t


def kernel(x, mem, positions, g_mix, w_in, b_forget, g_qa, g_ka, g_cq, g_ckv, w_uq, w_ukv, g_qb, g_kb, g_qc, g_kc, g_group, w_out, g_xmem, g_mem, w_mq, w_mkv, g_qm, g_km, w_mo, g_ffn, w_router, b_router, w_e1, b_e1, w_e2, b_e2):
    B, S, D = x.shape
    T = B * S
    depth = w_in.shape[0]
    E = w_e1.shape[1]
    tm = _pick_tile(S, PROJ_TILE)
    tq = _pick_tile(S, ATTN_TILE)
    tr = EXPERT_ROW_TILE
    tmoe = _pick_tile(S, MOE_TOKEN_TILE)
    n_rows = T * TOP_K + E * tr
    n_tiles = n_rows // tr

    tabs = _tables(positions, tm)
    x2d = x.reshape(T, D)
    for l in range(depth):
        prm = _prep_layer(l, D, w_in, b_forget, g_mix, g_qa, g_ka, g_cq, g_ckv, w_uq, w_ukv, g_qb, g_kb,
                          g_qc, g_kc, g_group, w_out, g_xmem, g_mem, w_mq, w_mkv, g_qm, g_km, w_mo, g_ffn,
                          w_router, b_router, w_e1, b_e1, w_e2, b_e2)
        qa, ka, va, csum, qb, kb, vb, qc, kc, vc = _inproj(x2d, prm, tabs, S, tm)
        kbias = -csum[:, :FOX_HEADS].reshape(B, S, FOX_HEADS // 2, 2).transpose(0, 2, 3, 1)
        r3 = lambda a: a.reshape(B, S, a.shape[-1])
        oa = _causal_attention(r3(qa), r3(ka), r3(va), kbias, tq=tq, split_heads=False)
        ob = _causal_attention(r3(qb), r3(kb), r3(vb), None, tq=tq, split_heads=True)
        oc = _dilated_attention(r3(qc), r3(kc), r3(vc), S)
        kmem, vmem = _memkv(mem, prm, tabs)
        x2, h3, ti, tw, rk, cnt = _postattn(x2d, oa.reshape(T, -1), ob.reshape(T, -1), oc.reshape(T, -1),
                                            kmem, vmem, prm, tabs, S, tm, E)
        pos, tile_expert = _routing_tables(ti, rk, cnt, E, tr, n_tiles)
        xs = _dispatch(h3, pos, n_rows, tmoe)
        ys = _experts(tile_expert, xs, w_e1, w_e2, l, prm, tabs, tr)
        x2d = _combine(x2, tw, ys, pos, tmoe)
    return x2d.reshape(B, S, D)
```

```python
import functools
import math

import jax
import jax.numpy as jnp
import numpy as np
from jax import lax
from jax.experimental import pallas as pl
from jax.experimental.pallas import tpu as pltpu

HEAD_DIM = 64
FOX_HEADS = 4
MLA_HEADS = 6
DIL_HEADS = 6
MLA_Q_RANK = 384
MLA_KV_RANK = 256
MLA_NOPE_DIM = 64
MLA_ROPE_DIM = 32
MLA_QK_DIM = MLA_NOPE_DIM + MLA_ROPE_DIM
MLA_V_DIM = 64
ROT_DIM = HEAD_DIM // 4
ROPE_THETA = 500000.0
DILATED_PATTERNS = ((128, 1), (512, 4), (2048, 16))
FOX_WIDTH = FOX_HEADS * HEAD_DIM
MLA_WIDTH = MLA_HEADS * MLA_V_DIM
DIL_WIDTH = DIL_HEADS * HEAD_DIM
MEM_HEADS = 4
MEM_WIDTH = MEM_HEADS * HEAD_DIM
TOP_K = 4
SWIGLU_ALPHA = 1.702
SWIGLU_LIMIT = 7.0
RMS_EPS = 1e-6

LANES = 128
VMEM_LIMIT_BYTES = 56 * 1024 * 1024

PROJ_TILE = 512
ATTN_TILE = 256
EXPERT_ROW_TILE = 512
MOE_TOKEN_TILE = 256
ROW_DMA_UNROLL = 8

NEG_BIG = -1e30
MLA_PAD = 128
DEINT_BLOCK = 256

F32 = jnp.float32
BF16 = jnp.bfloat16


def _cparams(sem):
    return pltpu.CompilerParams(dimension_semantics=sem, vmem_limit_bytes=VMEM_LIMIT_BYTES)


def _full(shape):
    nd = len(shape)
    return pl.BlockSpec(shape, lambda *_: (0,) * nd)


def _dot(a, b):
    return jnp.dot(a, b, preferred_element_type=F32)


def _dot_nt(a, b):
    return lax.dot_general(a, b, (((1,), (1,)), ((), ())), preferred_element_type=F32)


def _rms_rows(x, g):
    return x * lax.rsqrt(jnp.mean(x * x, axis=-1, keepdims=True) + RMS_EPS) * g


def _group_norm_tiles(y, ones_mat, inv_n):
    outs = []
    for j in range(y.shape[1] // LANES):
        yj = y[:, j * LANES:(j + 1) * LANES]
        ss = _dot((yj * yj).astype(BF16), ones_mat)
        outs.append(yj * lax.rsqrt(ss * inv_n + RMS_EPS))
    return outs


def _rope_tile(y, half, c, s_dn, s_up):
    return y * c + pltpu.roll(y, half, 1) * s_dn + pltpu.roll(y, LANES - half, 1) * s_up


def _inproj_kernel(x_ref, gmix_ref, w_ref, bmisc_ref, tri_ref, bd64_ref, ones_ref,
                   gqa_ref, gka_ref, gcq_ref, gckv_ref, wuq_ref, wuk_ref, wuv_ref, rep_ref,
                   gqb_ref, gkb_ref, gqc_ref, gkc_ref,
                   cm_ref, sdm_ref, sum_ref, cd_ref, sdd_ref, sud_ref,
                   qa_ref, ka_ref, va_ref, c_ref, qb_ref, kb_ref, vb_ref, qc_ref, kc_ref, vc_ref,
                   carry_ref, *, tiles_per_batch):
    i = pl.program_id(0)
    x = x_ref[...]
    h = _rms_rows(x, gmix_ref[...]).astype(BF16)
    bd64 = bd64_ref[...]
    ones = ones_ref[...]

    def seg(a, b):
        return _dot(h, w_ref[:, a:b])

    o = 0
    qa = seg(o, o + FOX_WIDTH); o += FOX_WIDTH
    ka = seg(o, o + FOX_WIDTH); o += FOX_WIDTH
    va = seg(o, o + FOX_WIDTH); o += FOX_WIDTH
    cq = seg(o, o + MLA_Q_RANK); o += MLA_Q_RANK
    ckv = seg(o, o + MLA_KV_RANK); o += MLA_KV_RANK
    qc = seg(o, o + DIL_WIDTH); o += DIL_WIDTH
    kc = seg(o, o + DIL_WIDTH); o += DIL_WIDTH
    vc = seg(o, o + DIL_WIDTH); o += DIL_WIDTH
    misc = seg(o, o + LANES)

    qa_n = jnp.concatenate(_group_norm_tiles(qa, bd64, 1.0 / HEAD_DIM), axis=1) * gqa_ref[...]
    ka_n = jnp.concatenate(_group_norm_tiles(ka, bd64, 1.0 / HEAD_DIM), axis=1) * gka_ref[...]
    qa_ref[...] = qa_n.astype(BF16)
    ka_ref[...] = ka_n.astype(BF16)
    va_ref[...] = va.astype(BF16)

    z = misc + bmisc_ref[...]
    logf = jnp.minimum(z, 0.0) - jnp.log(1.0 + jnp.exp(-jnp.abs(z)))
    lane = lax.broadcasted_iota(jnp.int32, logf.shape, 1)
    logf = jnp.where(lane < FOX_HEADS, logf, 0.0)
    p0 = logf.astype(BF16)
    r0 = logf - p0.astype(F32)
    p1 = r0.astype(BF16)
    p2 = (r0 - p1.astype(F32)).astype(BF16)
    tri = tri_ref[...]

    @pl.when(i % tiles_per_batch == 0)
    def _():
        carry_ref[...] = jnp.zeros_like(carry_ref)

    csum = _dot(tri, p0) + _dot(tri, p1) + _dot(tri, p2) + carry_ref[...]
    c_ref[...] = csum
    carry_ref[...] = csum[csum.shape[0] - 1:, :]

    cqn = _rms_rows(cq, gcq_ref[...]).astype(BF16)
    ckvn = _rms_rows(ckv, gckv_ref[...]).astype(BF16)
    qb_pre = _dot(cqn, wuq_ref[...])
    kb_pre = _dot(ckvn, wuk_ref[...]) + _dot(misc.astype(BF16), rep_ref[...])
    vb_ref[...] = _dot(ckvn, wuv_ref[...]).astype(BF16)
    cm, sdm, sum_ = cm_ref[...], sdm_ref[...], sum_ref[...]
    half_b = MLA_ROPE_DIM // 2
    qb_t = _group_norm_tiles(qb_pre, ones, 1.0 / MLA_QK_DIM)
    kb_t = _group_norm_tiles(kb_pre, ones, 1.0 / MLA_QK_DIM)
    for j in range(MLA_HEADS):
        sl = slice(j * LANES, (j + 1) * LANES)
        qj = qb_t[j] * gqb_ref[:, sl]
        kj = kb_t[j] * gkb_ref[:, sl]
        qb_ref[:, sl] = _rope_tile(qj, half_b, cm, sdm, sum_).astype(BF16)
        kb_ref[:, sl] = _rope_tile(kj, half_b, cm, sdm, sum_).astype(BF16)

    cd, sdd, sud = cd_ref[...], sdd_ref[...], sud_ref[...]
    half_c = ROT_DIM // 2
    qc_t = _group_norm_tiles(qc, bd64, 1.0 / HEAD_DIM)
    kc_t = _group_norm_tiles(kc, bd64, 1.0 / HEAD_DIM)
    for j in range(DIL_WIDTH // LANES):
        sl = slice(j * LANES, (j + 1) * LANES)
        qc_ref[:, sl] = _rope_tile(qc_t[j] * gqc_ref[:, sl], half_c, cd, sdd, sud)
        kc_ref[:, sl] = _rope_tile(kc_t[j] * gkc_ref[:, sl], half_c, cd, sdd, sud)
    vc_ref[...] = vc


def _inproj(x2d, prm, tabs, seq, tm):
    T, D = x2d.shape
    nm = prm["w_main"].shape[1]
    row = lambda w: pl.BlockSpec((tm, w), lambda i: (i, 0))
    in_specs = [row(D), _full((1, D)), _full((D, nm)), _full((1, LANES)), _full((tm, tm)),
                _full((LANES, LANES)), _full((LANES, LANES)),
                _full((1, FOX_WIDTH)), _full((1, FOX_WIDTH)), _full((1, MLA_Q_RANK)), _full((1, MLA_KV_RANK)),
                _full((MLA_Q_RANK, MLA_HEADS * MLA_PAD)), _full((MLA_KV_RANK, MLA_HEADS * MLA_PAD)),
                _full((MLA_KV_RANK, MLA_WIDTH)), _full((LANES, MLA_HEADS * MLA_PAD)),
                _full((1, MLA_HEADS * MLA_PAD)), _full((1, MLA_HEADS * MLA_PAD)),
                _full((1, DIL_WIDTH)), _full((1, DIL_WIDTH))] + [row(LANES)] * 6
    outs = [(FOX_WIDTH, BF16), (FOX_WIDTH, BF16), (FOX_WIDTH, BF16), (LANES, F32),
            (MLA_HEADS * MLA_PAD, BF16), (MLA_HEADS * MLA_PAD, BF16), (MLA_WIDTH, BF16),
            (DIL_WIDTH, F32), (DIL_WIDTH, F32), (DIL_WIDTH, F32)]
    return pl.pallas_call(
        functools.partial(_inproj_kernel, tiles_per_batch=seq // tm),
        out_shape=[jax.ShapeDtypeStruct((T, w), dt) for w, dt in outs],
        grid=(T // tm,),
        in_specs=in_specs,
        out_specs=[row(w) for w, _ in outs],
        scratch_shapes=[pltpu.VMEM((1, LANES), F32)],
        compiler_params=_cparams(("arbitrary",)),
        name="inproj",
    )(x2d, prm["g_mix"], prm["w_main"], prm["b_misc"], tabs["tri_incl"], tabs["bd64"], tabs["ones"],
      prm["g_qa"], prm["g_ka"], prm["g_cq"], prm["g_ckv"], prm["w_uq"], prm["w_uk"], prm["w_uv"], tabs["rep"],
      prm["g_qb"], prm["g_kb"], prm["g_qc"], prm["g_kc"],
      tabs["cm"], tabs["sdm"], tabs["sum"], tabs["cd"], tabs["sdd"], tabs["sud"])


def _causal_kernel(*refs, seq, tq, split_heads, with_bias):
    if with_bias:
        q_ref, k_ref, v_ref, b_ref, o_ref = refs
    else:
        q_ref, k_ref, v_ref, o_ref = refs
        b_ref = None
    lane = lax.broadcasted_iota(jnp.int32, (tq, LANES), 1)
    lo_half = lane < HEAD_DIM
    r = lax.broadcasted_iota(jnp.int32, (tq, tq), 0)
    c = lax.broadcasted_iota(jnp.int32, (tq, tq), 1)
    keep = c <= r
    for qi in range(seq // tq):
        q = q_ref[0, qi * tq:(qi + 1) * tq, :]
        n_keys = (qi + 1) * tq
        v = v_ref[0, :n_keys, :]
        res = []
        for hh in range(2):
            if split_heads:
                qh = q[:, hh * LANES:(hh + 1) * LANES]
                k = k_ref[0, :n_keys, hh * LANES:(hh + 1) * LANES]
            else:
                zero = jnp.zeros_like(q)
                qh = jnp.where(lo_half, q, zero) if hh == 0 else jnp.where(lo_half, zero, q)
                k = k_ref[0, :n_keys, :]
            s = _dot_nt(qh, k)
            if with_bias:
                s = s + b_ref[0, 0, hh:hh + 1, :n_keys]
            s_diag = jnp.where(keep, s[:, n_keys - tq:], NEG_BIG)
            s = jnp.concatenate([s[:, :n_keys - tq], s_diag], axis=1) if qi else s_diag
            p = jnp.exp(s - jnp.max(s, axis=-1, keepdims=True))
            l = jnp.sum(p, axis=-1, keepdims=True)
            res.append(_dot(p.astype(BF16), v) / l)
        o_ref[0, qi * tq:(qi + 1) * tq, :] = jnp.where(lo_half, res[0], res[1]).astype(o_ref.dtype)


def _causal_attention(q, k, v, kbias, *, tq, split_heads):
    B, S, _ = q.shape
    P = v.shape[2] // LANES
    wq = 2 * LANES if split_heads else LANES
    in_specs = [pl.BlockSpec((1, S, wq), lambda b, p: (b, 0, p)),
                pl.BlockSpec((1, S, wq), lambda b, p: (b, 0, p)),
                pl.BlockSpec((1, S, LANES), lambda b, p: (b, 0, p))]
    args = [q, k, v]
    if kbias is not None:
        in_specs.append(pl.BlockSpec((1, 1, 2, S), lambda b, p: (b, p, 0, 0)))
        args.append(kbias)
    return pl.pallas_call(
        functools.partial(_causal_kernel, seq=S, tq=tq, split_heads=split_heads, with_bias=kbias is not None),
        out_shape=jax.ShapeDtypeStruct((B, S, P * LANES), BF16),
        grid=(B, P),
        in_specs=in_specs,
        out_specs=pl.BlockSpec((1, S, LANES), lambda b, p: (b, 0, p)),
        compiler_params=_cparams(("parallel", "parallel")),
        name="causal_split" if split_heads else "causal_shared",
    )(*args)


def _dilated_kernel(q_ref, k_ref, v_ref, o_ref, acc_ref, m_ref, l_ref, *, seq, n, dilations):
    lane = lax.broadcasted_iota(jnp.int32, (1, 1, LANES), 2)
    lo_half = lane < HEAD_DIM
    row = lax.broadcasted_iota(jnp.int32, (n, n), 0)
    col = lax.broadcasted_iota(jnp.int32, (n, n), 1)
    own_keep = (col <= row)[None]
    prev_keep = (col >= row)[None]

    nblk = seq // n
    for pi, d in enumerate(dilations):
        L = seq // d
        nb = L // n

        def load(ref, d=d, L=L, nb=nb):
            if d == 1:
                return ref[0].reshape(nblk, n, LANES).astype(BF16)
            parts = [ref[0, pl.ds(r, L, stride=d), :].reshape(nb, n, LANES) for r in range(d)]
            return jnp.concatenate(parts, axis=0).astype(BF16)

        qb, kb, vb = load(q_ref), load(k_ref), load(v_ref)
        if nb > 1:
            zero = jnp.zeros((1, n, LANES), BF16)
            kp = jnp.concatenate([zero, kb[:nblk - 1]], axis=0)
            vp = jnp.concatenate([zero, vb[:nblk - 1]], axis=0)
            blk = lax.broadcasted_iota(jnp.int32, (nblk, n, n), 0)
            prev_ok = jnp.logical_and(prev_keep, lax.rem(blk, nb) != 0)
        zq = jnp.zeros_like(qb)
        heads = []
        for hh in range(2):
            qh = jnp.where(lo_half, qb, zq) if hh == 0 else jnp.where(lo_half, zq, qb)
            s_own = jnp.einsum("bqd,bkd->bqk", qh, kb, preferred_element_type=F32)
            s_own = jnp.where(own_keep, s_own, NEG_BIG)
            m = jnp.max(s_own, axis=-1, keepdims=True)
            if nb > 1:
                s_prev = jnp.einsum("bqd,bkd->bqk", qh, kp, preferred_element_type=F32)
                s_prev = jnp.where(prev_ok, s_prev, NEG_BIG)
                m = jnp.maximum(m, jnp.max(s_prev, axis=-1, keepdims=True))
            p_own = jnp.exp(s_own - m)
            l = jnp.sum(p_own, axis=-1, keepdims=True)
            pv = jnp.einsum("bqk,bkd->bqd", p_own.astype(BF16), vb, preferred_element_type=F32)
            if nb > 1:
                p_prev = jnp.exp(s_prev - m)
                l = l + jnp.sum(p_prev, axis=-1, keepdims=True)
                pv = pv + jnp.einsum("bqk,bkd->bqd", p_prev.astype(BF16), vp, preferred_element_type=F32)
            heads.append((m, l, pv))
        m_f = jnp.where(lo_half, heads[0][0], heads[1][0])
        l_f = jnp.where(lo_half, heads[0][1], heads[1][1])
        a_f = jnp.where(lo_half, heads[0][2], heads[1][2])
        for dst, val in ((acc_ref, a_f), (m_ref, m_f), (l_ref, l_f)):
            if d == 1:
                dst[pi] = val.reshape(seq, LANES)
            else:
                for r in range(d):
                    dst[pi, pl.ds(r, L, stride=d), :] = val[r * nb:(r + 1) * nb].reshape(L, LANES)


    np_ = len(dilations)
    m_all = m_ref[0]
    for pi in range(1, np_):
        m_all = jnp.maximum(m_all, m_ref[pi])
    num = jnp.zeros((seq, LANES), F32)
    den = jnp.zeros((seq, LANES), F32)
    for pi in range(np_):
        w = jnp.exp(m_ref[pi] - m_all)
        num = num + w * acc_ref[pi]
        den = den + w * l_ref[pi]
    o_ref[0] = num / den


def _dilated_attention(qc, kc, vc, seq):
    B = qc.shape[0]
    n = DILATED_PATTERNS[0][0] // DILATED_PATTERNS[0][1]
    dil = tuple(d for _, d in DILATED_PATTERNS)
    for w, d in DILATED_PATTERNS:
        assert w // d == n and seq % (d * n) == 0
    spec = pl.BlockSpec((1, seq, LANES), lambda b, p: (b, 0, p))
    np_ = len(dil)
    return pl.pallas_call(
        functools.partial(_dilated_kernel, seq=seq, n=n, dilations=dil),
        out_shape=jax.ShapeDtypeStruct((B, seq, DIL_WIDTH), F32),
        grid=(B, DIL_WIDTH // LANES),
        in_specs=[spec, spec, spec],
        out_specs=spec,
        scratch_shapes=[pltpu.VMEM((np_, seq, LANES), F32)] * 3,
        compiler_params=_cparams(("parallel", "parallel")),
        name="dilated",
    )(qc, kc, vc)


def _memkv_kernel(mem_ref, g_ref, w_ref, gk_ref, bd64_ref, k_ref, v_ref):
    mn = _rms_rows(mem_ref[0], g_ref[...]).astype(BF16)
    kv = _dot(mn, w_ref[...])
    k = jnp.concatenate(_group_norm_tiles(kv[:, :MEM_WIDTH], bd64_ref[...], 1.0 / HEAD_DIM), axis=1)
    k_ref[0] = (k * gk_ref[...]).astype(BF16)
    v_ref[0] = kv[:, MEM_WIDTH:].astype(BF16)


def _memkv(mem, prm, tabs):
    B, M, D = mem.shape
    return pl.pallas_call(
        _memkv_kernel,
        out_shape=[jax.ShapeDtypeStruct((B, M, MEM_WIDTH), BF16)] * 2,
        grid=(B,),
        in_specs=[pl.BlockSpec((1, M, D), lambda b: (b, 0, 0)), _full((1, D)), _full((D, 2 * MEM_WIDTH)),
                  _full((1, MEM_WIDTH)), _full((LANES, LANES))],
        out_specs=[pl.BlockSpec((1, M, MEM_WIDTH), lambda b: (b, 0, 0))] * 2,
        compiler_params=_cparams(("parallel",)),
        name="memkv",
    )(mem, prm["g_mem"], prm["w_mkv"], prm["g_km"], tabs["bd64"])


def _postattn_kernel(x_ref, oa_ref, ob_ref, oc_ref, gg_ref, wout_ref, gx_ref, wmq_ref, gqm_ref, bd64_ref,
                     km_ref, vm_ref, wmo_ref, gf_ref, wrh_ref, wrl_ref, br_ref, tri_ref,
                     x2_ref, ti_ref, tw_ref, rk_ref, cnt_ref, carry_ref, *, n_experts):
    i = pl.program_id(0)
    tm = x_ref.shape[0]
    x = x_ref[...]
    a0, a1 = FOX_WIDTH, FOX_WIDTH + MLA_WIDTH
    ma = _rms_rows(oa_ref[...].astype(F32), gg_ref[:, :a0]).astype(BF16)
    mb = _rms_rows(ob_ref[...].astype(F32), gg_ref[:, a0:a1]).astype(BF16)
    mc = _rms_rows(oc_ref[...], gg_ref[:, a1:]).astype(BF16)
    x1 = x + _dot(ma, wout_ref[:a0, :]) + _dot(mb, wout_ref[a0:a1, :]) + _dot(mc, wout_ref[a1:, :])

    h2 = _rms_rows(x1, gx_ref[...]).astype(BF16)
    qm = _dot(h2, wmq_ref[...])
    qm_t = _group_norm_tiles(qm, bd64_ref[...], 1.0 / HEAD_DIM)
    lane = lax.broadcasted_iota(jnp.int32, (tm, LANES), 1)
    lo_half = lane < HEAD_DIM
    outs = []
    for p in range(MEM_WIDTH // LANES):
        sl = slice(p * LANES, (p + 1) * LANES)
        q2 = (qm_t[p] * gqm_ref[:, sl]).astype(BF16)
        k2 = km_ref[0, :, sl]
        v2 = vm_ref[0, :, sl]
        zero = jnp.zeros_like(q2)
        res = []
        for hh in range(2):
            qh = jnp.where(lo_half, q2, zero) if hh == 0 else jnp.where(lo_half, zero, q2)
            s = _dot_nt(qh, k2)
            pr = jnp.exp(s - jnp.max(s, axis=-1, keepdims=True))
            l = jnp.sum(pr, axis=-1, keepdims=True)
            res.append(_dot(pr.astype(BF16), v2) / l)
        outs.append(jnp.where(lo_half, res[0], res[1]))
    om = jnp.concatenate(outs, axis=1).astype(BF16)
    x2 = x1 + _dot(om, wmo_ref[...])
    x2_ref[...] = x2

    h3 = _rms_rows(x2, gf_ref[...])
    h3_hi = h3.astype(BF16)
    h3_lo = (h3 - h3_hi.astype(F32)).astype(BF16)
    logits = _dot(h3_hi, wrh_ref[...]) + _dot(h3_hi, wrl_ref[...]) + _dot(h3_lo, wrh_ref[...]) + br_ref[...]
    logits = jnp.where(lane < n_experts, logits, NEG_BIG)
    work = logits
    lane_f = lane.astype(F32)
    sel = jnp.zeros((tm, LANES), F32)
    vals, idxs = [], []
    for _ in range(TOP_K):
        mv = jnp.max(work, axis=-1, keepdims=True)
        ix = jnp.min(jnp.where(work == mv, lane_f, float(LANES)), axis=-1, keepdims=True)
        hit = lane_f == ix
        work = jnp.where(hit, NEG_BIG, work)
        sel = jnp.where(hit, 1.0, sel)
        vals.append(mv)
        idxs.append(ix)
    es = [jnp.exp(v - vals[0]) for v in vals]
    den = es[0] + es[1] + es[2] + es[3]

    @pl.when(i == 0)
    def _():
        carry_ref[...] = jnp.zeros_like(carry_ref)

    before = _dot(tri_ref[...], sel.astype(BF16)) + carry_ref[...]
    total = before[tm - 1:, :] + sel[tm - 1:, :]
    carry_ref[...] = total
    cnt_ref[...] = total

    ti = jnp.zeros((tm, LANES), F32)
    tw = jnp.zeros((tm, LANES), F32)
    rk = jnp.zeros((tm, LANES), F32)
    for kk in range(TOP_K):
        at = lane == kk
        ti = jnp.where(at, idxs[kk], ti)
        tw = jnp.where(at, es[kk] / den, tw)
        r_k = jnp.sum(jnp.where(lane_f == idxs[kk], before, 0.0), axis=-1, keepdims=True)
        rk = jnp.where(at, r_k, rk)
    ti_ref[...] = ti.astype(jnp.int32)
    tw_ref[...] = tw
    rk_ref[...] = rk.astype(jnp.int32)


def _postattn(x2d, oa, ob, oc, kmem, vmem, prm, tabs, seq, tm, n_experts):
    T, D = x2d.shape
    M = kmem.shape[1]
    tpb = seq // tm
    row = lambda w: pl.BlockSpec((tm, w), lambda i: (i, 0))
    memspec = pl.BlockSpec((1, M, MEM_WIDTH), lambda i: (i // tpb, 0, 0))
    mix = FOX_WIDTH + MLA_WIDTH + DIL_WIDTH
    return pl.pallas_call(
        functools.partial(_postattn_kernel, n_experts=n_experts),
        out_shape=[jax.ShapeDtypeStruct((T, D), F32),
                   jax.ShapeDtypeStruct((T, LANES), jnp.int32), jax.ShapeDtypeStruct((T, LANES), F32),
                   jax.ShapeDtypeStruct((T, LANES), jnp.int32), jax.ShapeDtypeStruct((1, LANES), F32)],
        grid=(T // tm,),
        in_specs=[row(D), row(FOX_WIDTH), row(MLA_WIDTH), row(DIL_WIDTH), _full((1, mix)), _full((mix, D)),
                  _full((1, D)), _full((D, MEM_WIDTH)), _full((1, MEM_WIDTH)), _full((LANES, LANES)),
                  memspec, memspec, _full((MEM_WIDTH, D)), _full((1, D)),
                  _full((D, LANES)), _full((D, LANES)), _full((1, LANES)), _full((tm, tm))],
        out_specs=[row(D), row(LANES), row(LANES), row(LANES), _full((1, LANES))],
        scratch_shapes=[pltpu.VMEM((1, LANES), F32)],
        compiler_params=_cparams(("arbitrary",)),
        name="postattn",
    )(x2d, oa, ob, oc, prm["g_group"], prm["w_out"], prm["g_xmem"], prm["w_mq"], prm["g_qm"], tabs["bd64"],
      kmem, vmem, prm["w_mo"], prm["g_ffn"], prm["w_r_hi"], prm["w_r_lo"], prm["b_router"], tabs["tri_strict"])


def _row_copy(src_ref, src_row, dst_ref, dst_row, sem):
    return pltpu.make_async_copy(src_ref.at[pl.ds(src_row, 1), :], dst_ref.at[pl.ds(dst_row, 1), :], sem)


def _dispatch_kernel(fill_ref, pos_ref, x_ref, xs_ref, zero_ref, sem, zsem, *, tm, tr):
    @pl.when(pl.program_id(0) == 0)
    def _():
        n_experts = fill_ref.shape[0] - 1
        n_rows = xs_ref.shape[0]
        zero_ref[...] = jnp.zeros_like(zero_ref)
        starts = [(fill_ref[e], fill_ref[e] >= 0) for e in range(n_experts)]
        for j in range(n_experts + 1):
            start = fill_ref[n_experts] + j * tr
            starts.append((start, start < n_rows))

        def fill(start):
            return pltpu.make_async_copy(zero_ref, xs_ref.at[pl.ds(pl.multiple_of(start, tr), tr), :], zsem)

        for start, live in starts:
            @pl.when(live)
            def _(start=start):
                fill(start).start()
        for start, live in starts:
            @pl.when(live)
            def _(start=start):
                fill(start).wait()

    def issue(r, _):
        for kk in range(TOP_K):
            _row_copy(x_ref, r, xs_ref, pos_ref[0, 0, r * TOP_K + kk], sem).start(priority=kk % 2)
        return 0
    lax.fori_loop(0, tm, issue, 0, unroll=ROW_DMA_UNROLL)
    n = tm * TOP_K
    pltpu.make_async_copy(xs_ref.at[pl.ds(0, n), :], xs_ref.at[pl.ds(0, n), :], sem).wait()


def _dispatch(x2, pos, pad_start, n_rows, tm, tr):
    T, D = x2.shape
    return pl.pallas_call(
        functools.partial(_dispatch_kernel, tm=tm, tr=tr),
        out_shape=jax.ShapeDtypeStruct((n_rows, D), x2.dtype),
        grid_spec=pltpu.PrefetchScalarGridSpec(
            num_scalar_prefetch=1,
            grid=(T // tm,),
            in_specs=[pl.BlockSpec((1, 1, tm * TOP_K), lambda i, pad: (i, 0, 0), memory_space=pltpu.SMEM),
                      pl.BlockSpec((tm, D), lambda i, pad: (i, 0))],
            out_specs=pl.BlockSpec(memory_space=pl.ANY),
            scratch_shapes=[pltpu.VMEM((tr, D), x2.dtype), pltpu.SemaphoreType.DMA, pltpu.SemaphoreType.DMA],
        ),
        compiler_params=_cparams(("arbitrary",)),
        name="dispatch",
    )(pad_start, pos.reshape(T // tm, 1, tm * TOP_K), x2)


def _experts_kernel(te_ref, nu_ref, xs_ref, gf_ref, w1_ref, b1g_ref, b1l_ref, w2_ref, b2_ref, perm_ref,
                    ys_ref, w1s_ref, w2s_ref):
    i = pl.program_id(0)
    two_f = w1_ref.shape[3]
    n_blk = two_f // DEINT_BLOCK
    half = DEINT_BLOCK // 2

    @pl.when(i < nu_ref[0])
    def _():
        @pl.when(jnp.logical_or(i == 0, te_ref[i] != te_ref[jnp.maximum(i - 1, 0)]))
        def _():
            perm = perm_ref[...]
            for b in range(n_blk):
                sl = slice(b * DEINT_BLOCK, (b + 1) * DEINT_BLOCK)
                w1s_ref[:, sl] = _dot(w1_ref[0, 0, :, sl].astype(BF16), perm).astype(BF16)
            w2s_ref[...] = w2_ref[0, 0].astype(BF16)

        x = _rms_rows(xs_ref[...], gf_ref[...]).astype(BF16)
        h = _dot(x, w1s_ref[...])
        acts = []
        for b in range(n_blk):
            glu = h[:, b * DEINT_BLOCK:b * DEINT_BLOCK + half] + b1g_ref[0, :, b * half:(b + 1) * half]
            lin = h[:, b * DEINT_BLOCK + half:(b + 1) * DEINT_BLOCK] + b1l_ref[0, :, b * half:(b + 1) * half]
            glu = jnp.minimum(glu, SWIGLU_LIMIT)
            lin = jnp.clip(lin, -SWIGLU_LIMIT, SWIGLU_LIMIT)
            acts.append((glu * (1.0 / (1.0 + jnp.exp(-SWIGLU_ALPHA * glu))) * (lin + 1.0)).astype(BF16))
        act = jnp.concatenate(acts, axis=1) if n_blk > 1 else acts[0]
        ys_ref[...] = _dot(act, w2s_ref[...]) + b2_ref[0]

    @pl.when(i >= nu_ref[0])
    def _():
        ys_ref[...] = jnp.zeros_like(ys_ref)


def _experts(tile_expert, n_used, xs, w_e1, w_e2, layer, prm, tabs, tr):
    n_rows, D = xs.shape
    two_f = w_e1.shape[3]
    F = two_f // 2
    assert two_f % DEINT_BLOCK == 0
    n_tiles = n_rows // tr
    rows = pl.BlockSpec((tr, D), lambda i, te, nu: (i, 0))
    wspec = lambda s: pl.BlockSpec((1,) + s, lambda i, te, nu: (te[i], 0, 0))
    lspec = lambda s: pl.BlockSpec((1, 1) + s, lambda i, te, nu: (layer, te[i], 0, 0))
    return pl.pallas_call(
        _experts_kernel,
        out_shape=jax.ShapeDtypeStruct((n_rows, D), F32),
        grid_spec=pltpu.PrefetchScalarGridSpec(
            num_scalar_prefetch=2,
            grid=(n_tiles,),
            in_specs=[rows, pl.BlockSpec((1, D), lambda i, te, nu: (0, 0)),
                      lspec((D, two_f)), wspec((1, F)), wspec((1, F)), lspec((F, D)), wspec((1, D)),
                      pl.BlockSpec((DEINT_BLOCK, DEINT_BLOCK), lambda i, te, nu: (0, 0))],
            out_specs=rows,
            scratch_shapes=[pltpu.VMEM((D, two_f), BF16), pltpu.VMEM((F, D), BF16)],
        ),
        compiler_params=_cparams(("arbitrary",)),
        name="experts",
    )(tile_expert, n_used, xs, prm["g_ffn"], w_e1, prm["b1g"], prm["b1l"], w_e2, prm["b2"], tabs["deint"])


def _combine_kernel(pos_ref, x_ref, tw_ref, ys_ref, o_ref, buf, sem, *, tm):
    def issue(r, _):
        for kk in range(TOP_K):
            _row_copy(ys_ref, pos_ref[0, 0, r * TOP_K + kk], buf.at[kk], r, sem).start(priority=kk % 2)
        return 0
    lax.fori_loop(0, tm, issue, 0, unroll=ROW_DMA_UNROLL)
    pltpu.make_async_copy(buf, buf, sem).wait()

    tw = tw_ref[...]
    acc = x_ref[...]
    for kk in range(TOP_K):
        acc = acc + tw[:, kk:kk + 1] * buf[kk]
    o_ref[...] = acc


def _combine(x2, tw, ys, pos, tm):
    T, D = x2.shape
    W = ys.shape[1]
    return pl.pallas_call(
        functools.partial(_combine_kernel, tm=tm),
        out_shape=jax.ShapeDtypeStruct((T, D), F32),
        grid=(T // tm,),
        in_specs=[pl.BlockSpec((1, 1, tm * TOP_K), lambda i: (i, 0, 0), memory_space=pltpu.SMEM),
                  pl.BlockSpec((tm, D), lambda i: (i, 0)),
                  pl.BlockSpec((tm, LANES), lambda i: (i, 0)),
                  pl.BlockSpec(memory_space=pl.ANY)],
        out_specs=pl.BlockSpec((tm, D), lambda i: (i, 0)),
        scratch_shapes=[pltpu.VMEM((TOP_K, tm, W), ys.dtype), pltpu.SemaphoreType.DMA],
        compiler_params=_cparams(("arbitrary",)),
        name="combine",
    )(pos.reshape(T // tm, 1, tm * TOP_K), x2, tw, ys)


def _tile_gain(g, reps, scale=1.0):
    return (jnp.tile(g.astype(F32), reps) * scale)[None, :]


def _prep_layer(l, D, w_in, b_forget, g_mix, g_qa, g_ka, g_cq, g_ckv, w_uq, w_ukv, g_qb, g_kb, g_qc, g_kc,
                g_group, w_out, g_xmem, g_mem, w_mq, w_mkv, g_qm, g_km, w_mo, g_ffn, w_router, b_router,
                w_e1, b_e1, w_e2, b_e2):
    wi = w_in[l]
    sizes = (FOX_WIDTH, FOX_WIDTH, FOX_WIDTH, FOX_HEADS, MLA_Q_RANK, MLA_KV_RANK, MLA_ROPE_DIM,
             DIL_WIDTH, DIL_WIDTH, DIL_WIDTH)
    pts = np.cumsum((0,) + sizes)
    qa, ka, va, fa, cq, ckv, kr, qc, kc, vc = [wi[:, pts[j]:pts[j + 1]] for j in range(len(sizes))]
    misc = jnp.zeros((D, LANES), F32).at[:, :FOX_HEADS].set(fa).at[:, 32:32 + MLA_ROPE_DIM].set(kr)
    w_main = jnp.concatenate([qa, ka, va, cq, ckv, qc, kc, vc, misc], axis=1).astype(BF16)
    b_misc = jnp.zeros((1, LANES), F32).at[0, :FOX_HEADS].set(b_forget[l])

    uq = w_uq[l].reshape(MLA_Q_RANK, MLA_HEADS, MLA_QK_DIM)
    uq = jnp.pad(uq, ((0, 0), (0, 0), (0, MLA_PAD - MLA_QK_DIM))).reshape(MLA_Q_RANK, MLA_HEADS * MLA_PAD)
    ukv = w_ukv[l].reshape(MLA_KV_RANK, MLA_HEADS, MLA_NOPE_DIM + MLA_V_DIM)
    uk = jnp.pad(ukv[:, :, :MLA_NOPE_DIM], ((0, 0), (0, 0), (0, MLA_PAD - MLA_NOPE_DIM)))
    uk = uk.reshape(MLA_KV_RANK, MLA_HEADS * MLA_PAD)
    uv = ukv[:, :, MLA_NOPE_DIM:].reshape(MLA_KV_RANK, MLA_WIDTH)
    pad_gain = lambda g: jnp.pad(g.astype(F32), (0, MLA_PAD - MLA_QK_DIM))

    E = w_e1.shape[1]
    wr = jnp.zeros((D, LANES), F32).at[:, :E].set(w_router[l])
    wr_hi = wr.astype(BF16)
    wr_lo = (wr - wr_hi.astype(F32)).astype(BF16)
    return dict(
        g_mix=g_mix[l][None], w_main=w_main, b_misc=b_misc,
        g_qa=_tile_gain(g_qa[l], FOX_HEADS, HEAD_DIM ** -0.5), g_ka=_tile_gain(g_ka[l], FOX_HEADS),
        g_cq=g_cq[l][None], g_ckv=g_ckv[l][None],
        w_uq=uq.astype(BF16), w_uk=uk.astype(BF16), w_uv=uv.astype(BF16),
        g_qb=_tile_gain(pad_gain(g_qb[l]), MLA_HEADS, MLA_QK_DIM ** -0.5),
        g_kb=_tile_gain(pad_gain(g_kb[l]), MLA_HEADS),
        g_qc=_tile_gain(g_qc[l], DIL_HEADS, HEAD_DIM ** -0.5), g_kc=_tile_gain(g_kc[l], DIL_HEADS),
        g_group=g_group[l][None], w_out=w_out[l].astype(BF16),
        g_xmem=g_xmem[l][None], g_mem=g_mem[l][None],
        w_mq=w_mq[l].astype(BF16), w_mkv=w_mkv[l].astype(BF16),
        g_qm=_tile_gain(g_qm[l], MEM_HEADS, HEAD_DIM ** -0.5), g_km=_tile_gain(g_km[l], MEM_HEADS),
        w_mo=w_mo[l].astype(BF16), g_ffn=g_ffn[l][None],
        w_r_hi=wr_hi, w_r_lo=wr_lo,
        b_router=jnp.zeros((1, LANES), F32).at[0, :E].set(b_router[l]),
        b1g=b_e1[l][:, None, 0::2], b1l=b_e1[l][:, None, 1::2], b2=b_e2[l][:, None, :],
    )


def _rope_lane_tables(positions, rot_dim, period, offset):
    half = rot_dim // 2
    inv = ROPE_THETA ** (-jnp.arange(0, rot_dim, 2, dtype=F32) / rot_dim)
    ang = positions.astype(F32).reshape(-1)[:, None] * inv
    cos, sin = jnp.cos(ang), jnp.sin(ang)
    lane = np.arange(LANES) % period - offset
    idx = np.clip(lane % half if half else lane, 0, half - 1)
    first = (lane >= 0) & (lane < half)
    second = (lane >= half) & (lane < rot_dim)
    c = jnp.where(first | second, cos[:, idx], 1.0)
    s_dn = jnp.where(second, sin[:, idx], 0.0)
    s_up = jnp.where(first, -sin[:, idx], 0.0)
    return c, s_dn, s_up


def _tables(positions, tm):
    r = np.arange(tm)
    bd = (np.arange(LANES)[:, None] // HEAD_DIM) == (np.arange(LANES)[None, :] // HEAD_DIM)
    rep = np.zeros((LANES, MLA_HEADS * MLA_PAD), np.float32)
    for h in range(MLA_HEADS):
        for j in range(MLA_ROPE_DIM):
            rep[32 + j, h * MLA_PAD + MLA_NOPE_DIM + j] = 1.0
    half = DEINT_BLOCK // 2
    deint = np.zeros((DEINT_BLOCK, DEINT_BLOCK), np.float32)
    deint[2 * np.arange(half), np.arange(half)] = 1.0
    deint[2 * np.arange(half) + 1, half + np.arange(half)] = 1.0
    cm, sdm, sum_ = _rope_lane_tables(positions, MLA_ROPE_DIM, LANES, MLA_NOPE_DIM)
    cd, sdd, sud = _rope_lane_tables(positions, ROT_DIM, HEAD_DIM, 0)
    return dict(
        tri_incl=jnp.asarray(r[:, None] >= r[None, :], BF16),
        tri_strict=jnp.asarray(r[:, None] > r[None, :], BF16),
        bd64=jnp.asarray(bd, BF16), ones=jnp.ones((LANES, LANES), BF16), rep=jnp.asarray(rep, BF16),
        deint=jnp.asarray(deint, BF16),
        cm=cm, sdm=sdm, sum=sum_, cd=cd, sdd=sdd, sud=sud,
    )


def _routing_tables(ti, rk, cnt, n_experts, tr, n_tiles):
    counts = cnt[0, :n_experts].astype(jnp.int32)
    padded = ((counts + tr - 1) // tr) * tr
    ends = jnp.cumsum(padded)
    starts = ends - padded
    idx = ti[:, :TOP_K]
    hit = idx[:, :, None] == jnp.arange(n_experts, dtype=jnp.int32)
    pos = rk[:, :TOP_K] + jnp.sum(jnp.where(hit, starts, 0), axis=-1)
    tile_start = jnp.arange(n_tiles, dtype=jnp.int32) * tr
    tile_expert = jnp.sum((ends[None, :] <= tile_start[:, None]).astype(jnp.int32), axis=1)
    tile_expert = jnp.minimum(tile_expert, n_experts - 1)
    n_used = (ends[n_experts - 1:] // tr).astype(jnp.int32)
    pad_fill = jnp.where(padded > 0, ends - tr, -1)
    fill_start = jnp.concatenate([pad_fill, ends[n_experts - 1:]])
    return pos.astype(jnp.int32), tile_expert.astype(jnp.int32), fill_start.astype(jnp.int32), n_used


def _pick_tile(seq, want):
    t = min(want, seq)
    while seq % t:
        t //= 2
    return t


def kernel(x, mem, positions, g_mix, w_in, b_forget, g_qa, g_ka, g_cq, g_ckv, w_uq, w_ukv, g_qb, g_kb, g_qc, g_kc, g_group, w_out, g_xmem, g_mem, w_mq, w_mkv, g_qm, g_km, w_mo, g_ffn, w_router, b_router, w_e1, b_e1, w_e2, b_e2):
    B, S, D = x.shape
    T = B * S
    depth = w_in.shape[0]
    E = w_e1.shape[1]
    tm = _pick_tile(S, PROJ_TILE)
    tq = _pick_tile(S, ATTN_TILE)
    tr = EXPERT_ROW_TILE
    tmoe = _pick_tile(S, MOE_TOKEN_TILE)
    n_rows = T * TOP_K + (E + 1) * tr
    n_tiles = n_rows // tr

    tabs = _tables(positions, tm)
    x2d = x.reshape(T, D)
    for l in range(depth):
        prm = _prep_layer(l, D, w_in, b_forget, g_mix, g_qa, g_ka, g_cq, g_ckv, w_uq, w_ukv, g_qb, g_kb,
                          g_qc, g_kc, g_group, w_out, g_xmem, g_mem, w_mq, w_mkv, g_qm, g_km, w_mo, g_ffn,
                          w_router, b_router, w_e1, b_e1, w_e2, b_e2)
        qa, ka, va, csum, qb, kb, vb, qc, kc, vc = _inproj(x2d, prm, tabs, S, tm)
        kbias = -csum[:, :FOX_HEADS].reshape(B, S, FOX_HEADS // 2, 2).transpose(0, 2, 3, 1)
        r3 = lambda a: a.reshape(B, S, a.shape[-1])
        oa = _causal_attention(r3(qa), r3(ka), r3(va), kbias, tq=tq, split_heads=False)
        ob = _causal_attention(r3(qb), r3(kb), r3(vb), None, tq=tq, split_heads=True)
        oc = _dilated_attention(r3(qc), r3(kc), r3(vc), S)
        kmem, vmem = _memkv(mem, prm, tabs)
        x2, ti, tw, rk, cnt = _postattn(x2d, oa.reshape(T, -1), ob.reshape(T, -1), oc.reshape(T, -1),
                                        kmem, vmem, prm, tabs, S, tm, E)
        pos, tile_expert, pad_start, n_used = _routing_tables(ti, rk, cnt, E, tr, n_tiles)
        xs = _dispatch(x2, pos, pad_start, n_rows, tmoe, tr)
        ys = _experts(tile_expert, n_used, xs, w_e1, w_e2, l, prm, tabs, tr)
        x2d = _combine(x2, tw, ys, pos, tmoe)
    return x2d.reshape(B, S, D)
```

```python
import functools
import math

import jax
import jax.numpy as jnp
import numpy as np
from jax import lax
from jax.experimental import pallas as pl
from jax.experimental.pallas import tpu as pltpu

HEAD_DIM = 64
FOX_HEADS = 4
MLA_HEADS = 6
DIL_HEADS = 6
MLA_Q_RANK = 384
MLA_KV_RANK = 256
MLA_NOPE_DIM = 64
MLA_ROPE_DIM = 32
MLA_QK_DIM = MLA_NOPE_DIM + MLA_ROPE_DIM
MLA_V_DIM = 64
ROT_DIM = HEAD_DIM // 4
ROPE_THETA = 500000.0
DILATED_PATTERNS = ((128, 1), (512, 4), (2048, 16))
FOX_WIDTH = FOX_HEADS * HEAD_DIM
MLA_WIDTH = MLA_HEADS * MLA_V_DIM
DIL_WIDTH = DIL_HEADS * HEAD_DIM
MEM_HEADS = 4
MEM_WIDTH = MEM_HEADS * HEAD_DIM
TOP_K = 4
SWIGLU_ALPHA = 1.702
SWIGLU_LIMIT = 7.0
RMS_EPS = 1e-6

LANES = 128
VMEM_LIMIT_BYTES = 56 * 1024 * 1024

PROJ_TILE = 512
ATTN_TILE = 256
EXPERT_ROW_TILE = 512
MOE_TOKEN_TILE = 256
ROW_DMA_UNROLL = 4

NEG_BIG = -1e30
MLA_PAD = 128
DEINT_BLOCK = 256

F32 = jnp.float32
BF16 = jnp.bfloat16


def _cparams(sem):
    return pltpu.CompilerParams(dimension_semantics=sem, vmem_limit_bytes=VMEM_LIMIT_BYTES)


def _full(shape):
    nd = len(shape)
    return pl.BlockSpec(shape, lambda *_: (0,) * nd)


def _dot(a, b):
    return jnp.dot(a, b, preferred_element_type=F32)


def _dot_nt(a, b):
    return lax.dot_general(a, b, (((1,), (1,)), ((), ())), preferred_element_type=F32)


def _rms_rows(x, g):
    return x * lax.rsqrt(jnp.mean(x * x, axis=-1, keepdims=True) + RMS_EPS) * g


def _group_norm_tiles(y, ones_mat, inv_n):
    outs = []
    for j in range(y.shape[1] // LANES):
        yj = y[:, j * LANES:(j + 1) * LANES]
        ss = _dot((yj * yj).astype(BF16), ones_mat)
        outs.append(yj * lax.rsqrt(ss * inv_n + RMS_EPS))
    return outs


def _rope_tile(y, half, c, s_dn, s_up):
    return y * c + pltpu.roll(y, half, 1) * s_dn + pltpu.roll(y, LANES - half, 1) * s_up


def _inproj_kernel(x_ref, gmix_ref, w_ref, bmisc_ref, tri_ref, bd64_ref, ones_ref,
                   gqa_ref, gka_ref, gcq_ref, gckv_ref, wuq_ref, wuk_ref, wuv_ref, rep_ref,
                   gqb_ref, gkb_ref, gqc_ref, gkc_ref,
                   cm_ref, sdm_ref, sum_ref, cd_ref, sdd_ref, sud_ref,
                   qa_ref, ka_ref, va_ref, c_ref, qb_ref, kb_ref, vb_ref, qc_ref, kc_ref, vc_ref,
                   carry_ref, *, tiles_per_batch):
    i = pl.program_id(0)
    x = x_ref[...]
    h = _rms_rows(x, gmix_ref[...]).astype(BF16)
    bd64 = bd64_ref[...]
    ones = ones_ref[...]

    def seg(a, b):
        return _dot(h, w_ref[:, a:b])

    o = 0
    qa = seg(o, o + FOX_WIDTH); o += FOX_WIDTH
    ka = seg(o, o + FOX_WIDTH); o += FOX_WIDTH
    va = seg(o, o + FOX_WIDTH); o += FOX_WIDTH
    cq = seg(o, o + MLA_Q_RANK); o += MLA_Q_RANK
    ckv = seg(o, o + MLA_KV_RANK); o += MLA_KV_RANK
    qc = seg(o, o + DIL_WIDTH); o += DIL_WIDTH
    kc = seg(o, o + DIL_WIDTH); o += DIL_WIDTH
    vc = seg(o, o + DIL_WIDTH); o += DIL_WIDTH
    misc = seg(o, o + LANES)

    qa_n = jnp.concatenate(_group_norm_tiles(qa, bd64, 1.0 / HEAD_DIM), axis=1) * gqa_ref[...]
    ka_n = jnp.concatenate(_group_norm_tiles(ka, bd64, 1.0 / HEAD_DIM), axis=1) * gka_ref[...]
    qa_ref[...] = qa_n.astype(BF16)
    ka_ref[...] = ka_n.astype(BF16)
    va_ref[...] = va.astype(BF16)

    z = misc + bmisc_ref[...]
    logf = jnp.minimum(z, 0.0) - jnp.log(1.0 + jnp.exp(-jnp.abs(z)))
    lane = lax.broadcasted_iota(jnp.int32, logf.shape, 1)
    logf = jnp.where(lane < FOX_HEADS, logf, 0.0)
    p0 = logf.astype(BF16)
    r0 = logf - p0.astype(F32)
    p1 = r0.astype(BF16)
    p2 = (r0 - p1.astype(F32)).astype(BF16)
    tri = tri_ref[...]

    @pl.when(i % tiles_per_batch == 0)
    def _():
        carry_ref[...] = jnp.zeros_like(carry_ref)

    csum = _dot(tri, p0) + _dot(tri, p1) + _dot(tri, p2) + carry_ref[...]
    c_ref[...] = csum
    carry_ref[...] = csum[csum.shape[0] - 1:, :]

    cqn = _rms_rows(cq, gcq_ref[...]).astype(BF16)
    ckvn = _rms_rows(ckv, gckv_ref[...]).astype(BF16)
    qb_pre = _dot(cqn, wuq_ref[...])
    kb_pre = _dot(ckvn, wuk_ref[...]) + _dot(misc.astype(BF16), rep_ref[...])
    vb_ref[...] = _dot(ckvn, wuv_ref[...]).astype(BF16)
    cm, sdm, sum_ = cm_ref[...], sdm_ref[...], sum_ref[...]
    half_b = MLA_ROPE_DIM // 2
    qb_t = _group_norm_tiles(qb_pre, ones, 1.0 / MLA_QK_DIM)
    kb_t = _group_norm_tiles(kb_pre, ones, 1.0 / MLA_QK_DIM)
    for j in range(MLA_HEADS):
        sl = slice(j * LANES, (j + 1) * LANES)
        qj = qb_t[j] * gqb_ref[:, sl]
        kj = kb_t[j] * gkb_ref[:, sl]
        qb_ref[:, sl] = _rope_tile(qj, half_b, cm, sdm, sum_).astype(BF16)
        kb_ref[:, sl] = _rope_tile(kj, half_b, cm, sdm, sum_).astype(BF16)

    cd, sdd, sud = cd_ref[...], sdd_ref[...], sud_ref[...]
    half_c = ROT_DIM // 2
    qc_t = _group_norm_tiles(qc, bd64, 1.0 / HEAD_DIM)
    kc_t = _group_norm_tiles(kc, bd64, 1.0 / HEAD_DIM)
    for j in range(DIL_WIDTH // LANES):
        sl = slice(j * LANES, (j + 1) * LANES)
        qc_ref[:, sl] = _rope_tile(qc_t[j] * gqc_ref[:, sl], half_c, cd, sdd, sud)
        kc_ref[:, sl] = _rope_tile(kc_t[j] * gkc_ref[:, sl], half_c, cd, sdd, sud)
    vc_ref[...] = vc


def _inproj(x2d, prm, tabs, seq, tm):
    T, D = x2d.shape
    nm = prm["w_main"].shape[1]
    row = lambda w: pl.BlockSpec((tm, w), lambda i: (i, 0))
    in_specs = [row(D), _full((1, D)), _full((D, nm)), _full((1, LANES)), _full((tm, tm)),
                _full((LANES, LANES)), _full((LANES, LANES)),
                _full((1, FOX_WIDTH)), _full((1, FOX_WIDTH)), _full((1, MLA_Q_RANK)), _full((1, MLA_KV_RANK)),
                _full((MLA_Q_RANK, MLA_HEADS * MLA_PAD)), _full((MLA_KV_RANK, MLA_HEADS * MLA_PAD)),
                _full((MLA_KV_RANK, MLA_WIDTH)), _full((LANES, MLA_HEADS * MLA_PAD)),
                _full((1, MLA_HEADS * MLA_PAD)), _full((1, MLA_HEADS * MLA_PAD)),
                _full((1, DIL_WIDTH)), _full((1, DIL_WIDTH))] + [row(LANES)] * 6
    outs = [(FOX_WIDTH, BF16), (FOX_WIDTH, BF16), (FOX_WIDTH, BF16), (LANES, F32),
            (MLA_HEADS * MLA_PAD, BF16), (MLA_HEADS * MLA_PAD, BF16), (MLA_WIDTH, BF16),
            (DIL_WIDTH, F32), (DIL_WIDTH, F32), (DIL_WIDTH, F32)]
    return pl.pallas_call(
        functools.partial(_inproj_kernel, tiles_per_batch=seq // tm),
        out_shape=[jax.ShapeDtypeStruct((T, w), dt) for w, dt in outs],
        grid=(T // tm,),
        in_specs=in_specs,
        out_specs=[row(w) for w, _ in outs],
        scratch_shapes=[pltpu.VMEM((1, LANES), F32)],
        compiler_params=_cparams(("arbitrary",)),
        name="inproj",
    )(x2d, prm["g_mix"], prm["w_main"], prm["b_misc"], tabs["tri_incl"], tabs["bd64"], tabs["ones"],
      prm["g_qa"], prm["g_ka"], prm["g_cq"], prm["g_ckv"], prm["w_uq"], prm["w_uk"], prm["w_uv"], tabs["rep"],
      prm["g_qb"], prm["g_kb"], prm["g_qc"], prm["g_kc"],
      tabs["cm"], tabs["sdm"], tabs["sum"], tabs["cd"], tabs["sdd"], tabs["sud"])


def _causal_kernel(*refs, seq, tq, split_heads, with_bias):
    if with_bias:
        q_ref, k_ref, v_ref, b_ref, o_ref = refs
    else:
        q_ref, k_ref, v_ref, o_ref = refs
        b_ref = None
    lane = lax.broadcasted_iota(jnp.int32, (tq, LANES), 1)
    lo_half = lane < HEAD_DIM
    r = lax.broadcasted_iota(jnp.int32, (tq, tq), 0)
    c = lax.broadcasted_iota(jnp.int32, (tq, tq), 1)
    keep = c <= r
    for qi in range(seq // tq):
        q = q_ref[0, qi * tq:(qi + 1) * tq, :]
        n_keys = (qi + 1) * tq
        v = v_ref[0, :n_keys, :]
        res = []
        for hh in range(2):
            if split_heads:
                qh = q[:, hh * LANES:(hh + 1) * LANES]
                k = k_ref[0, :n_keys, hh * LANES:(hh + 1) * LANES]
            else:
                zero = jnp.zeros_like(q)
                qh = jnp.where(lo_half, q, zero) if hh == 0 else jnp.where(lo_half, zero, q)
                k = k_ref[0, :n_keys, :]
            s = _dot_nt(qh, k)
            if with_bias:
                s = s + b_ref[0, 0, hh:hh + 1, :n_keys]
            s_diag = jnp.where(keep, s[:, n_keys - tq:], NEG_BIG)
            s = jnp.concatenate([s[:, :n_keys - tq], s_diag], axis=1) if qi else s_diag
            p = jnp.exp(s - jnp.max(s, axis=-1, keepdims=True))
            l = jnp.sum(p, axis=-1, keepdims=True)
            res.append(_dot(p.astype(BF16), v) / l)
        o_ref[0, qi * tq:(qi + 1) * tq, :] = jnp.where(lo_half, res[0], res[1]).astype(o_ref.dtype)


def _causal_attention(q, k, v, kbias, *, tq, split_heads):
    B, S, _ = q.shape
    P = v.shape[2] // LANES
    wq = 2 * LANES if split_heads else LANES
    in_specs = [pl.BlockSpec((1, S, wq), lambda b, p: (b, 0, p)),
                pl.BlockSpec((1, S, wq), lambda b, p: (b, 0, p)),
                pl.BlockSpec((1, S, LANES), lambda b, p: (b, 0, p))]
    args = [q, k, v]
    if kbias is not None:
        in_specs.append(pl.BlockSpec((1, 1, 2, S), lambda b, p: (b, p, 0, 0)))
        args.append(kbias)
    return pl.pallas_call(
        functools.partial(_causal_kernel, seq=S, tq=tq, split_heads=split_heads, with_bias=kbias is not None),
        out_shape=jax.ShapeDtypeStruct((B, S, P * LANES), BF16),
        grid=(B, P),
        in_specs=in_specs,
        out_specs=pl.BlockSpec((1, S, LANES), lambda b, p: (b, 0, p)),
        compiler_params=_cparams(("parallel", "parallel")),
        name="causal_split" if split_heads else "causal_shared",
    )(*args)


def _dilated_kernel(q_ref, k_ref, v_ref, o_ref, acc_ref, m_ref, l_ref, *, seq, n, dilations):
    lane = lax.broadcasted_iota(jnp.int32, (1, 1, LANES), 2)
    lo_half = lane < HEAD_DIM
    row = lax.broadcasted_iota(jnp.int32, (n, n), 0)
    col = lax.broadcasted_iota(jnp.int32, (n, n), 1)
    own_keep = (col <= row)[None]
    prev_keep = (col >= row)[None]

    nblk = seq // n
    for pi, d in enumerate(dilations):
        L = seq // d
        nb = L // n

        def load(ref, d=d, L=L, nb=nb):
            if d == 1:
                return ref[0].reshape(nblk, n, LANES).astype(BF16)
            parts = [ref[0, pl.ds(r, L, stride=d), :].reshape(nb, n, LANES) for r in range(d)]
            return jnp.concatenate(parts, axis=0).astype(BF16)

        qb, kb, vb = load(q_ref), load(k_ref), load(v_ref)
        if nb > 1:
            zero = jnp.zeros((1, n, LANES), BF16)
            kp = jnp.concatenate([zero, kb[:nblk - 1]], axis=0)
            vp = jnp.concatenate([zero, vb[:nblk - 1]], axis=0)
            blk = lax.broadcasted_iota(jnp.int32, (nblk, n, n), 0)
            prev_ok = jnp.logical_and(prev_keep, lax.rem(blk, nb) != 0)
        zq = jnp.zeros_like(qb)
        heads = []
        for hh in range(2):
            qh = jnp.where(lo_half, qb, zq) if hh == 0 else jnp.where(lo_half, zq, qb)
            s_own = jnp.einsum("bqd,bkd->bqk", qh, kb, preferred_element_type=F32)
            s_own = jnp.where(own_keep, s_own, NEG_BIG)
            m = jnp.max(s_own, axis=-1, keepdims=True)
            if nb > 1:
                s_prev = jnp.einsum("bqd,bkd->bqk", qh, kp, preferred_element_type=F32)
                s_prev = jnp.where(prev_ok, s_prev, NEG_BIG)
                m = jnp.maximum(m, jnp.max(s_prev, axis=-1, keepdims=True))
            p_own = jnp.exp(s_own - m)
            l = jnp.sum(p_own, axis=-1, keepdims=True)
            pv = jnp.einsum("bqk,bkd->bqd", p_own.astype(BF16), vb, preferred_element_type=F32)
            if nb > 1:
                p_prev = jnp.exp(s_prev - m)
                l = l + jnp.sum(p_prev, axis=-1, keepdims=True)
                pv = pv + jnp.einsum("bqk,bkd->bqd", p_prev.astype(BF16), vp, preferred_element_type=F32)
            heads.append((m, l, pv))
        m_f = jnp.where(lo_half, heads[0][0], heads[1][0])
        l_f = jnp.where(lo_half, heads[0][1], heads[1][1])
        a_f = jnp.where(lo_half, heads[0][2], heads[1][2])
        for dst, val in ((acc_ref, a_f), (m_ref, m_f), (l_ref, l_f)):
            if d == 1:
                dst[pi] = val.reshape(seq, LANES)
            else:
                for r in range(d):
                    dst[pi, pl.ds(r, L, stride=d), :] = val[r * nb:(r + 1) * nb].reshape(L, LANES)


    np_ = len(dilations)
    m_all = m_ref[0]
    for pi in range(1, np_):
        m_all = jnp.maximum(m_all, m_ref[pi])
    num = jnp.zeros((seq, LANES), F32)
    den = jnp.zeros((seq, LANES), F32)
    for pi in range(np_):
        w = jnp.exp(m_ref[pi] - m_all)
        num = num + w * acc_ref[pi]
        den = den + w * l_ref[pi]
    o_ref[0] = num / den


def _dilated_attention(qc, kc, vc, seq):
    B = qc.shape[0]
    n = DILATED_PATTERNS[0][0] // DILATED_PATTERNS[0][1]
    dil = tuple(d for _, d in DILATED_PATTERNS)
    for w, d in DILATED_PATTERNS:
        assert w // d == n and seq % (d * n) == 0
    spec = pl.BlockSpec((1, seq, LANES), lambda b, p: (b, 0, p))
    np_ = len(dil)
    return pl.pallas_call(
        functools.partial(_dilated_kernel, seq=seq, n=n, dilations=dil),
        out_shape=jax.ShapeDtypeStruct((B, seq, DIL_WIDTH), F32),
        grid=(B, DIL_WIDTH // LANES),
        in_specs=[spec, spec, spec],
        out_specs=spec,
        scratch_shapes=[pltpu.VMEM((np_, seq, LANES), F32)] * 3,
        compiler_params=_cparams(("parallel", "parallel")),
        name="dilated",
    )(qc, kc, vc)


def _memkv_kernel(mem_ref, g_ref, w_ref, gk_ref, bd64_ref, k_ref, v_ref):
    mn = _rms_rows(mem_ref[0], g_ref[...]).astype(BF16)
    kv = _dot(mn, w_ref[...])
    k = jnp.concatenate(_group_norm_tiles(kv[:, :MEM_WIDTH], bd64_ref[...], 1.0 / HEAD_DIM), axis=1)
    k_ref[0] = (k * gk_ref[...]).astype(BF16)
    v_ref[0] = kv[:, MEM_WIDTH:].astype(BF16)


def _memkv(mem, prm, tabs):
    B, M, D = mem.shape
    return pl.pallas_call(
        _memkv_kernel,
        out_shape=[jax.ShapeDtypeStruct((B, M, MEM_WIDTH), BF16)] * 2,
        grid=(B,),
        in_specs=[pl.BlockSpec((1, M, D), lambda b: (b, 0, 0)), _full((1, D)), _full((D, 2 * MEM_WIDTH)),
                  _full((1, MEM_WIDTH)), _full((LANES, LANES))],
        out_specs=[pl.BlockSpec((1, M, MEM_WIDTH), lambda b: (b, 0, 0))] * 2,
        compiler_params=_cparams(("parallel",)),
        name="memkv",
    )(mem, prm["g_mem"], prm["w_mkv"], prm["g_km"], tabs["bd64"])


def _postattn_kernel(x_ref, oa_ref, ob_ref, oc_ref, gg_ref, wout_ref, gx_ref, wmq_ref, gqm_ref, bd64_ref,
                     km_ref, vm_ref, wmo_ref, gf_ref, wrh_ref, wrl_ref, br_ref, tri_ref,
                     x2_ref, ti_ref, tw_ref, rk_ref, cnt_ref, carry_ref, *, n_experts):
    i = pl.program_id(0)
    tm = x_ref.shape[0]
    x = x_ref[...]
    a0, a1 = FOX_WIDTH, FOX_WIDTH + MLA_WIDTH
    ma = _rms_rows(oa_ref[...].astype(F32), gg_ref[:, :a0]).astype(BF16)
    mb = _rms_rows(ob_ref[...].astype(F32), gg_ref[:, a0:a1]).astype(BF16)
    mc = _rms_rows(oc_ref[...], gg_ref[:, a1:]).astype(BF16)
    x1 = x + _dot(ma, wout_ref[:a0, :]) + _dot(mb, wout_ref[a0:a1, :]) + _dot(mc, wout_ref[a1:, :])

    h2 = _rms_rows(x1, gx_ref[...]).astype(BF16)
    qm = _dot(h2, wmq_ref[...])
    qm_t = _group_norm_tiles(qm, bd64_ref[...], 1.0 / HEAD_DIM)
    lane = lax.broadcasted_iota(jnp.int32, (tm, LANES), 1)
    lo_half = lane < HEAD_DIM
    outs = []
    for p in range(MEM_WIDTH // LANES):
        sl = slice(p * LANES, (p + 1) * LANES)
        q2 = (qm_t[p] * gqm_ref[:, sl]).astype(BF16)
        k2 = km_ref[0, :, sl]
        v2 = vm_ref[0, :, sl]
        zero = jnp.zeros_like(q2)
        res = []
        for hh in range(2):
            qh = jnp.where(lo_half, q2, zero) if hh == 0 else jnp.where(lo_half, zero, q2)
            s = _dot_nt(qh, k2)
            pr = jnp.exp(s - jnp.max(s, axis=-1, keepdims=True))
            l = jnp.sum(pr, axis=-1, keepdims=True)
            res.append(_dot(pr.astype(BF16), v2) / l)
        outs.append(jnp.where(lo_half, res[0], res[1]))
    om = jnp.concatenate(outs, axis=1).astype(BF16)
    x2 = x1 + _dot(om, wmo_ref[...])
    x2_ref[...] = x2

    h3 = _rms_rows(x2, gf_ref[...])
    h3_hi = h3.astype(BF16)
    h3_lo = (h3 - h3_hi.astype(F32)).astype(BF16)
    logits = _dot(h3_hi, wrh_ref[...]) + _dot(h3_hi, wrl_ref[...]) + _dot(h3_lo, wrh_ref[...]) + br_ref[...]
    logits = jnp.where(lane < n_experts, logits, NEG_BIG)
    work = logits
    lane_f = lane.astype(F32)
    sel = jnp.zeros((tm, LANES), F32)
    vals, idxs = [], []
    for _ in range(TOP_K):
        mv = jnp.max(work, axis=-1, keepdims=True)
        ix = jnp.min(jnp.where(work == mv, lane_f, float(LANES)), axis=-1, keepdims=True)
        hit = lane_f == ix
        work = jnp.where(hit, NEG_BIG, work)
        sel = jnp.where(hit, 1.0, sel)
        vals.append(mv)
        idxs.append(ix)
    es = [jnp.exp(v - vals[0]) for v in vals]
    den = es[0] + es[1] + es[2] + es[3]

    @pl.when(i == 0)
    def _():
        carry_ref[...] = jnp.zeros_like(carry_ref)

    before = _dot(tri_ref[...], sel.astype(BF16)) + carry_ref[...]
    total = before[tm - 1:, :] + sel[tm - 1:, :]
    carry_ref[...] = total
    cnt_ref[...] = total

    ti = jnp.zeros((tm, LANES), F32)
    tw = jnp.zeros((tm, LANES), F32)
    rk = jnp.zeros((tm, LANES), F32)
    for kk in range(TOP_K):
        at = lane == kk
        ti = jnp.where(at, idxs[kk], ti)
        tw = jnp.where(at, es[kk] / den, tw)
        r_k = jnp.sum(jnp.where(lane_f == idxs[kk], before, 0.0), axis=-1, keepdims=True)
        rk = jnp.where(at, r_k, rk)
    ti_ref[...] = ti.astype(jnp.int32)
    tw_ref[...] = tw
    rk_ref[...] = rk.astype(jnp.int32)


def _postattn(x2d, oa, ob, oc, kmem, vmem, prm, tabs, seq, tm, n_experts):
    T, D = x2d.shape
    M = kmem.shape[1]
    tpb = seq // tm
    row = lambda w: pl.BlockSpec((tm, w), lambda i: (i, 0))
    memspec = pl.BlockSpec((1, M, MEM_WIDTH), lambda i: (i // tpb, 0, 0))
    mix = FOX_WIDTH + MLA_WIDTH + DIL_WIDTH
    return pl.pallas_call(
        functools.partial(_postattn_kernel, n_experts=n_experts),
        out_shape=[jax.ShapeDtypeStruct((T, D), F32),
                   jax.ShapeDtypeStruct((T, LANES), jnp.int32), jax.ShapeDtypeStruct((T, LANES), F32),
                   jax.ShapeDtypeStruct((T, LANES), jnp.int32), jax.ShapeDtypeStruct((1, LANES), F32)],
        grid=(T // tm,),
        in_specs=[row(D), row(FOX_WIDTH), row(MLA_WIDTH), row(DIL_WIDTH), _full((1, mix)), _full((mix, D)),
                  _full((1, D)), _full((D, MEM_WIDTH)), _full((1, MEM_WIDTH)), _full((LANES, LANES)),
                  memspec, memspec, _full((MEM_WIDTH, D)), _full((1, D)),
                  _full((D, LANES)), _full((D, LANES)), _full((1, LANES)), _full((tm, tm))],
        out_specs=[row(D), row(LANES), row(LANES), row(LANES), _full((1, LANES))],
        scratch_shapes=[pltpu.VMEM((1, LANES), F32)],
        compiler_params=_cparams(("arbitrary",)),
        name="postattn",
    )(x2d, oa, ob, oc, prm["g_group"], prm["w_out"], prm["g_xmem"], prm["w_mq"], prm["g_qm"], tabs["bd64"],
      kmem, vmem, prm["w_mo"], prm["g_ffn"], prm["w_r_hi"], prm["w_r_lo"], prm["b_router"], tabs["tri_strict"])


def _row_copy(src_ref, src_row, dst_ref, dst_row, sem):
    return pltpu.make_async_copy(src_ref.at[pl.ds(src_row, 1), :], dst_ref.at[pl.ds(dst_row, 1), :], sem)


def _experts_kernel(te_ref, nu_ref, src0_ref, src_ref, dst_ref, x_ref, gf_ref, w1_ref, b1g_ref, b1l_ref,
                    w2_ref, b2_ref, perm_ref, y_ref, w1s_ref, w2s_ref, xbuf, obuf, gsem, ssem, *, tr):
    i = pl.program_id(0)
    n_used = nu_ref[0]
    slot = i % 2
    other = 1 - slot
    two_f = w1_ref.shape[3]
    n_blk = two_f // DEINT_BLOCK
    half = DEINT_BLOCK // 2

    def gather(idx_ref, to_slot):
        def pair(q, _):
            for p in range(2):
                r = 2 * q + p
                _row_copy(x_ref, idx_ref[0, 0, r], xbuf.at[to_slot], r, gsem.at[to_slot]).start(priority=p)
            return 0
        lax.fori_loop(0, tr // 2, pair, 0, unroll=ROW_DMA_UNROLL)

    def scatter(from_slot):
        def pair(q, _):
            for p in range(2):
                r = 2 * q + p
                _row_copy(obuf.at[from_slot], r, y_ref, dst_ref[0, 0, r], ssem.at[from_slot]).start(priority=p)
            return 0
        lax.fori_loop(0, tr // 2, pair, 0, unroll=ROW_DMA_UNROLL)

    def wait_gather(s):
        pltpu.make_async_copy(xbuf.at[s], xbuf.at[s], gsem.at[s]).wait()

    def wait_scatter(s):
        pltpu.make_async_copy(obuf.at[s], obuf.at[s], ssem.at[s]).wait()

    @pl.when(i == 0)
    def _():
        obuf[1] = jnp.zeros((tr, obuf.shape[2]), obuf.dtype)
        gather(src0_ref, 0)

    @pl.when(i < n_used)
    def _():
        @pl.when(i >= 1)
        def _():
            wait_scatter(slot)
        wait_gather(slot)

        @pl.when(jnp.logical_or(i == 0, te_ref[i] != te_ref[jnp.maximum(i - 1, 0)]))
        def _():
            perm = perm_ref[...]
            for b in range(n_blk):
                sl = slice(b * DEINT_BLOCK, (b + 1) * DEINT_BLOCK)
                w1s_ref[:, sl] = _dot(w1_ref[0, 0, :, sl].astype(BF16), perm).astype(BF16)
            w2s_ref[...] = w2_ref[0, 0].astype(BF16)

        scatter(other)
        gather(src_ref, other)
        x = _rms_rows(xbuf[slot], gf_ref[...]).astype(BF16)
        h = _dot(x, w1s_ref[...])
        acts = []
        for b in range(n_blk):
            glu = h[:, b * DEINT_BLOCK:b * DEINT_BLOCK + half] + b1g_ref[0, :, b * half:(b + 1) * half]
            lin = h[:, b * DEINT_BLOCK + half:(b + 1) * DEINT_BLOCK] + b1l_ref[0, :, b * half:(b + 1) * half]
            glu = jnp.minimum(glu, SWIGLU_LIMIT)
            lin = jnp.clip(lin, -SWIGLU_LIMIT, SWIGLU_LIMIT)
            acts.append((glu * (1.0 / (1.0 + jnp.exp(-SWIGLU_ALPHA * glu))) * (lin + 1.0)).astype(BF16))
        act = jnp.concatenate(acts, axis=1) if n_blk > 1 else acts[0]
        obuf[slot] = _dot(act, w2s_ref[...]) + b2_ref[0]

    @pl.when(i == n_used)
    def _():
        wait_scatter(slot)
        wait_gather(slot)
        scatter(other)
        wait_scatter(other)


def _experts(tables, x2, w_e1, w_e2, layer, prm, tabs, tr):
    T, D = x2.shape
    two_f = w_e1.shape[3]
    F = two_f // 2
    assert two_f % DEINT_BLOCK == 0
    n_tiles = tables["tile_expert"].shape[0]
    idx = lambda off: pl.BlockSpec((1, 1, tr), lambda i, te, nu: (jnp.minimum(i + off, n_tiles), 0, 0),
                                   memory_space=pltpu.SMEM)
    wspec = lambda s: pl.BlockSpec((1,) + s, lambda i, te, nu: (te[jnp.minimum(i, n_tiles - 1)], 0, 0))
    lspec = lambda s: pl.BlockSpec((1, 1) + s, lambda i, te, nu: (layer, te[jnp.minimum(i, n_tiles - 1)], 0, 0))
    return pl.pallas_call(
        functools.partial(_experts_kernel, tr=tr),
        out_shape=jax.ShapeDtypeStruct((T * TOP_K + tr, D), F32),
        grid_spec=pltpu.PrefetchScalarGridSpec(
            num_scalar_prefetch=2,
            grid=(n_tiles + 1,),
            in_specs=[pl.BlockSpec((1, 1, tr), lambda i, te, nu: (0, 0, 0), memory_space=pltpu.SMEM),
                      idx(1), idx(0),
                      pl.BlockSpec(memory_space=pl.ANY), pl.BlockSpec((1, D), lambda i, te, nu: (0, 0)),
                      lspec((D, two_f)), wspec((1, F)), wspec((1, F)), lspec((F, D)), wspec((1, D)),
                      pl.BlockSpec((DEINT_BLOCK, DEINT_BLOCK), lambda i, te, nu: (0, 0))],
            out_specs=pl.BlockSpec(memory_space=pl.ANY),
            scratch_shapes=[pltpu.VMEM((D, two_f), BF16), pltpu.VMEM((F, D), BF16),
                            pltpu.VMEM((2, tr, D), F32), pltpu.VMEM((2, tr, D), F32),
                            pltpu.SemaphoreType.DMA((2,)), pltpu.SemaphoreType.DMA((2,))],
        ),
        compiler_params=_cparams(("arbitrary",)),
        name="experts",
    )(tables["tile_expert"], tables["n_used"], tables["src"], tables["src"], tables["dst"], x2, prm["g_ffn"],
      w_e1, prm["b1g"], prm["b1l"], w_e2, prm["b2"], tabs["deint"])


def _combine_kernel(x_ref, tw_ref, *refs):
    o_ref = refs[TOP_K]
    tw = tw_ref[...]
    acc = x_ref[...]
    for kk in range(TOP_K):
        acc = acc + tw[:, kk:kk + 1] * refs[kk][...]
    o_ref[...] = acc


def _combine(x2, tw, y, tm):
    T, D = x2.shape
    nt = T // tm
    return pl.pallas_call(
        _combine_kernel,
        out_shape=jax.ShapeDtypeStruct((T, D), F32),
        grid=(nt,),
        in_specs=[pl.BlockSpec((tm, D), lambda i: (i, 0)),
                  pl.BlockSpec((tm, LANES), lambda i: (i, 0))]
                 + [pl.BlockSpec((tm, D), lambda i, kk=kk: (kk * nt + i, 0)) for kk in range(TOP_K)],
        out_specs=pl.BlockSpec((tm, D), lambda i: (i, 0)),
        compiler_params=_cparams(("parallel",)),
        name="combine",
    )(x2, tw, *([y] * TOP_K))


def _tile_gain(g, reps, scale=1.0):
    return (jnp.tile(g.astype(F32), reps) * scale)[None, :]


def _prep_layer(l, D, w_in, b_forget, g_mix, g_qa, g_ka, g_cq, g_ckv, w_uq, w_ukv, g_qb, g_kb, g_qc, g_kc,
                g_group, w_out, g_xmem, g_mem, w_mq, w_mkv, g_qm, g_km, w_mo, g_ffn, w_router, b_router,
                w_e1, b_e1, w_e2, b_e2):
    wi = w_in[l]
    sizes = (FOX_WIDTH, FOX_WIDTH, FOX_WIDTH, FOX_HEADS, MLA_Q_RANK, MLA_KV_RANK, MLA_ROPE_DIM,
             DIL_WIDTH, DIL_WIDTH, DIL_WIDTH)
    pts = np.cumsum((0,) + sizes)
    qa, ka, va, fa, cq, ckv, kr, qc, kc, vc = [wi[:, pts[j]:pts[j + 1]] for j in range(len(sizes))]
    misc = jnp.zeros((D, LANES), F32).at[:, :FOX_HEADS].set(fa).at[:, 32:32 + MLA_ROPE_DIM].set(kr)
    w_main = jnp.concatenate([qa, ka, va, cq, ckv, qc, kc, vc, misc], axis=1).astype(BF16)
    b_misc = jnp.zeros((1, LANES), F32).at[0, :FOX_HEADS].set(b_forget[l])

    uq = w_uq[l].reshape(MLA_Q_RANK, MLA_HEADS, MLA_QK_DIM)
    uq = jnp.pad(uq, ((0, 0), (0, 0), (0, MLA_PAD - MLA_QK_DIM))).reshape(MLA_Q_RANK, MLA_HEADS * MLA_PAD)
    ukv = w_ukv[l].reshape(MLA_KV_RANK, MLA_HEADS, MLA_NOPE_DIM + MLA_V_DIM)
    uk = jnp.pad(ukv[:, :, :MLA_NOPE_DIM], ((0, 0), (0, 0), (0, MLA_PAD - MLA_NOPE_DIM)))
    uk = uk.reshape(MLA_KV_RANK, MLA_HEADS * MLA_PAD)
    uv = ukv[:, :, MLA_NOPE_DIM:].reshape(MLA_KV_RANK, MLA_WIDTH)
    pad_gain = lambda g: jnp.pad(g.astype(F32), (0, MLA_PAD - MLA_QK_DIM))

    E = w_e1.shape[1]
    wr = jnp.zeros((D, LANES), F32).at[:, :E].set(w_router[l])
    wr_hi = wr.astype(BF16)
    wr_lo = (wr - wr_hi.astype(F32)).astype(BF16)
    return dict(
        g_mix=g_mix[l][None], w_main=w_main, b_misc=b_misc,
        g_qa=_tile_gain(g_qa[l], FOX_HEADS, HEAD_DIM ** -0.5), g_ka=_tile_gain(g_ka[l], FOX_HEADS),
        g_cq=g_cq[l][None], g_ckv=g_ckv[l][None],
        w_uq=uq.astype(BF16), w_uk=uk.astype(BF16), w_uv=uv.astype(BF16),
        g_qb=_tile_gain(pad_gain(g_qb[l]), MLA_HEADS, MLA_QK_DIM ** -0.5),
        g_kb=_tile_gain(pad_gain(g_kb[l]), MLA_HEADS),
        g_qc=_tile_gain(g_qc[l], DIL_HEADS, HEAD_DIM ** -0.5), g_kc=_tile_gain(g_kc[l], DIL_HEADS),
        g_group=g_group[l][None], w_out=w_out[l].astype(BF16),
        g_xmem=g_xmem[l][None], g_mem=g_mem[l][None],
        w_mq=w_mq[l].astype(BF16), w_mkv=w_mkv[l].astype(BF16),
        g_qm=_tile_gain(g_qm[l], MEM_HEADS, HEAD_DIM ** -0.5), g_km=_tile_gain(g_km[l], MEM_HEADS),
        w_mo=w_mo[l].astype(BF16), g_ffn=g_ffn[l][None],
        w_r_hi=wr_hi, w_r_lo=wr_lo,
        b_router=jnp.zeros((1, LANES), F32).at[0, :E].set(b_router[l]),
        b1g=b_e1[l][:, None, 0::2], b1l=b_e1[l][:, None, 1::2], b2=b_e2[l][:, None, :],
    )


def _rope_lane_tables(positions, rot_dim, period, offset):
    half = rot_dim // 2
    inv = ROPE_THETA ** (-jnp.arange(0, rot_dim, 2, dtype=F32) / rot_dim)
    ang = positions.astype(F32).reshape(-1)[:, None] * inv
    cos, sin = jnp.cos(ang), jnp.sin(ang)
    lane = np.arange(LANES) % period - offset
    idx = np.clip(lane % half if half else lane, 0, half - 1)
    first = (lane >= 0) & (lane < half)
    second = (lane >= half) & (lane < rot_dim)
    c = jnp.where(first | second, cos[:, idx], 1.0)
    s_dn = jnp.where(second, sin[:, idx], 0.0)
    s_up = jnp.where(first, -sin[:, idx], 0.0)
    return c, s_dn, s_up


def _tables(positions, tm):
    r = np.arange(tm)
    bd = (np.arange(LANES)[:, None] // HEAD_DIM) == (np.arange(LANES)[None, :] // HEAD_DIM)
    rep = np.zeros((LANES, MLA_HEADS * MLA_PAD), np.float32)
    for h in range(MLA_HEADS):
        for j in range(MLA_ROPE_DIM):
            rep[32 + j, h * MLA_PAD + MLA_NOPE_DIM + j] = 1.0
    half = DEINT_BLOCK // 2
    deint = np.zeros((DEINT_BLOCK, DEINT_BLOCK), np.float32)
    deint[2 * np.arange(half), np.arange(half)] = 1.0
    deint[2 * np.arange(half) + 1, half + np.arange(half)] = 1.0
    cm, sdm, sum_ = _rope_lane_tables(positions, MLA_ROPE_DIM, LANES, MLA_NOPE_DIM)
    cd, sdd, sud = _rope_lane_tables(positions, ROT_DIM, HEAD_DIM, 0)
    return dict(
        tri_incl=jnp.asarray(r[:, None] >= r[None, :], BF16),
        tri_strict=jnp.asarray(r[:, None] > r[None, :], BF16),
        bd64=jnp.asarray(bd, BF16), ones=jnp.ones((LANES, LANES), BF16), rep=jnp.asarray(rep, BF16),
        deint=jnp.asarray(deint, BF16),
        cm=cm, sdm=sdm, sum=sum_, cd=cd, sdd=sdd, sud=sud,
    )


def _moe_tables(ti, cnt, n_experts, tr):
    T = ti.shape[0]
    n_pairs = T * TOP_K
    n_tiles = n_pairs // tr + n_experts
    counts = cnt[0, :n_experts].astype(jnp.int32)
    pad = (-counts) % tr
    ends = jnp.cumsum(counts + pad)
    experts = jnp.arange(n_experts, dtype=jnp.int32)
    filler_key = jnp.where(jnp.arange(tr, dtype=jnp.int32)[None, :] < pad[:, None], experts[:, None], n_experts)
    keys = jnp.concatenate([ti[:, :TOP_K].reshape(-1), filler_key.reshape(-1)])
    vals = jnp.concatenate([jnp.arange(n_pairs, dtype=jnp.int32), jnp.full((n_experts * tr,), -1, jnp.int32)])
    _, pair = lax.sort((keys, vals), num_keys=1, is_stable=True)
    real = pair >= 0
    spare = n_pairs + jnp.arange(n_tiles * tr, dtype=jnp.int32) % tr
    src = jnp.where(real, pair // TOP_K, 0)
    dst = jnp.where(real, (pair % TOP_K) * T + pair // TOP_K, spare)
    one_tile = lambda v: jnp.full((1, 1, tr), v, jnp.int32)
    tile_start = jnp.arange(n_tiles, dtype=jnp.int32) * tr
    tile_expert = jnp.sum((ends[None, :] <= tile_start[:, None]).astype(jnp.int32), axis=1)
    return dict(
        tile_expert=jnp.minimum(tile_expert, n_experts - 1).astype(jnp.int32),
        n_used=(ends[n_experts - 1:] // tr).astype(jnp.int32),
        src=jnp.concatenate([src.reshape(n_tiles, 1, tr), one_tile(0)]),
        dst=jnp.concatenate([spare[:tr].reshape(1, 1, tr), dst.reshape(n_tiles, 1, tr)]),
    )


def _pick_tile(seq, want):
    t = min(want, seq)
    while seq % t:
        t //= 2
    return t


def kernel(x, mem, positions, g_mix, w_in, b_forget, g_qa, g_ka, g_cq, g_ckv, w_uq, w_ukv, g_qb, g_kb, g_qc, g_kc, g_group, w_out, g_xmem, g_mem, w_mq, w_mkv, g_qm, g_km, w_mo, g_ffn, w_router, b_router, w_e1, b_e1, w_e2, b_e2):
    B, S, D = x.shape
    T = B * S
    depth = w_in.shape[0]
    E = w_e1.shape[1]
    tm = _pick_tile(S, PROJ_TILE)
    tq = _pick_tile(S, ATTN_TILE)
    tr = EXPERT_ROW_TILE
    tmoe = _pick_tile(S, MOE_TOKEN_TILE)
    assert (T * TOP_K) % tr == 0

    tabs = _tables(positions, tm)
    x2d = x.reshape(T, D)
    for l in range(depth):
        prm = _prep_layer(l, D, w_in, b_forget, g_mix, g_qa, g_ka, g_cq, g_ckv, w_uq, w_ukv, g_qb, g_kb,
                          g_qc, g_kc, g_group, w_out, g_xmem, g_mem, w_mq, w_mkv, g_qm, g_km, w_mo, g_ffn,
                          w_router, b_router, w_e1, b_e1, w_e2, b_e2)
        qa, ka, va, csum, qb, kb, vb, qc, kc, vc = _inproj(x2d, prm, tabs, S, tm)
        kbias = -csum[:, :FOX_HEADS].reshape(B, S, FOX_HEADS // 2, 2).transpose(0, 2, 3, 1)
        r3 = lambda a: a.reshape(B, S, a.shape[-1])
        oa = _causal_attention(r3(qa), r3(ka), r3(va), kbias, tq=tq, split_heads=False)
        ob = _causal_attention(r3(qb), r3(kb), r3(vb), None, tq=tq, split_heads=True)
        oc = _dilated_attention(r3(qc), r3(kc), r3(vc), S)
        kmem, vmem = _memkv(mem, prm, tabs)
        x2, ti, tw, rk, cnt = _postattn(x2d, oa.reshape(T, -1), ob.reshape(T, -1), oc.reshape(T, -1),
                                        kmem, vmem, prm, tabs, S, tm, E)
        y = _experts(_moe_tables(ti, cnt, E, tr), x2, w_e1, w_e2, l, prm, tabs, tr)
        x2d = _combine(x2, tw, y, tmoe)
    return x2d.reshape(B, S, D)
```

```python
import functools
import math

import jax
import jax.numpy as jnp
import numpy as np
from jax import lax
from jax.experimental import pallas as pl
from jax.experimental.pallas import tpu as pltpu

HEAD_DIM = 64
FOX_HEADS = 4
MLA_HEADS = 6
DIL_HEADS = 6
MLA_Q_RANK = 384
MLA_KV_RANK = 256
MLA_NOPE_DIM = 64
MLA_ROPE_DIM = 32
MLA_QK_DIM = MLA_NOPE_DIM + MLA_ROPE_DIM
MLA_V_DIM = 64
ROT_DIM = HEAD_DIM // 4
ROPE_THETA = 500000.0
DILATED_PATTERNS = ((128, 1), (512, 4), (2048, 16))
FOX_WIDTH = FOX_HEADS * HEAD_DIM
MLA_WIDTH = MLA_HEADS * MLA_V_DIM
DIL_WIDTH = DIL_HEADS * HEAD_DIM
MEM_HEADS = 4
MEM_WIDTH = MEM_HEADS * HEAD_DIM
TOP_K = 4
SWIGLU_ALPHA = 1.702
SWIGLU_LIMIT = 7.0
RMS_EPS = 1e-6
LOG2E = math.log2(math.e)

LANES = 128
VMEM_LIMIT_BYTES = 56 * 1024 * 1024

PROJ_TILE = 512
PROJ_SPLIT = 2
ATTN_TILE = 256
EXPERT_ROW_TILE = 512
MOE_TOKEN_TILE = 256
ROW_DMA_UNROLL = 8

NEG_BIG = -1e30
MLA_PAD = 128
DEINT_BLOCK = 256

F32 = jnp.float32
BF16 = jnp.bfloat16


def _cparams(sem):
    return pltpu.CompilerParams(dimension_semantics=sem, vmem_limit_bytes=VMEM_LIMIT_BYTES)


def _full(shape):
    nd = len(shape)
    return pl.BlockSpec(shape, lambda *_: (0,) * nd)


def _dot(a, b):
    return jnp.dot(a, b, preferred_element_type=F32)


def _dot_nt(a, b):
    return lax.dot_general(a, b, (((1,), (1,)), ((), ())), preferred_element_type=F32)


def _rms_rows(x, g):
    return x * lax.rsqrt(jnp.mean(x * x, axis=-1, keepdims=True) + RMS_EPS) * g


def _group_norm_tiles(y, ones_mat, inv_n):
    outs = []
    for j in range(y.shape[1] // LANES):
        yj = y[:, j * LANES:(j + 1) * LANES]
        ss = _dot((yj * yj).astype(BF16), ones_mat)
        outs.append(yj * lax.rsqrt(ss * inv_n + RMS_EPS))
    return outs


def _rope_tile(y, half, c, s_dn, s_up):
    return y * c + pltpu.roll(y, half, 1) * s_dn + pltpu.roll(y, LANES - half, 1) * s_up


def _interleave(gens):
    live = list(gens)
    while live:
        for g in list(live):
            try:
                next(g)
            except StopIteration:
                live.remove(g)


def _inproj_kernel(x_ref, gmix_ref, w_ref, bmisc_ref, tri_ref, bd64_ref, ones_ref,
                   gqa_ref, gka_ref, gcq_ref, gckv_ref, wuq_ref, wuk_ref, wuv_ref, rep_ref,
                   gqb_ref, gkb_ref, gqc_ref, gkc_ref,
                   cm_ref, sdm_ref, sum_ref, cd_ref, sdd_ref, sud_ref,
                   qa_ref, ka_ref, va_ref, c_ref, qb_ref, kb_ref, vb_ref, qc_ref, kc_ref, vc_ref,
                   carry_ref, *, tiles_per_batch):
    i = pl.program_id(0)
    tm = x_ref.shape[0]
    hm = tm // PROJ_SPLIT
    bd64 = bd64_ref[...]
    ones = ones_ref[...]
    tri = tri_ref[...]
    half_b = MLA_ROPE_DIM // 2
    half_c = ROT_DIM // 2
    offs = np.cumsum((0, FOX_WIDTH, FOX_WIDTH, FOX_WIDTH, MLA_Q_RANK, MLA_KV_RANK,
                      DIL_WIDTH, DIL_WIDTH, DIL_WIDTH, LANES))

    @pl.when(i % tiles_per_batch == 0)
    def _():
        carry_ref[...] = jnp.zeros_like(carry_ref)

    carry = {"c": carry_ref[...]}

    def rows(g):
        rs = slice(g * hm, (g + 1) * hm)
        h = _rms_rows(x_ref[rs, :], gmix_ref[...]).astype(BF16)
        seg = lambda j: _dot(h, w_ref[:, offs[j]:offs[j + 1]])
        yield
        qa, ka, va, cq = seg(0), seg(1), seg(2), seg(3)
        yield
        qa_ref[rs, :] = (jnp.concatenate(_group_norm_tiles(qa, bd64, 1.0 / HEAD_DIM), axis=1)
                         * gqa_ref[...]).astype(BF16)
        ka_ref[rs, :] = (jnp.concatenate(_group_norm_tiles(ka, bd64, 1.0 / HEAD_DIM), axis=1)
                         * gka_ref[...]).astype(BF16)
        va_ref[rs, :] = va.astype(BF16)
        cqn = _rms_rows(cq, gcq_ref[...]).astype(BF16)
        yield
        ckv, qc = seg(4), seg(5)
        qb_pre = _dot(cqn, wuq_ref[...])
        yield
        ckvn = _rms_rows(ckv, gckv_ref[...]).astype(BF16)
        cd, sdd, sud = cd_ref[rs, :], sdd_ref[rs, :], sud_ref[rs, :]
        qc_t = _group_norm_tiles(qc, bd64, 1.0 / HEAD_DIM)
        for j in range(DIL_WIDTH // LANES):
            sl = slice(j * LANES, (j + 1) * LANES)
            qc_ref[rs, sl] = _rope_tile(qc_t[j] * gqc_ref[:, sl], half_c, cd, sdd, sud)
        yield
        kc, vc, misc = seg(6), seg(7), seg(8)
        kb_nope = _dot(ckvn, wuk_ref[...])
        vb_ref[rs, :] = _dot(ckvn, wuv_ref[...]).astype(BF16)
        yield
        cm, sdm, sum_ = cm_ref[rs, :], sdm_ref[rs, :], sum_ref[rs, :]
        qb_t = _group_norm_tiles(qb_pre, ones, 1.0 / MLA_QK_DIM)
        for j in range(MLA_HEADS):
            sl = slice(j * LANES, (j + 1) * LANES)
            qb_ref[rs, sl] = _rope_tile(qb_t[j] * gqb_ref[:, sl], half_b, cm, sdm, sum_).astype(BF16)
        kc_t = _group_norm_tiles(kc, bd64, 1.0 / HEAD_DIM)
        for j in range(DIL_WIDTH // LANES):
            sl = slice(j * LANES, (j + 1) * LANES)
            kc_ref[rs, sl] = _rope_tile(kc_t[j] * gkc_ref[:, sl], half_c, cd, sdd, sud)
        vc_ref[rs, :] = vc
        z = misc + bmisc_ref[...]
        logf = jnp.minimum(z, 0.0) - jnp.log(1.0 + jnp.exp(-jnp.abs(z)))
        lane = lax.broadcasted_iota(jnp.int32, logf.shape, 1)
        logf = jnp.where(lane < FOX_HEADS, logf, 0.0)
        p0 = logf.astype(BF16)
        r0 = logf - p0.astype(F32)
        p1 = r0.astype(BF16)
        p2 = (r0 - p1.astype(F32)).astype(BF16)
        yield
        csum = _dot(tri, p0) + _dot(tri, p1) + _dot(tri, p2)
        kb_pre = kb_nope + _dot(misc.astype(BF16), rep_ref[...])
        yield
        kb_t = _group_norm_tiles(kb_pre, ones, 1.0 / MLA_QK_DIM)
        for j in range(MLA_HEADS):
            sl = slice(j * LANES, (j + 1) * LANES)
            kb_ref[rs, sl] = _rope_tile(kb_t[j] * gkb_ref[:, sl], half_b, cm, sdm, sum_).astype(BF16)
        csum = csum + carry["c"]
        c_ref[rs, :] = csum
        carry["c"] = csum[hm - 1:, :]

    _interleave([rows(g) for g in range(PROJ_SPLIT)])
    carry_ref[...] = carry["c"]


def _inproj(x2d, prm, tabs, seq, tm):
    T, D = x2d.shape
    nm = prm["w_main"].shape[1]
    hm = tm // PROJ_SPLIT
    row = lambda w: pl.BlockSpec((tm, w), lambda i: (i, 0))
    in_specs = [row(D), _full((1, D)), _full((D, nm)), _full((1, LANES)), _full((hm, hm)),
                _full((LANES, LANES)), _full((LANES, LANES)),
                _full((1, FOX_WIDTH)), _full((1, FOX_WIDTH)), _full((1, MLA_Q_RANK)), _full((1, MLA_KV_RANK)),
                _full((MLA_Q_RANK, MLA_HEADS * MLA_PAD)), _full((MLA_KV_RANK, MLA_HEADS * MLA_PAD)),
                _full((MLA_KV_RANK, MLA_WIDTH)), _full((LANES, MLA_HEADS * MLA_PAD)),
                _full((1, MLA_HEADS * MLA_PAD)), _full((1, MLA_HEADS * MLA_PAD)),
                _full((1, DIL_WIDTH)), _full((1, DIL_WIDTH))] + [row(LANES)] * 6
    outs = [(FOX_WIDTH, BF16), (FOX_WIDTH, BF16), (FOX_WIDTH, BF16), (LANES, F32),
            (MLA_HEADS * MLA_PAD, BF16), (MLA_HEADS * MLA_PAD, BF16), (MLA_WIDTH, BF16),
            (DIL_WIDTH, F32), (DIL_WIDTH, F32), (DIL_WIDTH, F32)]
    return pl.pallas_call(
        functools.partial(_inproj_kernel, tiles_per_batch=seq // tm),
        out_shape=[jax.ShapeDtypeStruct((T, w), dt) for w, dt in outs],
        grid=(T // tm,),
        in_specs=in_specs,
        out_specs=[row(w) for w, _ in outs],
        scratch_shapes=[pltpu.VMEM((1, LANES), F32)],
        compiler_params=_cparams(("arbitrary",)),
        name="inproj",
    )(x2d, prm["g_mix"], prm["w_main"], prm["b_misc"], tabs["tri_incl"], tabs["bd64"], tabs["ones"],
      prm["g_qa"], prm["g_ka"], prm["g_cq"], prm["g_ckv"], prm["w_uq"], prm["w_uk"], prm["w_uv"], tabs["rep"],
      prm["g_qb"], prm["g_kb"], prm["g_qc"], prm["g_kc"],
      tabs["cm"], tabs["sdm"], tabs["sum"], tabs["cd"], tabs["sdd"], tabs["sud"])


def _causal_kernel(*refs, seq, tq, split_heads, with_bias):
    if with_bias:
        q_ref, k_ref, v_ref, b_ref, o_ref = refs
    else:
        q_ref, k_ref, v_ref, o_ref = refs
        b_ref = None
    lane = lax.broadcasted_iota(jnp.int32, (tq, LANES), 1)
    lo_half = lane < HEAD_DIM
    r = lax.broadcasted_iota(jnp.int32, (tq, tq), 0)
    c = lax.broadcasted_iota(jnp.int32, (tq, tq), 1)
    keep = c <= r
    for qi in range(seq // tq):
        q = q_ref[0, qi * tq:(qi + 1) * tq, :]
        n_keys = (qi + 1) * tq
        v = v_ref[0, :n_keys, :]
        res = []
        for hh in range(2):
            if split_heads:
                qh = q[:, hh * LANES:(hh + 1) * LANES]
                k = k_ref[0, :n_keys, hh * LANES:(hh + 1) * LANES]
            else:
                zero = jnp.zeros_like(q)
                qh = jnp.where(lo_half, q, zero) if hh == 0 else jnp.where(lo_half, zero, q)
                k = k_ref[0, :n_keys, :]
            s = _dot_nt(qh, k)
            if with_bias:
                s = s + b_ref[0, 0, hh:hh + 1, :n_keys]
            s_diag = jnp.where(keep, s[:, n_keys - tq:], NEG_BIG)
            s = jnp.concatenate([s[:, :n_keys - tq], s_diag], axis=1) if qi else s_diag
            p = jnp.exp2(s - jnp.max(s, axis=-1, keepdims=True))
            l = jnp.sum(p, axis=-1, keepdims=True)
            res.append(_dot(p.astype(BF16), v) / l)
        o_ref[0, qi * tq:(qi + 1) * tq, :] = jnp.where(lo_half, res[0], res[1]).astype(o_ref.dtype)


def _causal_attention(q, k, v, kbias, *, tq, split_heads):
    B, S, _ = q.shape
    P = v.shape[2] // LANES
    wq = 2 * LANES if split_heads else LANES
    in_specs = [pl.BlockSpec((1, S, wq), lambda b, p: (b, 0, p)),
                pl.BlockSpec((1, S, wq), lambda b, p: (b, 0, p)),
                pl.BlockSpec((1, S, LANES), lambda b, p: (b, 0, p))]
    args = [q, k, v]
    if kbias is not None:
        in_specs.append(pl.BlockSpec((1, 1, 2, S), lambda b, p: (b, p, 0, 0)))
        args.append(kbias)
    return pl.pallas_call(
        functools.partial(_causal_kernel, seq=S, tq=tq, split_heads=split_heads, with_bias=kbias is not None),
        out_shape=jax.ShapeDtypeStruct((B, S, P * LANES), BF16),
        grid=(B, P),
        in_specs=in_specs,
        out_specs=pl.BlockSpec((1, S, LANES), lambda b, p: (b, 0, p)),
        compiler_params=_cparams(("parallel", "parallel")),
        name="causal_split" if split_heads else "causal_shared",
    )(*args)


def _dilated_kernel(q_ref, k_ref, v_ref, o_ref, acc_ref, m_ref, l_ref, *, seq, n, dilations):
    lane = lax.broadcasted_iota(jnp.int32, (1, 1, LANES), 2)
    lo_half = lane < HEAD_DIM
    row = lax.broadcasted_iota(jnp.int32, (n, n), 0)
    col = lax.broadcasted_iota(jnp.int32, (n, n), 1)
    own_keep = (col <= row)[None]
    prev_keep = (col >= row)[None]

    nblk = seq // n
    for pi, d in enumerate(dilations):
        L = seq // d
        nb = L // n

        def load(ref, d=d, L=L, nb=nb):
            if d == 1:
                return ref[0].reshape(nblk, n, LANES).astype(BF16)
            parts = [ref[0, pl.ds(r, L, stride=d), :].reshape(nb, n, LANES) for r in range(d)]
            return jnp.concatenate(parts, axis=0).astype(BF16)

        qb, kb, vb = load(q_ref), load(k_ref), load(v_ref)
        if nb > 1:
            zero = jnp.zeros((1, n, LANES), BF16)
            kp = jnp.concatenate([zero, kb[:nblk - 1]], axis=0)
            vp = jnp.concatenate([zero, vb[:nblk - 1]], axis=0)
            blk = lax.broadcasted_iota(jnp.int32, (nblk, n, n), 0)
            prev_ok = jnp.logical_and(prev_keep, lax.rem(blk, nb) != 0)
        zq = jnp.zeros_like(qb)
        heads = []
        for hh in range(2):
            qh = jnp.where(lo_half, qb, zq) if hh == 0 else jnp.where(lo_half, zq, qb)
            s_own = jnp.einsum("bqd,bkd->bqk", qh, kb, preferred_element_type=F32)
            s_own = jnp.where(own_keep, s_own, NEG_BIG)
            m = jnp.max(s_own, axis=-1, keepdims=True)
            if nb > 1:
                s_prev = jnp.einsum("bqd,bkd->bqk", qh, kp, preferred_element_type=F32)
                s_prev = jnp.where(prev_ok, s_prev, NEG_BIG)
                m = jnp.maximum(m, jnp.max(s_prev, axis=-1, keepdims=True))
            p_own = jnp.exp2(s_own - m)
            l = jnp.sum(p_own, axis=-1, keepdims=True)
            pv = jnp.einsum("bqk,bkd->bqd", p_own.astype(BF16), vb, preferred_element_type=F32)
            if nb > 1:
                p_prev = jnp.exp2(s_prev - m)
                l = l + jnp.sum(p_prev, axis=-1, keepdims=True)
                pv = pv + jnp.einsum("bqk,bkd->bqd", p_prev.astype(BF16), vp, preferred_element_type=F32)
            heads.append((m, l, pv))
        m_f = jnp.where(lo_half, heads[0][0], heads[1][0])
        l_f = jnp.where(lo_half, heads[0][1], heads[1][1])
        a_f = jnp.where(lo_half, heads[0][2], heads[1][2])
        for dst, val in ((acc_ref, a_f), (m_ref, m_f), (l_ref, l_f)):
            if d == 1:
                dst[pi] = val.reshape(seq, LANES)
            else:
                for r in range(d):
                    dst[pi, pl.ds(r, L, stride=d), :] = val[r * nb:(r + 1) * nb].reshape(L, LANES)

    np_ = len(dilations)
    m_all = m_ref[0]
    for pi in range(1, np_):
        m_all = jnp.maximum(m_all, m_ref[pi])
    num = jnp.zeros((seq, LANES), F32)
    den = jnp.zeros((seq, LANES), F32)
    for pi in range(np_):
        w = jnp.exp2(m_ref[pi] - m_all)
        num = num + w * acc_ref[pi]
        den = den + w * l_ref[pi]
    o_ref[0] = num / den


def _dilated_attention(qc, kc, vc, seq):
    B = qc.shape[0]
    n = DILATED_PATTERNS[0][0] // DILATED_PATTERNS[0][1]
    dil = tuple(d for _, d in DILATED_PATTERNS)
    for w, d in DILATED_PATTERNS:
        assert w // d == n and seq % (d * n) == 0
    spec = pl.BlockSpec((1, seq, LANES), lambda b, p: (b, 0, p))
    np_ = len(dil)
    return pl.pallas_call(
        functools.partial(_dilated_kernel, seq=seq, n=n, dilations=dil),
        out_shape=jax.ShapeDtypeStruct((B, seq, DIL_WIDTH), F32),
        grid=(B, DIL_WIDTH // LANES),
        in_specs=[spec, spec, spec],
        out_specs=spec,
        scratch_shapes=[pltpu.VMEM((np_, seq, LANES), F32)] * 3,
        compiler_params=_cparams(("parallel", "parallel")),
        name="dilated",
    )(qc, kc, vc)


def _memkv_kernel(mem_ref, g_ref, w_ref, gk_ref, bd64_ref, k_ref, v_ref):
    mn = _rms_rows(mem_ref[0], g_ref[...]).astype(BF16)
    kv = _dot(mn, w_ref[...])
    k = jnp.concatenate(_group_norm_tiles(kv[:, :MEM_WIDTH], bd64_ref[...], 1.0 / HEAD_DIM), axis=1)
    k_ref[0] = (k * gk_ref[...]).astype(BF16)
    v_ref[0] = kv[:, MEM_WIDTH:].astype(BF16)


def _memkv(mem, prm, tabs):
    B, M, D = mem.shape
    return pl.pallas_call(
        _memkv_kernel,
        out_shape=[jax.ShapeDtypeStruct((B, M, MEM_WIDTH), BF16)] * 2,
        grid=(B,),
        in_specs=[pl.BlockSpec((1, M, D), lambda b: (b, 0, 0)), _full((1, D)), _full((D, 2 * MEM_WIDTH)),
                  _full((1, MEM_WIDTH)), _full((LANES, LANES))],
        out_specs=[pl.BlockSpec((1, M, MEM_WIDTH), lambda b: (b, 0, 0))] * 2,
        compiler_params=_cparams(("parallel",)),
        name="memkv",
    )(mem, prm["g_mem"], prm["w_mkv"], prm["g_km"], tabs["bd64"])


def _postattn_kernel(x_ref, oa_ref, ob_ref, oc_ref, gg_ref, wout_ref, gx_ref, wmq_ref, gqm_ref, bd64_ref,
                     km_ref, vm_ref, wmo_ref, gf_ref, wrh_ref, wrl_ref, br_ref, tri_ref,
                     x2_ref, ti_ref, tw_ref, rk_ref, cnt_ref, carry_ref, *, n_experts):
    i = pl.program_id(0)
    tm = x_ref.shape[0]
    hm = tm // PROJ_SPLIT
    a0, a1 = FOX_WIDTH, FOX_WIDTH + MLA_WIDTH
    lane = lax.broadcasted_iota(jnp.int32, (hm, LANES), 1)
    lane_f = lane.astype(F32)
    lo_half = lane < HEAD_DIM
    tri = tri_ref[...]

    @pl.when(i == 0)
    def _():
        carry_ref[...] = jnp.zeros_like(carry_ref)

    carry = {"n": carry_ref[...]}

    def rows(g):
        rs = slice(g * hm, (g + 1) * hm)
        ma = _rms_rows(oa_ref[rs, :].astype(F32), gg_ref[:, :a0]).astype(BF16)
        mb = _rms_rows(ob_ref[rs, :].astype(F32), gg_ref[:, a0:a1]).astype(BF16)
        mc = _rms_rows(oc_ref[rs, :], gg_ref[:, a1:]).astype(BF16)
        yield
        x1 = (x_ref[rs, :] + _dot(ma, wout_ref[:a0, :]) + _dot(mb, wout_ref[a0:a1, :])
              + _dot(mc, wout_ref[a1:, :]))
        yield
        h2 = _rms_rows(x1, gx_ref[...]).astype(BF16)
        yield
        qm = _dot(h2, wmq_ref[...])
        yield
        qm_t = _group_norm_tiles(qm, bd64_ref[...], 1.0 / HEAD_DIM)
        outs = []
        for p in range(MEM_WIDTH // LANES):
            sl = slice(p * LANES, (p + 1) * LANES)
            q2 = (qm_t[p] * gqm_ref[:, sl]).astype(BF16)
            k2 = km_ref[0, :, sl]
            v2 = vm_ref[0, :, sl]
            zero = jnp.zeros_like(q2)
            res = []
            for hh in range(2):
                qh = jnp.where(lo_half, q2, zero) if hh == 0 else jnp.where(lo_half, zero, q2)
                s = _dot_nt(qh, k2)
                pr = jnp.exp2(s - jnp.max(s, axis=-1, keepdims=True))
                l = jnp.sum(pr, axis=-1, keepdims=True)
                res.append(_dot(pr.astype(BF16), v2) / l)
            outs.append(jnp.where(lo_half, res[0], res[1]))
            yield
        om = jnp.concatenate(outs, axis=1).astype(BF16)
        x2 = x1 + _dot(om, wmo_ref[...])
        x2_ref[rs, :] = x2
        yield
        h3 = _rms_rows(x2, gf_ref[...])
        h3_hi = h3.astype(BF16)
        h3_lo = (h3 - h3_hi.astype(F32)).astype(BF16)
        yield
        logits = _dot(h3_hi, wrh_ref[...]) + _dot(h3_hi, wrl_ref[...]) + _dot(h3_lo, wrh_ref[...]) + br_ref[...]
        yield
        logits = jnp.where(lane < n_experts, logits, NEG_BIG)
        work = logits
        sel = jnp.zeros((hm, LANES), F32)
        vals, idxs = [], []
        for _ in range(TOP_K):
            mv = jnp.max(work, axis=-1, keepdims=True)
            ix = jnp.min(jnp.where(work == mv, lane_f, float(LANES)), axis=-1, keepdims=True)
            hit = lane_f == ix
            work = jnp.where(hit, NEG_BIG, work)
            sel = jnp.where(hit, 1.0, sel)
            vals.append(mv)
            idxs.append(ix)
        es = [jnp.exp(v - vals[0]) for v in vals]
        den = es[0] + es[1] + es[2] + es[3]
        yield
        before = _dot(tri, sel.astype(BF16))
        yield
        before = before + carry["n"]
        carry["n"] = before[hm - 1:, :] + sel[hm - 1:, :]
        ti = jnp.zeros((hm, LANES), F32)
        tw = jnp.zeros((hm, LANES), F32)
        rk = jnp.zeros((hm, LANES), F32)
        for kk in range(TOP_K):
            at = lane == kk
            ti = jnp.where(at, idxs[kk], ti)
            tw = jnp.where(at, es[kk] / den, tw)
            r_k = jnp.sum(jnp.where(lane_f == idxs[kk], before, 0.0), axis=-1, keepdims=True)
            rk = jnp.where(at, r_k, rk)
        ti_ref[rs, :] = ti.astype(jnp.int32)
        tw_ref[rs, :] = tw
        rk_ref[rs, :] = rk.astype(jnp.int32)

    _interleave([rows(g) for g in range(PROJ_SPLIT)])
    carry_ref[...] = carry["n"]
    cnt_ref[...] = carry["n"]


def _postattn(x2d, oa, ob, oc, kmem, vmem, prm, tabs, seq, tm, n_experts):
    T, D = x2d.shape
    M = kmem.shape[1]
    tpb = seq // tm
    hm = tm // PROJ_SPLIT
    row = lambda w: pl.BlockSpec((tm, w), lambda i: (i, 0))
    memspec = pl.BlockSpec((1, M, MEM_WIDTH), lambda i: (i // tpb, 0, 0))
    mix = FOX_WIDTH + MLA_WIDTH + DIL_WIDTH
    return pl.pallas_call(
        functools.partial(_postattn_kernel, n_experts=n_experts),
        out_shape=[jax.ShapeDtypeStruct((T, D), F32),
                   jax.ShapeDtypeStruct((T, LANES), jnp.int32), jax.ShapeDtypeStruct((T, LANES), F32),
                   jax.ShapeDtypeStruct((T, LANES), jnp.int32), jax.ShapeDtypeStruct((1, LANES), F32)],
        grid=(T // tm,),
        in_specs=[row(D), row(FOX_WIDTH), row(MLA_WIDTH), row(DIL_WIDTH), _full((1, mix)), _full((mix, D)),
                  _full((1, D)), _full((D, MEM_WIDTH)), _full((1, MEM_WIDTH)), _full((LANES, LANES)),
                  memspec, memspec, _full((MEM_WIDTH, D)), _full((1, D)),
                  _full((D, LANES)), _full((D, LANES)), _full((1, LANES)), _full((hm, hm))],
        out_specs=[row(D), row(LANES), row(LANES), row(LANES), _full((1, LANES))],
        scratch_shapes=[pltpu.VMEM((1, LANES), F32)],
        compiler_params=_cparams(("arbitrary",)),
        name="postattn",
    )(x2d, oa, ob, oc, prm["g_group"], prm["w_out"], prm["g_xmem"], prm["w_mq"], prm["g_qm"], tabs["bd64"],
      kmem, vmem, prm["w_mo"], prm["g_ffn"], prm["w_r_hi"], prm["w_r_lo"], prm["b_router"], tabs["tri_strict"])


def _row_copy(src_ref, src_row, dst_ref, dst_row, sem):
    return pltpu.make_async_copy(src_ref.at[pl.ds(src_row, 1), :], dst_ref.at[pl.ds(dst_row, 1), :], sem)


def _dispatch_kernel(fill_ref, pos_ref, x_ref, xs_ref, zero_ref, sem, zsem, *, tm, tr):
    @pl.when(pl.program_id(0) == 0)
    def _():
        n_experts = fill_ref.shape[0] - 1
        n_rows = xs_ref.shape[0]
        zero_ref[...] = jnp.zeros_like(zero_ref)
        starts = [(fill_ref[e], fill_ref[e] >= 0) for e in range(n_experts)]
        for j in range(n_experts + 1):
            start = fill_ref[n_experts] + j * tr
            starts.append((start, start < n_rows))

        def fill(start):
            return pltpu.make_async_copy(zero_ref, xs_ref.at[pl.ds(pl.multiple_of(start, tr), tr), :], zsem)

        for start, live in starts:
            @pl.when(live)
            def _(start=start):
                fill(start).start()
        for start, live in starts:
            @pl.when(live)
            def _(start=start):
                fill(start).wait()

    def issue(r, _):
        for kk in range(TOP_K):
            _row_copy(x_ref, r, xs_ref, pos_ref[0, 0, r * TOP_K + kk], sem).start(priority=kk % 2)
        return 0
    lax.fori_loop(0, tm, issue, 0, unroll=ROW_DMA_UNROLL)
    n = tm * TOP_K
    pltpu.make_async_copy(xs_ref.at[pl.ds(0, n), :], xs_ref.at[pl.ds(0, n), :], sem).wait()


def _dispatch(x2, pos, pad_start, n_rows, tm, tr):
    T, D = x2.shape
    return pl.pallas_call(
        functools.partial(_dispatch_kernel, tm=tm, tr=tr),
        out_shape=jax.ShapeDtypeStruct((n_rows, D), x2.dtype),
        grid_spec=pltpu.PrefetchScalarGridSpec(
            num_scalar_prefetch=1,
            grid=(T // tm,),
            in_specs=[pl.BlockSpec((1, 1, tm * TOP_K), lambda i, pad: (i, 0, 0), memory_space=pltpu.SMEM),
                      pl.BlockSpec((tm, D), lambda i, pad: (i, 0))],
            out_specs=pl.BlockSpec(memory_space=pl.ANY),
            scratch_shapes=[pltpu.VMEM((tr, D), x2.dtype), pltpu.SemaphoreType.DMA, pltpu.SemaphoreType.DMA],
        ),
        compiler_params=_cparams(("arbitrary",)),
        name="dispatch",
    )(pad_start, pos.reshape(T // tm, 1, tm * TOP_K), x2)


def _experts_kernel(te_ref, nu_ref, xs_ref, gf_ref, w1_ref, b1g_ref, b1l_ref, w2_ref, b2_ref, perm_ref,
                    ys_ref, w1s_ref, w2s_ref):
    i = pl.program_id(0)
    two_f = w1_ref.shape[3]
    n_blk = two_f // DEINT_BLOCK
    half = DEINT_BLOCK // 2

    @pl.when(i < nu_ref[0])
    def _():
        @pl.when(jnp.logical_or(i == 0, te_ref[i] != te_ref[jnp.maximum(i - 1, 0)]))
        def _():
            perm = perm_ref[...]
            for b in range(n_blk):
                sl = slice(b * DEINT_BLOCK, (b + 1) * DEINT_BLOCK)
                w1s_ref[:, sl] = _dot(w1_ref[0, 0, :, sl].astype(BF16), perm).astype(BF16)
            w2s_ref[...] = w2_ref[0, 0].astype(BF16)

        x = _rms_rows(xs_ref[...], gf_ref[...]).astype(BF16)
        h = _dot(x, w1s_ref[...])
        acts = []
        for b in range(n_blk):
            glu = h[:, b * DEINT_BLOCK:b * DEINT_BLOCK + half] + b1g_ref[0, :, b * half:(b + 1) * half]
            lin = h[:, b * DEINT_BLOCK + half:(b + 1) * DEINT_BLOCK] + b1l_ref[0, :, b * half:(b + 1) * half]
            glu = jnp.minimum(glu, SWIGLU_LIMIT)
            lin = jnp.clip(lin, -SWIGLU_LIMIT, SWIGLU_LIMIT)
            acts.append((glu * (1.0 / (1.0 + jnp.exp(-SWIGLU_ALPHA * glu))) * (lin + 1.0)).astype(BF16))
        act = jnp.concatenate(acts, axis=1) if n_blk > 1 else acts[0]
        ys_ref[...] = _dot(act, w2s_ref[...]) + b2_ref[0]

    @pl.when(i >= nu_ref[0])
    def _():
        ys_ref[...] = jnp.zeros_like(ys_ref)


def _experts(tile_expert, n_used, xs, w_e1, w_e2, layer, prm, tabs, tr):
    n_rows, D = xs.shape
    two_f = w_e1.shape[3]
    F = two_f // 2
    assert two_f % DEINT_BLOCK == 0
    n_tiles = n_rows // tr
    rows = pl.BlockSpec((tr, D), lambda i, te, nu: (i, 0))
    wspec = lambda s: pl.BlockSpec((1,) + s, lambda i, te, nu: (te[i], 0, 0))
    lspec = lambda s: pl.BlockSpec((1, 1) + s, lambda i, te, nu: (layer, te[i], 0, 0))
    return pl.pallas_call(
        _experts_kernel,
        out_shape=jax.ShapeDtypeStruct((n_rows, D), F32),
        grid_spec=pltpu.PrefetchScalarGridSpec(
            num_scalar_prefetch=2,
            grid=(n_tiles,),
            in_specs=[rows, pl.BlockSpec((1, D), lambda i, te, nu: (0, 0)),
                      lspec((D, two_f)), wspec((1, F)), wspec((1, F)), lspec((F, D)), wspec((1, D)),
                      pl.BlockSpec((DEINT_BLOCK, DEINT_BLOCK), lambda i, te, nu: (0, 0))],
            out_specs=rows,
            scratch_shapes=[pltpu.VMEM((D, two_f), BF16), pltpu.VMEM((F, D), BF16)],
        ),
        compiler_params=_cparams(("arbitrary",)),
        name="experts",
    )(tile_expert, n_used, xs, prm["g_ffn"], w_e1, prm["b1g"], prm["b1l"], w_e2, prm["b2"], tabs["deint"])


def _combine_kernel(pos_ref, x_ref, tw_ref, ys_ref, o_ref, buf, sem, *, tm):
    def issue(r, _):
        for kk in range(TOP_K):
            _row_copy(ys_ref, pos_ref[0, 0, r * TOP_K + kk], buf.at[kk], r, sem).start(priority=kk % 2)
        return 0
    lax.fori_loop(0, tm, issue, 0, unroll=ROW_DMA_UNROLL)
    pltpu.make_async_copy(buf, buf, sem).wait()

    tw = tw_ref[...]
    acc = x_ref[...]
    for kk in range(TOP_K):
        acc = acc + tw[:, kk:kk + 1] * buf[kk]
    o_ref[...] = acc


def _combine(x2, tw, ys, pos, tm):
    T, D = x2.shape
    W = ys.shape[1]
    return pl.pallas_call(
        functools.partial(_combine_kernel, tm=tm),
        out_shape=jax.ShapeDtypeStruct((T, D), F32),
        grid=(T // tm,),
        in_specs=[pl.BlockSpec((1, 1, tm * TOP_K), lambda i: (i, 0, 0), memory_space=pltpu.SMEM),
                  pl.BlockSpec((tm, D), lambda i: (i, 0)),
                  pl.BlockSpec((tm, LANES), lambda i: (i, 0)),
                  pl.BlockSpec(memory_space=pl.ANY)],
        out_specs=pl.BlockSpec((tm, D), lambda i: (i, 0)),
        scratch_shapes=[pltpu.VMEM((TOP_K, tm, W), ys.dtype), pltpu.SemaphoreType.DMA],
        compiler_params=_cparams(("arbitrary",)),
        name="combine",
    )(pos.reshape(T // tm, 1, tm * TOP_K), x2, tw, ys)


def _tile_gain(g, reps, scale=1.0):
    return (jnp.tile(g.astype(F32), reps) * scale)[None, :]


def _prep_layer(l, D, w_in, b_forget, g_mix, g_qa, g_ka, g_cq, g_ckv, w_uq, w_ukv, g_qb, g_kb, g_qc, g_kc,
                g_group, w_out, g_xmem, g_mem, w_mq, w_mkv, g_qm, g_km, w_mo, g_ffn, w_router, b_router,
                w_e1, b_e1, w_e2, b_e2):
    wi = w_in[l]
    sizes = (FOX_WIDTH, FOX_WIDTH, FOX_WIDTH, FOX_HEADS, MLA_Q_RANK, MLA_KV_RANK, MLA_ROPE_DIM,
             DIL_WIDTH, DIL_WIDTH, DIL_WIDTH)
    pts = np.cumsum((0,) + sizes)
    qa, ka, va, fa, cq, ckv, kr, qc, kc, vc = [wi[:, pts[j]:pts[j + 1]] for j in range(len(sizes))]
    misc = jnp.zeros((D, LANES), F32).at[:, :FOX_HEADS].set(fa).at[:, 32:32 + MLA_ROPE_DIM].set(kr)
    w_main = jnp.concatenate([qa, ka, va, cq, ckv, qc, kc, vc, misc], axis=1).astype(BF16)
    b_misc = jnp.zeros((1, LANES), F32).at[0, :FOX_HEADS].set(b_forget[l])

    uq = w_uq[l].reshape(MLA_Q_RANK, MLA_HEADS, MLA_QK_DIM)
    uq = jnp.pad(uq, ((0, 0), (0, 0), (0, MLA_PAD - MLA_QK_DIM))).reshape(MLA_Q_RANK, MLA_HEADS * MLA_PAD)
    ukv = w_ukv[l].reshape(MLA_KV_RANK, MLA_HEADS, MLA_NOPE_DIM + MLA_V_DIM)
    uk = jnp.pad(ukv[:, :, :MLA_NOPE_DIM], ((0, 0), (0, 0), (0, MLA_PAD - MLA_NOPE_DIM)))
    uk = uk.reshape(MLA_KV_RANK, MLA_HEADS * MLA_PAD)
    uv = ukv[:, :, MLA_NOPE_DIM:].reshape(MLA_KV_RANK, MLA_WIDTH)
    pad_gain = lambda g: jnp.pad(g.astype(F32), (0, MLA_PAD - MLA_QK_DIM))

    E = w_e1.shape[1]
    wr = jnp.zeros((D, LANES), F32).at[:, :E].set(w_router[l])
    wr_hi = wr.astype(BF16)
    wr_lo = (wr - wr_hi.astype(F32)).astype(BF16)
    q_scale = lambda dim: dim ** -0.5 * LOG2E
    return dict(
        g_mix=g_mix[l][None], w_main=w_main, b_misc=b_misc,
        g_qa=_tile_gain(g_qa[l], FOX_HEADS, q_scale(HEAD_DIM)), g_ka=_tile_gain(g_ka[l], FOX_HEADS),
        g_cq=g_cq[l][None], g_ckv=g_ckv[l][None],
        w_uq=uq.astype(BF16), w_uk=uk.astype(BF16), w_uv=uv.astype(BF16),
        g_qb=_tile_gain(pad_gain(g_qb[l]), MLA_HEADS, q_scale(MLA_QK_DIM)),
        g_kb=_tile_gain(pad_gain(g_kb[l]), MLA_HEADS),
        g_qc=_tile_gain(g_qc[l], DIL_HEADS, q_scale(HEAD_DIM)), g_kc=_tile_gain(g_kc[l], DIL_HEADS),
        g_group=g_group[l][None], w_out=w_out[l].astype(BF16),
        g_xmem=g_xmem[l][None], g_mem=g_mem[l][None],
        w_mq=w_mq[l].astype(BF16), w_mkv=w_mkv[l].astype(BF16),
        g_qm=_tile_gain(g_qm[l], MEM_HEADS, q_scale(HEAD_DIM)), g_km=_tile_gain(g_km[l], MEM_HEADS),
        w_mo=w_mo[l].astype(BF16), g_ffn=g_ffn[l][None],
        w_r_hi=wr_hi, w_r_lo=wr_lo,
        b_router=jnp.zeros((1, LANES), F32).at[0, :E].set(b_router[l]),
        b1g=b_e1[l][:, None, 0::2], b1l=b_e1[l][:, None, 1::2], b2=b_e2[l][:, None, :],
    )


def _rope_lane_tables(positions, rot_dim, period, offset):
    half = rot_dim // 2
    inv = ROPE_THETA ** (-jnp.arange(0, rot_dim, 2, dtype=F32) / rot_dim)
    ang = positions.astype(F32).reshape(-1)[:, None] * inv
    cos, sin = jnp.cos(ang), jnp.sin(ang)
    lane = np.arange(LANES) % period - offset
    idx = np.clip(lane % half if half else lane, 0, half - 1)
    first = (lane >= 0) & (lane < half)
    second = (lane >= half) & (lane < rot_dim)
    c = jnp.where(first | second, cos[:, idx], 1.0)
    s_dn = jnp.where(second, sin[:, idx], 0.0)
    s_up = jnp.where(first, -sin[:, idx], 0.0)
    return c, s_dn, s_up


def _tables(positions, tm):
    r = np.arange(tm // PROJ_SPLIT)
    bd = (np.arange(LANES)[:, None] // HEAD_DIM) == (np.arange(LANES)[None, :] // HEAD_DIM)
    rep = np.zeros((LANES, MLA_HEADS * MLA_PAD), np.float32)
    for h in range(MLA_HEADS):
        for j in range(MLA_ROPE_DIM):
            rep[32 + j, h * MLA_PAD + MLA_NOPE_DIM + j] = 1.0
    half = DEINT_BLOCK // 2
    deint = np.zeros((DEINT_BLOCK, DEINT_BLOCK), np.float32)
    deint[2 * np.arange(half), np.arange(half)] = 1.0
    deint[2 * np.arange(half) + 1, half + np.arange(half)] = 1.0
    cm, sdm, sum_ = _rope_lane_tables(positions, MLA_ROPE_DIM, LANES, MLA_NOPE_DIM)
    cd, sdd, sud = _rope_lane_tables(positions, ROT_DIM, HEAD_DIM, 0)
    return dict(
        tri_incl=jnp.asarray(r[:, None] >= r[None, :], BF16),
        tri_strict=jnp.asarray(r[:, None] > r[None, :], BF16),
        bd64=jnp.asarray(bd, BF16), ones=jnp.ones((LANES, LANES), BF16), rep=jnp.asarray(rep, BF16),
        deint=jnp.asarray(deint, BF16),
        cm=cm, sdm=sdm, sum=sum_, cd=cd, sdd=sdd, sud=sud,
    )


def _routing_tables(ti, rk, cnt, n_experts, tr, n_tiles):
    counts = cnt[0, :n_experts].astype(jnp.int32)
    padded = ((counts + tr - 1) // tr) * tr
    ends = jnp.cumsum(padded)
    starts = ends - padded
    idx = ti[:, :TOP_K]
    hit = idx[:, :, None] == jnp.arange(n_experts, dtype=jnp.int32)
    pos = rk[:, :TOP_K] + jnp.sum(jnp.where(hit, starts, 0), axis=-1)
    tile_start = jnp.arange(n_tiles, dtype=jnp.int32) * tr
    tile_expert = jnp.sum((ends[None, :] <= tile_start[:, None]).astype(jnp.int32), axis=1)
    tile_expert = jnp.minimum(tile_expert, n_experts - 1)
    n_used = (ends[n_experts - 1:] // tr).astype(jnp.int32)
    pad_fill = jnp.where(padded > 0, ends - tr, -1)
    fill_start = jnp.concatenate([pad_fill, ends[n_experts - 1:]])
    return pos.astype(jnp.int32), tile_expert.astype(jnp.int32), fill_start.astype(jnp.int32), n_used


def _pick_tile(seq, want):
    t = min(want, seq)
    while seq % t:
        t //= 2
    return t


def kernel(x, mem, positions, g_mix, w_in, b_forget, g_qa, g_ka, g_cq, g_ckv, w_uq, w_ukv, g_qb, g_kb, g_qc, g_kc, g_group, w_out, g_xmem, g_mem, w_mq, w_mkv, g_qm, g_km, w_mo, g_ffn, w_router, b_router, w_e1, b_e1, w_e2, b_e2):
    B, S, D = x.shape
    T = B * S
    depth = w_in.shape[0]
    E = w_e1.shape[1]
    tm = _pick_tile(S, PROJ_TILE)
    tq = _pick_tile(S, ATTN_TILE)
    tr = EXPERT_ROW_TILE
    tmoe = _pick_tile(S, MOE_TOKEN_TILE)
    n_rows = T * TOP_K + (E + 1) * tr
    n_tiles = n_rows // tr

    tabs = _tables(positions, tm)
    x2d = x.reshape(T, D)
    for l in range(depth):
        prm = _prep_layer(l, D, w_in, b_forget, g_mix, g_qa, g_ka, g_cq, g_ckv, w_uq, w_ukv, g_qb, g_kb,
                          g_qc, g_kc, g_group, w_out, g_xmem, g_mem, w_mq, w_mkv, g_qm, g_km, w_mo, g_ffn,
                          w_router, b_router, w_e1, b_e1, w_e2, b_e2)
        qa, ka, va, csum, qb, kb, vb, qc, kc, vc = _inproj(x2d, prm, tabs, S, tm)
        kbias = -LOG2E * csum[:, :FOX_HEADS].reshape(B, S, FOX_HEADS // 2, 2).transpose(0, 2, 3, 1)
        r3 = lambda a: a.reshape(B, S, a.shape[-1])
        oa = _causal_attention(r3(qa), r3(ka), r3(va), kbias, tq=tq, split_heads=False)
        ob = _causal_attention(r3(qb), r3(kb), r3(vb), None, tq=tq, split_heads=True)
        oc = _dilated_attention(r3(qc), r3(kc), r3(vc), S)
        kmem, vmem = _memkv(mem, prm, tabs)
        x2, ti, tw, rk, cnt = _postattn(x2d, oa.reshape(T, -1), ob.reshape(T, -1), oc.reshape(T, -1),
                                        kmem, vmem, prm, tabs, S, tm, E)
        pos, tile_expert, pad_start, n_used = _routing_tables(ti, rk, cnt, E, tr, n_tiles)
        xs = _dispatch(x2, pos, pad_start, n_rows, tmoe, tr)
        ys = _experts(tile_expert, n_used, xs, w_e1, w_e2, l, prm, tabs, tr)
        x2d = _combine(x2, tw, ys, pos, tmoe)
    return x2d.reshape(B, S, D)
```

```python
import functools
import math

import jax
import jax.numpy as jnp
import numpy as np
from jax import lax
from jax.experimental import pallas as pl
from jax.experimental.pallas import tpu as pltpu

HEAD_DIM = 64
FOX_HEADS = 4
MLA_HEADS = 6
DIL_HEADS = 6
MLA_Q_RANK = 384
MLA_KV_RANK = 256
MLA_NOPE_DIM = 64
MLA_ROPE_DIM = 32
MLA_QK_DIM = MLA_NOPE_DIM + MLA_ROPE_DIM
MLA_V_DIM = 64
ROT_DIM = HEAD_DIM // 4
ROPE_THETA = 500000.0
DILATED_PATTERNS = ((128, 1), (512, 4), (2048, 16))
FOX_WIDTH = FOX_HEADS * HEAD_DIM
MLA_WIDTH = MLA_HEADS * MLA_V_DIM
DIL_WIDTH = DIL_HEADS * HEAD_DIM
MEM_HEADS = 4
MEM_WIDTH = MEM_HEADS * HEAD_DIM
TOP_K = 4
SWIGLU_ALPHA = 1.702
SWIGLU_LIMIT = 7.0
RMS_EPS = 1e-6
LOG2E = math.log2(math.e)

LANES = 128
VMEM_LIMIT_BYTES = 56 * 1024 * 1024

PROJ_TILE = 512
PROJ_SPLIT = 2
ATTN_TILE = 512
EXPERT_ROW_TILE = 512
MOE_TOKEN_TILE = 512
ROW_DMA_UNROLL = 8

NEG_BIG = -1e30
MLA_PAD = 128
DEINT_BLOCK = 256

F32 = jnp.float32
BF16 = jnp.bfloat16


def _cparams(sem):
    return pltpu.CompilerParams(dimension_semantics=sem, vmem_limit_bytes=VMEM_LIMIT_BYTES)


def _full(shape):
    nd = len(shape)
    return pl.BlockSpec(shape, lambda *_: (0,) * nd)


def _dot(a, b):
    return jnp.dot(a, b, preferred_element_type=F32)


def _dot_nt(a, b):
    return lax.dot_general(a, b, (((1,), (1,)), ((), ())), preferred_element_type=F32)


def _rms_rows(x, g):
    return x * lax.rsqrt(jnp.mean(x * x, axis=-1, keepdims=True) + RMS_EPS) * g


def _group_norm_tiles(y, ones_mat, inv_n):
    outs = []
    for j in range(y.shape[1] // LANES):
        yj = y[:, j * LANES:(j + 1) * LANES]
        ss = _dot((yj * yj).astype(BF16), ones_mat)
        outs.append(yj * lax.rsqrt(ss * inv_n + RMS_EPS))
    return outs


def _rope_tile(y, half, c, s_dn, s_up):
    return y * c + pltpu.roll(y, half, 1) * s_dn + pltpu.roll(y, LANES - half, 1) * s_up


def _interleave(gens):
    live = list(gens)
    while live:
        for g in list(live):
            try:
                next(g)
            except StopIteration:
                live.remove(g)


def _inproj_kernel(x_ref, gmix_ref, w_ref, bmisc_ref, tri_ref, bd64_ref, ones_ref,
                   gqa_ref, gka_ref, gcq_ref, gckv_ref, wuq_ref, wuk_ref, wuv_ref, rep_ref,
                   gqb_ref, gkb_ref, gqc_ref, gkc_ref,
                   cm_ref, sdm_ref, sum_ref, cd_ref, sdd_ref, sud_ref,
                   qa_ref, ka_ref, va_ref, c_ref, qb_ref, kb_ref, vb_ref, qc_ref, kc_ref, vc_ref,
                   carry_ref, *, tiles_per_batch):
    i = pl.program_id(0)
    tm = x_ref.shape[0]
    hm = tm // PROJ_SPLIT
    bd64 = bd64_ref[...]
    ones = ones_ref[...]
    tri = tri_ref[...]
    half_b = MLA_ROPE_DIM // 2
    half_c = ROT_DIM // 2
    offs = np.cumsum((0, FOX_WIDTH, FOX_WIDTH, FOX_WIDTH, MLA_Q_RANK, MLA_KV_RANK,
                      DIL_WIDTH, DIL_WIDTH, DIL_WIDTH, LANES))

    @pl.when(i % tiles_per_batch == 0)
    def _():
        carry_ref[...] = jnp.zeros_like(carry_ref)

    carry = {"c": carry_ref[...]}

    def rows(g):
        rs = slice(g * hm, (g + 1) * hm)
        h = _rms_rows(x_ref[rs, :], gmix_ref[...]).astype(BF16)
        seg = lambda j: _dot(h, w_ref[:, offs[j]:offs[j + 1]])
        yield
        qa, ka, va, cq = seg(0), seg(1), seg(2), seg(3)
        yield
        qa_ref[rs, :] = (jnp.concatenate(_group_norm_tiles(qa, bd64, 1.0 / HEAD_DIM), axis=1)
                         * gqa_ref[...]).astype(BF16)
        ka_ref[rs, :] = (jnp.concatenate(_group_norm_tiles(ka, bd64, 1.0 / HEAD_DIM), axis=1)
                         * gka_ref[...]).astype(BF16)
        va_ref[rs, :] = va.astype(BF16)
        cqn = _rms_rows(cq, gcq_ref[...]).astype(BF16)
        yield
        ckv, qc = seg(4), seg(5)
        qb_pre = _dot(cqn, wuq_ref[...])
        yield
        ckvn = _rms_rows(ckv, gckv_ref[...]).astype(BF16)
        cd, sdd, sud = cd_ref[rs, :], sdd_ref[rs, :], sud_ref[rs, :]
        qc_t = _group_norm_tiles(qc, bd64, 1.0 / HEAD_DIM)
        for j in range(DIL_WIDTH // LANES):
            sl = slice(j * LANES, (j + 1) * LANES)
            qc_ref[rs, sl] = _rope_tile(qc_t[j] * gqc_ref[:, sl], half_c, cd, sdd, sud)
        yield
        kc, vc, misc = seg(6), seg(7), seg(8)
        kb_nope = _dot(ckvn, wuk_ref[...])
        vb_ref[rs, :] = _dot(ckvn, wuv_ref[...]).astype(BF16)
        yield
        cm, sdm, sum_ = cm_ref[rs, :], sdm_ref[rs, :], sum_ref[rs, :]
        qb_t = _group_norm_tiles(qb_pre, ones, 1.0 / MLA_QK_DIM)
        for j in range(MLA_HEADS):
            sl = slice(j * LANES, (j + 1) * LANES)
            qb_ref[rs, sl] = _rope_tile(qb_t[j] * gqb_ref[:, sl], half_b, cm, sdm, sum_).astype(BF16)
        kc_t = _group_norm_tiles(kc, bd64, 1.0 / HEAD_DIM)
        for j in range(DIL_WIDTH // LANES):
            sl = slice(j * LANES, (j + 1) * LANES)
            kc_ref[rs, sl] = _rope_tile(kc_t[j] * gkc_ref[:, sl], half_c, cd, sdd, sud)
        vc_ref[rs, :] = vc
        z = misc + bmisc_ref[...]
        logf = jnp.minimum(z, 0.0) - jnp.log(1.0 + jnp.exp(-jnp.abs(z)))
        lane = lax.broadcasted_iota(jnp.int32, logf.shape, 1)
        logf = jnp.where(lane < FOX_HEADS, logf, 0.0)
        p0 = logf.astype(BF16)
        r0 = logf - p0.astype(F32)
        p1 = r0.astype(BF16)
        p2 = (r0 - p1.astype(F32)).astype(BF16)
        yield
        csum = _dot(tri, p0) + _dot(tri, p1) + _dot(tri, p2)
        kb_pre = kb_nope + _dot(misc.astype(BF16), rep_ref[...])
        yield
        kb_t = _group_norm_tiles(kb_pre, ones, 1.0 / MLA_QK_DIM)
        for j in range(MLA_HEADS):
            sl = slice(j * LANES, (j + 1) * LANES)
            kb_ref[rs, sl] = _rope_tile(kb_t[j] * gkb_ref[:, sl], half_b, cm, sdm, sum_).astype(BF16)
        csum = csum + carry["c"]
        c_ref[rs, :] = csum
        carry["c"] = csum[hm - 1:, :]

    _interleave([rows(g) for g in range(PROJ_SPLIT)])
    carry_ref[...] = carry["c"]


def _inproj(x2d, prm, tabs, seq, tm):
    T, D = x2d.shape
    nm = prm["w_main"].shape[1]
    hm = tm // PROJ_SPLIT
    row = lambda w: pl.BlockSpec((tm, w), lambda i: (i, 0))
    in_specs = [row(D), _full((1, D)), _full((D, nm)), _full((1, LANES)), _full((hm, hm)),
                _full((LANES, LANES)), _full((LANES, LANES)),
                _full((1, FOX_WIDTH)), _full((1, FOX_WIDTH)), _full((1, MLA_Q_RANK)), _full((1, MLA_KV_RANK)),
                _full((MLA_Q_RANK, MLA_HEADS * MLA_PAD)), _full((MLA_KV_RANK, MLA_HEADS * MLA_PAD)),
                _full((MLA_KV_RANK, MLA_WIDTH)), _full((LANES, MLA_HEADS * MLA_PAD)),
                _full((1, MLA_HEADS * MLA_PAD)), _full((1, MLA_HEADS * MLA_PAD)),
                _full((1, DIL_WIDTH)), _full((1, DIL_WIDTH))] + [row(LANES)] * 6
    outs = [(FOX_WIDTH, BF16), (FOX_WIDTH, BF16), (FOX_WIDTH, BF16), (LANES, F32),
            (MLA_HEADS * MLA_PAD, BF16), (MLA_HEADS * MLA_PAD, BF16), (MLA_WIDTH, BF16),
            (DIL_WIDTH, F32), (DIL_WIDTH, F32), (DIL_WIDTH, F32)]
    return pl.pallas_call(
        functools.partial(_inproj_kernel, tiles_per_batch=seq // tm),
        out_shape=[jax.ShapeDtypeStruct((T, w), dt) for w, dt in outs],
        grid=(T // tm,),
        in_specs=in_specs,
        out_specs=[row(w) for w, _ in outs],
        scratch_shapes=[pltpu.VMEM((1, LANES), F32)],
        compiler_params=_cparams(("arbitrary",)),
        name="inproj",
    )(x2d, prm["g_mix"], prm["w_main"], prm["b_misc"], tabs["tri_incl"], tabs["bd64"], tabs["ones"],
      prm["g_qa"], prm["g_ka"], prm["g_cq"], prm["g_ckv"], prm["w_uq"], prm["w_uk"], prm["w_uv"], tabs["rep"],
      prm["g_qb"], prm["g_kb"], prm["g_qc"], prm["g_kc"],
      tabs["cm"], tabs["sdm"], tabs["sum"], tabs["cd"], tabs["sdd"], tabs["sud"])


def _causal_kernel(*refs, seq, tq, split_heads, with_bias):
    if with_bias:
        q_ref, k_ref, v_ref, b_ref, o_ref = refs
    else:
        q_ref, k_ref, v_ref, o_ref = refs
        b_ref = None
    lane = lax.broadcasted_iota(jnp.int32, (tq, LANES), 1)
    lo_half = lane < HEAD_DIM
    r = lax.broadcasted_iota(jnp.int32, (tq, tq), 0)
    c = lax.broadcasted_iota(jnp.int32, (tq, tq), 1)
    keep = c <= r
    for qi in range(seq // tq):
        q = q_ref[0, qi * tq:(qi + 1) * tq, :]
        n_keys = (qi + 1) * tq
        v = v_ref[0, :n_keys, :]
        res = []
        for hh in range(2):
            if split_heads:
                qh = q[:, hh * LANES:(hh + 1) * LANES]
                k = k_ref[0, :n_keys, hh * LANES:(hh + 1) * LANES]
            else:
                zero = jnp.zeros_like(q)
                qh = jnp.where(lo_half, q, zero) if hh == 0 else jnp.where(lo_half, zero, q)
                k = k_ref[0, :n_keys, :]
            s = _dot_nt(qh, k)
            if with_bias:
                s = s + b_ref[0, 0, hh:hh + 1, :n_keys]
            s_diag = jnp.where(keep, s[:, n_keys - tq:], NEG_BIG)
            s = jnp.concatenate([s[:, :n_keys - tq], s_diag], axis=1) if qi else s_diag
            p = jnp.exp2(s - jnp.max(s, axis=-1, keepdims=True))
            l = jnp.sum(p, axis=-1, keepdims=True)
            res.append(_dot(p.astype(BF16), v) / l)
        o_ref[0, qi * tq:(qi + 1) * tq, :] = jnp.where(lo_half, res[0], res[1]).astype(o_ref.dtype)


def _causal_attention(q, k, v, kbias, *, tq, split_heads):
    B, S, _ = q.shape
    P = v.shape[2] // LANES
    wq = 2 * LANES if split_heads else LANES
    in_specs = [pl.BlockSpec((1, S, wq), lambda b, p: (b, 0, p)),
                pl.BlockSpec((1, S, wq), lambda b, p: (b, 0, p)),
                pl.BlockSpec((1, S, LANES), lambda b, p: (b, 0, p))]
    args = [q, k, v]
    if kbias is not None:
        in_specs.append(pl.BlockSpec((1, 1, 2, S), lambda b, p: (b, p, 0, 0)))
        args.append(kbias)
    return pl.pallas_call(
        functools.partial(_causal_kernel, seq=S, tq=tq, split_heads=split_heads, with_bias=kbias is not None),
        out_shape=jax.ShapeDtypeStruct((B, S, P * LANES), BF16),
        grid=(B, P),
        in_specs=in_specs,
        out_specs=pl.BlockSpec((1, S, LANES), lambda b, p: (b, 0, p)),
        compiler_params=_cparams(("parallel", "parallel")),
        name="causal_split" if split_heads else "causal_shared",
    )(*args)


def _dilated_kernel(q_ref, k_ref, v_ref, o_ref, acc_ref, m_ref, l_ref, *, seq, n, dilations):
    lane = lax.broadcasted_iota(jnp.int32, (1, 1, LANES), 2)
    lo_half = lane < HEAD_DIM
    row = lax.broadcasted_iota(jnp.int32, (n, n), 0)
    col = lax.broadcasted_iota(jnp.int32, (n, n), 1)
    own_keep = (col <= row)[None]
    prev_keep = (col >= row)[None]

    nblk = seq // n
    for pi, d in enumerate(dilations):
        L = seq // d
        nb = L // n

        def load(ref, d=d, L=L, nb=nb):
            if d == 1:
                return ref[0].reshape(nblk, n, LANES).astype(BF16)
            parts = [ref[0, pl.ds(r, L, stride=d), :].reshape(nb, n, LANES) for r in range(d)]
            return jnp.concatenate(parts, axis=0).astype(BF16)

        qb, kb, vb = load(q_ref), load(k_ref), load(v_ref)
        if nb > 1:
            zero = jnp.zeros((1, n, LANES), BF16)
            kp = jnp.concatenate([zero, kb[:nblk - 1]], axis=0)
            vp = jnp.concatenate([zero, vb[:nblk - 1]], axis=0)
            blk = lax.broadcasted_iota(jnp.int32, (nblk, n, n), 0)
            prev_ok = jnp.logical_and(prev_keep, lax.rem(blk, nb) != 0)
        zq = jnp.zeros_like(qb)
        heads = []
        for hh in range(2):
            qh = jnp.where(lo_half, qb, zq) if hh == 0 else jnp.where(lo_half, zq, qb)
            s_own = jnp.einsum("bqd,bkd->bqk", qh, kb, preferred_element_type=F32)
            s_own = jnp.where(own_keep, s_own, NEG_BIG)
            m = jnp.max(s_own, axis=-1, keepdims=True)
            if nb > 1:
                s_prev = jnp.einsum("bqd,bkd->bqk", qh, kp, preferred_element_type=F32)
                s_prev = jnp.where(prev_ok, s_prev, NEG_BIG)
                m = jnp.maximum(m, jnp.max(s_prev, axis=-1, keepdims=True))
            p_own = jnp.exp2(s_own - m)
            l = jnp.sum(p_own, axis=-1, keepdims=True)
            pv = jnp.einsum("bqk,bkd->bqd", p_own.astype(BF16), vb, preferred_element_type=F32)
            if nb > 1:
                p_prev = jnp.exp2(s_prev - m)
                l = l + jnp.sum(p_prev, axis=-1, keepdims=True)
                pv = pv + jnp.einsum("bqk,bkd->bqd", p_prev.astype(BF16), vp, preferred_element_type=F32)
            heads.append((m, l, pv))
        m_f = jnp.where(lo_half, heads[0][0], heads[1][0])
        l_f = jnp.where(lo_half, heads[0][1], heads[1][1])
        a_f = jnp.where(lo_half, heads[0][2], heads[1][2])
        for dst, val in ((acc_ref, a_f), (m_ref, m_f), (l_ref, l_f)):
            if d == 1:
                dst[pi] = val.reshape(seq, LANES)
            else:
                for r in range(d):
                    dst[pi, pl.ds(r, L, stride=d), :] = val[r * nb:(r + 1) * nb].reshape(L, LANES)

    np_ = len(dilations)
    m_all = m_ref[0]
    for pi in range(1, np_):
        m_all = jnp.maximum(m_all, m_ref[pi])
    num = jnp.zeros((seq, LANES), F32)
    den = jnp.zeros((seq, LANES), F32)
    for pi in range(np_):
        w = jnp.exp2(m_ref[pi] - m_all)
        num = num + w * acc_ref[pi]
        den = den + w * l_ref[pi]
    o_ref[0] = num / den


def _dilated_attention(qc, kc, vc, seq):
    B = qc.shape[0]
    n = DILATED_PATTERNS[0][0] // DILATED_PATTERNS[0][1]
    dil = tuple(d for _, d in DILATED_PATTERNS)
    for w, d in DILATED_PATTERNS:
        assert w // d == n and seq % (d * n) == 0
    spec = pl.BlockSpec((1, seq, LANES), lambda b, p: (b, 0, p))
    np_ = len(dil)
    return pl.pallas_call(
        functools.partial(_dilated_kernel, seq=seq, n=n, dilations=dil),
        out_shape=jax.ShapeDtypeStruct((B, seq, DIL_WIDTH), F32),
        grid=(B, DIL_WIDTH // LANES),
        in_specs=[spec, spec, spec],
        out_specs=spec,
        scratch_shapes=[pltpu.VMEM((np_, seq, LANES), F32)] * 3,
        compiler_params=_cparams(("parallel", "parallel")),
        name="dilated",
    )(qc, kc, vc)


def _memkv_kernel(mem_ref, g_ref, w_ref, gk_ref, bd64_ref, k_ref, v_ref):
    mn = _rms_rows(mem_ref[0], g_ref[...]).astype(BF16)
    kv = _dot(mn, w_ref[...])
    k = jnp.concatenate(_group_norm_tiles(kv[:, :MEM_WIDTH], bd64_ref[...], 1.0 / HEAD_DIM), axis=1)
    k_ref[0] = (k * gk_ref[...]).astype(BF16)
    v_ref[0] = kv[:, MEM_WIDTH:].astype(BF16)


def _memkv(mem, prm, tabs):
    B, M, D = mem.shape
    return pl.pallas_call(
        _memkv_kernel,
        out_shape=[jax.ShapeDtypeStruct((B, M, MEM_WIDTH), BF16)] * 2,
        grid=(B,),
        in_specs=[pl.BlockSpec((1, M, D), lambda b: (b, 0, 0)), _full((1, D)), _full((D, 2 * MEM_WIDTH)),
                  _full((1, MEM_WIDTH)), _full((LANES, LANES))],
        out_specs=[pl.BlockSpec((1, M, MEM_WIDTH), lambda b: (b, 0, 0))] * 2,
        compiler_params=_cparams(("parallel",)),
        name="memkv",
    )(mem, prm["g_mem"], prm["w_mkv"], prm["g_km"], tabs["bd64"])


def _postattn_kernel(x_ref, oa_ref, ob_ref, oc_ref, gg_ref, wout_ref, gx_ref, wmq_ref, gqm_ref, bd64_ref,
                     km_ref, vm_ref, wmo_ref, gf_ref, wrh_ref, wrl_ref, br_ref, tri_ref,
                     x2_ref, ti_ref, tw_ref, rk_ref, cnt_ref, carry_ref, *, n_experts):
    i = pl.program_id(0)
    tm = x_ref.shape[0]
    hm = tm // PROJ_SPLIT
    a0, a1 = FOX_WIDTH, FOX_WIDTH + MLA_WIDTH
    lane = lax.broadcasted_iota(jnp.int32, (hm, LANES), 1)
    lane_f = lane.astype(F32)
    lo_half = lane < HEAD_DIM
    tri = tri_ref[...]

    @pl.when(i == 0)
    def _():
        carry_ref[...] = jnp.zeros_like(carry_ref)

    carry = {"n": carry_ref[...]}

    def rows(g):
        rs = slice(g * hm, (g + 1) * hm)
        ma = _rms_rows(oa_ref[rs, :].astype(F32), gg_ref[:, :a0]).astype(BF16)
        mb = _rms_rows(ob_ref[rs, :].astype(F32), gg_ref[:, a0:a1]).astype(BF16)
        mc = _rms_rows(oc_ref[rs, :], gg_ref[:, a1:]).astype(BF16)
        yield
        x1 = (x_ref[rs, :] + _dot(ma, wout_ref[:a0, :]) + _dot(mb, wout_ref[a0:a1, :])
              + _dot(mc, wout_ref[a1:, :]))
        yield
        h2 = _rms_rows(x1, gx_ref[...]).astype(BF16)
        yield
        qm = _dot(h2, wmq_ref[...])
        yield
        qm_t = _group_norm_tiles(qm, bd64_ref[...], 1.0 / HEAD_DIM)
        outs = []
        for p in range(MEM_WIDTH // LANES):
            sl = slice(p * LANES, (p + 1) * LANES)
            q2 = (qm_t[p] * gqm_ref[:, sl]).astype(BF16)
            k2 = km_ref[0, :, sl]
            v2 = vm_ref[0, :, sl]
            zero = jnp.zeros_like(q2)
            res = []
            for hh in range(2):
                qh = jnp.where(lo_half, q2, zero) if hh == 0 else jnp.where(lo_half, zero, q2)
                s = _dot_nt(qh, k2)
                pr = jnp.exp2(s - jnp.max(s, axis=-1, keepdims=True))
                l = jnp.sum(pr, axis=-1, keepdims=True)
                res.append(_dot(pr.astype(BF16), v2) / l)
            outs.append(jnp.where(lo_half, res[0], res[1]))
            yield
        om = jnp.concatenate(outs, axis=1).astype(BF16)
        x2 = x1 + _dot(om, wmo_ref[...])
        x2_ref[rs, :] = x2
        yield
        h3 = _rms_rows(x2, gf_ref[...])
        h3_hi = h3.astype(BF16)
        h3_lo = (h3 - h3_hi.astype(F32)).astype(BF16)
        yield
        logits = _dot(h3_hi, wrh_ref[...]) + _dot(h3_hi, wrl_ref[...]) + _dot(h3_lo, wrh_ref[...]) + br_ref[...]
        yield
        logits = jnp.where(lane < n_experts, logits, NEG_BIG)
        work = logits
        sel = jnp.zeros((hm, LANES), F32)
        vals, idxs = [], []
        for _ in range(TOP_K):
            mv = jnp.max(work, axis=-1, keepdims=True)
            ix = jnp.min(jnp.where(work == mv, lane_f, float(LANES)), axis=-1, keepdims=True)
            hit = lane_f == ix
            work = jnp.where(hit, NEG_BIG, work)
            sel = jnp.where(hit, 1.0, sel)
            vals.append(mv)
            idxs.append(ix)
        es = [jnp.exp(v - vals[0]) for v in vals]
        den = es[0] + es[1] + es[2] + es[3]
        yield
        before = _dot(tri, sel.astype(BF16))
        yield
        before = before + carry["n"]
        carry["n"] = before[hm - 1:, :] + sel[hm - 1:, :]
        ti = jnp.zeros((hm, LANES), F32)
        tw = jnp.zeros((hm, LANES), F32)
        rk = jnp.zeros((hm, LANES), F32)
        for kk in range(TOP_K):
            at = lane == kk
            ti = jnp.where(at, idxs[kk], ti)
            tw = jnp.where(at, es[kk] / den, tw)
            r_k = jnp.sum(jnp.where(lane_f == idxs[kk], before, 0.0), axis=-1, keepdims=True)
            rk = jnp.where(at, r_k, rk)
        ti_ref[rs, :] = ti.astype(jnp.int32)
        tw_ref[rs, :] = tw
        rk_ref[rs, :] = rk.astype(jnp.int32)

    _interleave([rows(g) for g in range(PROJ_SPLIT)])
    carry_ref[...] = carry["n"]
    cnt_ref[...] = carry["n"]


def _postattn(x2d, oa, ob, oc, kmem, vmem, prm, tabs, seq, tm, n_experts):
    T, D = x2d.shape
    M = kmem.shape[1]
    tpb = seq // tm
    hm = tm // PROJ_SPLIT
    row = lambda w: pl.BlockSpec((tm, w), lambda i: (i, 0))
    memspec = pl.BlockSpec((1, M, MEM_WIDTH), lambda i: (i // tpb, 0, 0))
    mix = FOX_WIDTH + MLA_WIDTH + DIL_WIDTH
    return pl.pallas_call(
        functools.partial(_postattn_kernel, n_experts=n_experts),
        out_shape=[jax.ShapeDtypeStruct((T, D), F32),
                   jax.ShapeDtypeStruct((T, LANES), jnp.int32), jax.ShapeDtypeStruct((T, LANES), F32),
                   jax.ShapeDtypeStruct((T, LANES), jnp.int32), jax.ShapeDtypeStruct((1, LANES), F32)],
        grid=(T // tm,),
        in_specs=[row(D), row(FOX_WIDTH), row(MLA_WIDTH), row(DIL_WIDTH), _full((1, mix)), _full((mix, D)),
                  _full((1, D)), _full((D, MEM_WIDTH)), _full((1, MEM_WIDTH)), _full((LANES, LANES)),
                  memspec, memspec, _full((MEM_WIDTH, D)), _full((1, D)),
                  _full((D, LANES)), _full((D, LANES)), _full((1, LANES)), _full((hm, hm))],
        out_specs=[row(D), row(LANES), row(LANES), row(LANES), _full((1, LANES))],
        scratch_shapes=[pltpu.VMEM((1, LANES), F32)],
        compiler_params=_cparams(("arbitrary",)),
        name="postattn",
    )(x2d, oa, ob, oc, prm["g_group"], prm["w_out"], prm["g_xmem"], prm["w_mq"], prm["g_qm"], tabs["bd64"],
      kmem, vmem, prm["w_mo"], prm["g_ffn"], prm["w_r_hi"], prm["w_r_lo"], prm["b_router"], tabs["tri_strict"])


def _row_copy(src_ref, src_row, dst_ref, dst_row, sem):
    return pltpu.make_async_copy(src_ref.at[pl.ds(src_row, 1), :], dst_ref.at[pl.ds(dst_row, 1), :], sem)


def _dispatch_kernel(fill_ref, pos_ref, x_ref, xs_ref, zero_ref, sem, zsem, *, tm, tr):
    @pl.when(pl.program_id(0) == 0)
    def _():
        n_experts = fill_ref.shape[0] - 1
        n_rows = xs_ref.shape[0]
        zero_ref[...] = jnp.zeros_like(zero_ref)
        starts = [(fill_ref[e], fill_ref[e] >= 0) for e in range(n_experts)]
        for j in range(n_experts + 1):
            start = fill_ref[n_experts] + j * tr
            starts.append((start, start < n_rows))

        def fill(start):
            return pltpu.make_async_copy(zero_ref, xs_ref.at[pl.ds(pl.multiple_of(start, tr), tr), :], zsem)

        for start, live in starts:
            @pl.when(live)
            def _(start=start):
                fill(start).start()
        for start, live in starts:
            @pl.when(live)
            def _(start=start):
                fill(start).wait()

    def issue(r, _):
        for kk in range(TOP_K):
            _row_copy(x_ref, r, xs_ref, pos_ref[0, 0, r * TOP_K + kk], sem).start(priority=kk % 2)
        return 0
    lax.fori_loop(0, tm, issue, 0, unroll=ROW_DMA_UNROLL)
    n = tm * TOP_K
    pltpu.make_async_copy(xs_ref.at[pl.ds(0, n), :], xs_ref.at[pl.ds(0, n), :], sem).wait()


def _dispatch(x2, pos, pad_start, n_rows, tm, tr):
    T, D = x2.shape
    return pl.pallas_call(
        functools.partial(_dispatch_kernel, tm=tm, tr=tr),
        out_shape=jax.ShapeDtypeStruct((n_rows, D), x2.dtype),
        grid_spec=pltpu.PrefetchScalarGridSpec(
            num_scalar_prefetch=1,
            grid=(T // tm,),
            in_specs=[pl.BlockSpec((1, 1, tm * TOP_K), lambda i, pad: (i, 0, 0), memory_space=pltpu.SMEM),
                      pl.BlockSpec((tm, D), lambda i, pad: (i, 0))],
            out_specs=pl.BlockSpec(memory_space=pl.ANY),
            scratch_shapes=[pltpu.VMEM((tr, D), x2.dtype), pltpu.SemaphoreType.DMA, pltpu.SemaphoreType.DMA],
        ),
        compiler_params=_cparams(("arbitrary",)),
        name="dispatch",
    )(pad_start, pos.reshape(T // tm, 1, tm * TOP_K), x2)


def _experts_kernel(te_ref, nu_ref, xs_ref, gf_ref, w1_ref, b1g_ref, b1l_ref, w2_ref, b2_ref, perm_ref,
                    ys_ref, w1s_ref, w2s_ref):
    i = pl.program_id(0)
    two_f = w1_ref.shape[3]
    n_blk = two_f // DEINT_BLOCK
    half = DEINT_BLOCK // 2

    @pl.when(i < nu_ref[0])
    def _():
        @pl.when(jnp.logical_or(i == 0, te_ref[i] != te_ref[jnp.maximum(i - 1, 0)]))
        def _():
            perm = perm_ref[...]
            for b in range(n_blk):
                sl = slice(b * DEINT_BLOCK, (b + 1) * DEINT_BLOCK)
                w1s_ref[:, sl] = _dot(w1_ref[0, 0, :, sl].astype(BF16), perm).astype(BF16)
            w2s_ref[...] = w2_ref[0, 0].astype(BF16)

        x = _rms_rows(xs_ref[...], gf_ref[...]).astype(BF16)
        h = _dot(x, w1s_ref[...])
        acts = []
        for b in range(n_blk):
            glu = h[:, b * DEINT_BLOCK:b * DEINT_BLOCK + half] + b1g_ref[0, :, b * half:(b + 1) * half]
            lin = h[:, b * DEINT_BLOCK + half:(b + 1) * DEINT_BLOCK] + b1l_ref[0, :, b * half:(b + 1) * half]
            glu = jnp.minimum(glu, SWIGLU_LIMIT)
            lin = jnp.clip(lin, -SWIGLU_LIMIT, SWIGLU_LIMIT)
            acts.append((glu * (1.0 / (1.0 + jnp.exp(-SWIGLU_ALPHA * glu))) * (lin + 1.0)).astype(BF16))
        act = jnp.concatenate(acts, axis=1) if n_blk > 1 else acts[0]
        ys_ref[...] = _dot(act, w2s_ref[...]) + b2_ref[0]

    @pl.when(i >= nu_ref[0])
    def _():
        ys_ref[...] = jnp.zeros_like(ys_ref)


def _experts(tile_expert, n_used, xs, w_e1, w_e2, layer, prm, tabs, tr):
    n_rows, D = xs.shape
    two_f = w_e1.shape[3]
    F = two_f // 2
    assert two_f % DEINT_BLOCK == 0
    n_tiles = n_rows // tr
    rows = pl.BlockSpec((tr, D), lambda i, te, nu: (i, 0))
    wspec = lambda s: pl.BlockSpec((1,) + s, lambda i, te, nu: (te[i], 0, 0))
    lspec = lambda s: pl.BlockSpec((1, 1) + s, lambda i, te, nu: (layer, te[i], 0, 0))
    return pl.pallas_call(
        _experts_kernel,
        out_shape=jax.ShapeDtypeStruct((n_rows, D), F32),
        grid_spec=pltpu.PrefetchScalarGridSpec(
            num_scalar_prefetch=2,
            grid=(n_tiles,),
            in_specs=[rows, pl.BlockSpec((1, D), lambda i, te, nu: (0, 0)),
                      lspec((D, two_f)), wspec((1, F)), wspec((1, F)), lspec((F, D)), wspec((1, D)),
                      pl.BlockSpec((DEINT_BLOCK, DEINT_BLOCK), lambda i, te, nu: (0, 0))],
            out_specs=rows,
            scratch_shapes=[pltpu.VMEM((D, two_f), BF16), pltpu.VMEM((F, D), BF16)],
        ),
        compiler_params=_cparams(("arbitrary",)),
        name="experts",
    )(tile_expert, n_used, xs, prm["g_ffn"], w_e1, prm["b1g"], prm["b1l"], w_e2, prm["b2"], tabs["deint"])


def _combine_kernel(pos_ref, x_ref, tw_ref, ys_ref, o_ref, buf, sem, *, tm):
    def issue(r, _):
        for kk in range(TOP_K):
            _row_copy(ys_ref, pos_ref[0, 0, r * TOP_K + kk], buf.at[kk], r, sem).start(priority=kk % 2)
        return 0
    lax.fori_loop(0, tm, issue, 0, unroll=ROW_DMA_UNROLL)
    pltpu.make_async_copy(buf, buf, sem).wait()

    tw = tw_ref[...]
    acc = x_ref[...]
    for kk in range(TOP_K):
        acc = acc + tw[:, kk:kk + 1] * buf[kk]
    o_ref[...] = acc


def _combine(x2, tw, ys, pos, tm):
    T, D = x2.shape
    W = ys.shape[1]
    return pl.pallas_call(
        functools.partial(_combine_kernel, tm=tm),
        out_shape=jax.ShapeDtypeStruct((T, D), F32),
        grid=(T // tm,),
        in_specs=[pl.BlockSpec((1, 1, tm * TOP_K), lambda i: (i, 0, 0), memory_space=pltpu.SMEM),
                  pl.BlockSpec((tm, D), lambda i: (i, 0)),
                  pl.BlockSpec((tm, LANES), lambda i: (i, 0)),
                  pl.BlockSpec(memory_space=pl.ANY)],
        out_specs=pl.BlockSpec((tm, D), lambda i: (i, 0)),
        scratch_shapes=[pltpu.VMEM((TOP_K, tm, W), ys.dtype), pltpu.SemaphoreType.DMA],
        compiler_params=_cparams(("arbitrary",)),
        name="combine",
    )(pos.reshape(T // tm, 1, tm * TOP_K), x2, tw, ys)


def _tile_gain(g, reps, scale=1.0):
    return (jnp.tile(g.astype(F32), reps) * scale)[None, :]


def _prep_layer(l, D, w_in, b_forget, g_mix, g_qa, g_ka, g_cq, g_ckv, w_uq, w_ukv, g_qb, g_kb, g_qc, g_kc,
                g_group, w_out, g_xmem, g_mem, w_mq, w_mkv, g_qm, g_km, w_mo, g_ffn, w_router, b_router,
                w_e1, b_e1, w_e2, b_e2):
    wi = w_in[l]
    sizes = (FOX_WIDTH, FOX_WIDTH, FOX_WIDTH, FOX_HEADS, MLA_Q_RANK, MLA_KV_RANK, MLA_ROPE_DIM,
             DIL_WIDTH, DIL_WIDTH, DIL_WIDTH)
    pts = np.cumsum((0,) + sizes)
    qa, ka, va, fa, cq, ckv, kr, qc, kc, vc = [wi[:, pts[j]:pts[j + 1]] for j in range(len(sizes))]
    misc = jnp.zeros((D, LANES), F32).at[:, :FOX_HEADS].set(fa).at[:, 32:32 + MLA_ROPE_DIM].set(kr)
    w_main = jnp.concatenate([qa, ka, va, cq, ckv, qc, kc, vc, misc], axis=1).astype(BF16)
    b_misc = jnp.zeros((1, LANES), F32).at[0, :FOX_HEADS].set(b_forget[l])

    uq = w_uq[l].reshape(MLA_Q_RANK, MLA_HEADS, MLA_QK_DIM)
    uq = jnp.pad(uq, ((0, 0), (0, 0), (0, MLA_PAD - MLA_QK_DIM))).reshape(MLA_Q_RANK, MLA_HEADS * MLA_PAD)
    ukv = w_ukv[l].reshape(MLA_KV_RANK, MLA_HEADS, MLA_NOPE_DIM + MLA_V_DIM)
    uk = jnp.pad(ukv[:, :, :MLA_NOPE_DIM], ((0, 0), (0, 0), (0, MLA_PAD - MLA_NOPE_DIM)))
    uk = uk.reshape(MLA_KV_RANK, MLA_HEADS * MLA_PAD)
    uv = ukv[:, :, MLA_NOPE_DIM:].reshape(MLA_KV_RANK, MLA_WIDTH)
    pad_gain = lambda g: jnp.pad(g.astype(F32), (0, MLA_PAD - MLA_QK_DIM))

    E = w_e1.shape[1]
    wr = jnp.zeros((D, LANES), F32).at[:, :E].set(w_router[l])
    wr_hi = wr.astype(BF16)
    wr_lo = (wr - wr_hi.astype(F32)).astype(BF16)
    q_scale = lambda dim: dim ** -0.5 * LOG2E
    return dict(
        g_mix=g_mix[l][None], w_main=w_main, b_misc=b_misc,
        g_qa=_tile_gain(g_qa[l], FOX_HEADS, q_scale(HEAD_DIM)), g_ka=_tile_gain(g_ka[l], FOX_HEADS),
        g_cq=g_cq[l][None], g_ckv=g_ckv[l][None],
        w_uq=uq.astype(BF16), w_uk=uk.astype(BF16), w_uv=uv.astype(BF16),
        g_qb=_tile_gain(pad_gain(g_qb[l]), MLA_HEADS, q_scale(MLA_QK_DIM)),
        g_kb=_tile_gain(pad_gain(g_kb[l]), MLA_HEADS),
        g_qc=_tile_gain(g_qc[l], DIL_HEADS, q_scale(HEAD_DIM)), g_kc=_tile_gain(g_kc[l], DIL_HEADS),
        g_group=g_group[l][None], w_out=w_out[l].astype(BF16),
        g_xmem=g_xmem[l][None], g_mem=g_mem[l][None],
        w_mq=w_mq[l].astype(BF16), w_mkv=w_mkv[l].astype(BF16),
        g_qm=_tile_gain(g_qm[l], MEM_HEADS, q_scale(HEAD_DIM)), g_km=_tile_gain(g_km[l], MEM_HEADS),
        w_mo=w_mo[l].astype(BF16), g_ffn=g_ffn[l][None],
        w_r_hi=wr_hi, w_r_lo=wr_lo,
        b_router=jnp.zeros((1, LANES), F32).at[0, :E].set(b_router[l]),
        b1g=b_e1[l][:, None, 0::2], b1l=b_e1[l][:, None, 1::2], b2=b_e2[l][:, None, :],
    )


def _rope_lane_tables(positions, rot_dim, period, offset):
    half = rot_dim // 2
    inv = ROPE_THETA ** (-jnp.arange(0, rot_dim, 2, dtype=F32) / rot_dim)
    ang = positions.astype(F32).reshape(-1)[:, None] * inv
    cos, sin = jnp.cos(ang), jnp.sin(ang)
    lane = np.arange(LANES) % period - offset
    idx = np.clip(lane % half if half else lane, 0, half - 1)
    first = (lane >= 0) & (lane < half)
    second = (lane >= half) & (lane < rot_dim)
    c = jnp.where(first | second, cos[:, idx], 1.0)
    s_dn = jnp.where(second, sin[:, idx], 0.0)
    s_up = jnp.where(first, -sin[:, idx], 0.0)
    return c, s_dn, s_up


def _tables(positions, tm):
    r = np.arange(tm // PROJ_SPLIT)
    bd = (np.arange(LANES)[:, None] // HEAD_DIM) == (np.arange(LANES)[None, :] // HEAD_DIM)
    rep = np.zeros((LANES, MLA_HEADS * MLA_PAD), np.float32)
    for h in range(MLA_HEADS):
        for j in range(MLA_ROPE_DIM):
            rep[32 + j, h * MLA_PAD + MLA_NOPE_DIM + j] = 1.0
    half = DEINT_BLOCK // 2
    deint = np.zeros((DEINT_BLOCK, DEINT_BLOCK), np.float32)
    deint[2 * np.arange(half), np.arange(half)] = 1.0
    deint[2 * np.arange(half) + 1, half + np.arange(half)] = 1.0
    cm, sdm, sum_ = _rope_lane_tables(positions, MLA_ROPE_DIM, LANES, MLA_NOPE_DIM)
    cd, sdd, sud = _rope_lane_tables(positions, ROT_DIM, HEAD_DIM, 0)
    return dict(
        tri_incl=jnp.asarray(r[:, None] >= r[None, :], BF16),
        tri_strict=jnp.asarray(r[:, None] > r[None, :], BF16),
        bd64=jnp.asarray(bd, BF16), ones=jnp.ones((LANES, LANES), BF16), rep=jnp.asarray(rep, BF16),
        deint=jnp.asarray(deint, BF16),
        cm=cm, sdm=sdm, sum=sum_, cd=cd, sdd=sdd, sud=sud,
    )


def _routing_tables(ti, rk, cnt, n_experts, tr, n_tiles):
    counts = cnt[0, :n_experts].astype(jnp.int32)
    padded = ((counts + tr - 1) // tr) * tr
    ends = jnp.cumsum(padded)
    starts = ends - padded
    idx = ti[:, :TOP_K]
    hit = idx[:, :, None] == jnp.arange(n_experts, dtype=jnp.int32)
    pos = rk[:, :TOP_K] + jnp.sum(jnp.where(hit, starts, 0), axis=-1)
    tile_start = jnp.arange(n_tiles, dtype=jnp.int32) * tr
    tile_expert = jnp.sum((ends[None, :] <= tile_start[:, None]).astype(jnp.int32), axis=1)
    tile_expert = jnp.minimum(tile_expert, n_experts - 1)
    n_used = (ends[n_experts - 1:] // tr).astype(jnp.int32)
    pad_fill = jnp.where(padded > 0, ends - tr, -1)
    fill_start = jnp.concatenate([pad_fill, ends[n_experts - 1:]])
    return pos.astype(jnp.int32), tile_expert.astype(jnp.int32), fill_start.astype(jnp.int32), n_used


def _pick_tile(seq, want):
    t = min(want, seq)
    while seq % t:
        t //= 2
    return t


def kernel(x, mem, positions, g_mix, w_in, b_forget, g_qa, g_ka, g_cq, g_ckv, w_uq, w_ukv, g_qb, g_kb, g_qc, g_kc, g_group, w_out, g_xmem, g_mem, w_mq, w_mkv, g_qm, g_km, w_mo, g_ffn, w_router, b_router, w_e1, b_e1, w_e2, b_e2):
    B, S, D = x.shape
    T = B * S
    depth = w_in.shape[0]
    E = w_e1.shape[1]
    tm = _pick_tile(S, PROJ_TILE)
    tq = _pick_tile(S, ATTN_TILE)
    tr = EXPERT_ROW_TILE
    tmoe = _pick_tile(S, MOE_TOKEN_TILE)
    n_rows = T * TOP_K + (E + 1) * tr
    n_tiles = n_rows // tr

    tabs = _tables(positions, tm)
    x2d = x.reshape(T, D)
    for l in range(depth):
        prm = _prep_layer(l, D, w_in, b_forget, g_mix, g_qa, g_ka, g_cq, g_ckv, w_uq, w_ukv, g_qb, g_kb,
                          g_qc, g_kc, g_group, w_out, g_xmem, g_mem, w_mq, w_mkv, g_qm, g_km, w_mo, g_ffn,
                          w_router, b_router, w_e1, b_e1, w_e2, b_e2)
        qa, ka, va, csum, qb, kb, vb, qc, kc, vc = _inproj(x2d, prm, tabs, S, tm)
        kbias = -LOG2E * csum[:, :FOX_HEADS].reshape(B, S, FOX_HEADS // 2, 2).transpose(0, 2, 3, 1)
        r3 = lambda a: a.reshape(B, S, a.shape[-1])
        oa = _causal_attention(r3(qa), r3(ka), r3(va), kbias, tq=tq, split_heads=False)
        ob = _causal_attention(r3(qb), r3(kb), r3(vb), None, tq=tq, split_heads=True)
        oc = _dilated_attention(r3(qc), r3(kc), r3(vc), S)
        kmem, vmem = _memkv(mem, prm, tabs)
        x2, ti, tw, rk, cnt = _postattn(x2d, oa.reshape(T, -1), ob.reshape(T, -1), oc.reshape(T, -1),
                                        kmem, vmem, prm, tabs, S, tm, E)
        pos, tile_expert, pad_start, n_used = _routing_tables(ti, rk, cnt, E, tr, n_tiles)
        xs = _dispatch(x2, pos, pad_start, n_rows, tmoe, tr)
        ys = _experts(tile_expert, n_used, xs, w_e1, w_e2, l, prm, tabs, tr)
        x2d = _combine(x2, tw, ys, pos, tmoe)
    return x2d.reshape(B, S, D)
```

```python
import functools
import math

import jax
import jax.numpy as jnp
import numpy as np
from jax import lax
from jax.experimental import pallas as pl
from jax.experimental.pallas import tpu as pltpu

HEAD_DIM = 64
FOX_HEADS = 4
MLA_HEADS = 6
DIL_HEADS = 6
MLA_Q_RANK = 384
MLA_KV_RANK = 256
MLA_NOPE_DIM = 64
MLA_ROPE_DIM = 32
MLA_QK_DIM = MLA_NOPE_DIM + MLA_ROPE_DIM
MLA_V_DIM = 64
ROT_DIM = HEAD_DIM // 4
ROPE_THETA = 500000.0
DILATED_PATTERNS = ((128, 1), (512, 4), (2048, 16))
FOX_WIDTH = FOX_HEADS * HEAD_DIM
MLA_WIDTH = MLA_HEADS * MLA_V_DIM
DIL_WIDTH = DIL_HEADS * HEAD_DIM
MEM_HEADS = 4
MEM_WIDTH = MEM_HEADS * HEAD_DIM
TOP_K = 4
SWIGLU_ALPHA = 1.702
SWIGLU_LIMIT = 7.0
RMS_EPS = 1e-6
LOG2E = math.log2(math.e)

LANES = 128
VMEM_LIMIT_BYTES = 56 * 1024 * 1024

INPROJ_TILE = 512
INPROJ_SPLIT = 1
POSTATTN_TILE = 1024
POSTATTN_SPLIT = 2
ATTN_TILE = 512
EXPERT_ROW_TILE = 512
MOE_TOKEN_TILE = 1024
ROW_DMA_UNROLL = 8

NEG_BIG = -1e30
MLA_PAD = 128
DEINT_BLOCK = 256

F32 = jnp.float32
BF16 = jnp.bfloat16


def _cparams(sem):
    return pltpu.CompilerParams(dimension_semantics=sem, vmem_limit_bytes=VMEM_LIMIT_BYTES)


def _full(shape):
    nd = len(shape)
    return pl.BlockSpec(shape, lambda *_: (0,) * nd)


def _dot(a, b):
    return jnp.dot(a, b, preferred_element_type=F32)


def _dot_nt(a, b):
    return lax.dot_general(a, b, (((1,), (1,)), ((), ())), preferred_element_type=F32)


def _rms_rows(x, g):
    return x * lax.rsqrt(jnp.mean(x * x, axis=-1, keepdims=True) + RMS_EPS) * g


def _group_norm_tiles(y, ones_mat, inv_n):
    outs = []
    for j in range(y.shape[1] // LANES):
        yj = y[:, j * LANES:(j + 1) * LANES]
        ss = _dot((yj * yj).astype(BF16), ones_mat)
        outs.append(yj * lax.rsqrt(ss * inv_n + RMS_EPS))
    return outs


def _rope_tile(y, half, c, s_dn, s_up):
    return y * c + pltpu.roll(y, half, 1) * s_dn + pltpu.roll(y, LANES - half, 1) * s_up


def _interleave(gens):
    live = list(gens)
    while live:
        for g in list(live):
            try:
                next(g)
            except StopIteration:
                live.remove(g)


def _inproj_kernel(x_ref, gmix_ref, w_ref, bmisc_ref, tri_ref, bd64_ref, ones_ref,
                   gqa_ref, gka_ref, gcq_ref, gckv_ref, wuq_ref, wuk_ref, wuv_ref, rep_ref,
                   gqb_ref, gkb_ref, gqc_ref, gkc_ref,
                   cm_ref, sdm_ref, sum_ref, cd_ref, sdd_ref, sud_ref,
                   qa_ref, ka_ref, va_ref, c_ref, qb_ref, kb_ref, vb_ref, qc_ref, kc_ref, vc_ref,
                   carry_ref, *, tiles_per_batch):
    i = pl.program_id(0)
    tm = x_ref.shape[0]
    hm = tm // INPROJ_SPLIT
    bd64 = bd64_ref[...]
    ones = ones_ref[...]
    tri = tri_ref[...]
    half_b = MLA_ROPE_DIM // 2
    half_c = ROT_DIM // 2
    offs = np.cumsum((0, FOX_WIDTH, FOX_WIDTH, FOX_WIDTH, MLA_Q_RANK, MLA_KV_RANK,
                      DIL_WIDTH, DIL_WIDTH, DIL_WIDTH, LANES))

    @pl.when(i % tiles_per_batch == 0)
    def _():
        carry_ref[...] = jnp.zeros_like(carry_ref)

    carry = {"c": carry_ref[...]}

    def rows(g):
        rs = slice(g * hm, (g + 1) * hm)
        h = _rms_rows(x_ref[rs, :], gmix_ref[...]).astype(BF16)
        seg = lambda j: _dot(h, w_ref[:, offs[j]:offs[j + 1]])
        yield
        qa, ka, va, cq = seg(0), seg(1), seg(2), seg(3)
        yield
        qa_ref[rs, :] = (jnp.concatenate(_group_norm_tiles(qa, bd64, 1.0 / HEAD_DIM), axis=1)
                         * gqa_ref[...]).astype(BF16)
        ka_ref[rs, :] = (jnp.concatenate(_group_norm_tiles(ka, bd64, 1.0 / HEAD_DIM), axis=1)
                         * gka_ref[...]).astype(BF16)
        va_ref[rs, :] = va.astype(BF16)
        cqn = _rms_rows(cq, gcq_ref[...]).astype(BF16)
        yield
        ckv, qc = seg(4), seg(5)
        qb_pre = _dot(cqn, wuq_ref[...])
        yield
        ckvn = _rms_rows(ckv, gckv_ref[...]).astype(BF16)
        cd, sdd, sud = cd_ref[rs, :], sdd_ref[rs, :], sud_ref[rs, :]
        qc_t = _group_norm_tiles(qc, bd64, 1.0 / HEAD_DIM)
        for j in range(DIL_WIDTH // LANES):
            sl = slice(j * LANES, (j + 1) * LANES)
            qc_ref[rs, sl] = _rope_tile(qc_t[j] * gqc_ref[:, sl], half_c, cd, sdd, sud)
        yield
        kc, vc, misc = seg(6), seg(7), seg(8)
        kb_nope = _dot(ckvn, wuk_ref[...])
        vb_ref[rs, :] = _dot(ckvn, wuv_ref[...]).astype(BF16)
        yield
        cm, sdm, sum_ = cm_ref[rs, :], sdm_ref[rs, :], sum_ref[rs, :]
        qb_t = _group_norm_tiles(qb_pre, ones, 1.0 / MLA_QK_DIM)
        for j in range(MLA_HEADS):
            sl = slice(j * LANES, (j + 1) * LANES)
            qb_ref[rs, sl] = _rope_tile(qb_t[j] * gqb_ref[:, sl], half_b, cm, sdm, sum_).astype(BF16)
        kc_t = _group_norm_tiles(kc, bd64, 1.0 / HEAD_DIM)
        for j in range(DIL_WIDTH // LANES):
            sl = slice(j * LANES, (j + 1) * LANES)
            kc_ref[rs, sl] = _rope_tile(kc_t[j] * gkc_ref[:, sl], half_c, cd, sdd, sud)
        vc_ref[rs, :] = vc
        z = misc + bmisc_ref[...]
        logf = jnp.minimum(z, 0.0) - jnp.log(1.0 + jnp.exp(-jnp.abs(z)))
        lane = lax.broadcasted_iota(jnp.int32, logf.shape, 1)
        logf = jnp.where(lane < FOX_HEADS, logf, 0.0)
        p0 = logf.astype(BF16)
        r0 = logf - p0.astype(F32)
        p1 = r0.astype(BF16)
        p2 = (r0 - p1.astype(F32)).astype(BF16)
        yield
        csum = _dot(tri, p0) + _dot(tri, p1) + _dot(tri, p2)
        kb_pre = kb_nope + _dot(misc.astype(BF16), rep_ref[...])
        yield
        kb_t = _group_norm_tiles(kb_pre, ones, 1.0 / MLA_QK_DIM)
        for j in range(MLA_HEADS):
            sl = slice(j * LANES, (j + 1) * LANES)
            kb_ref[rs, sl] = _rope_tile(kb_t[j] * gkb_ref[:, sl], half_b, cm, sdm, sum_).astype(BF16)
        csum = csum + carry["c"]
        c_ref[rs, :] = csum
        carry["c"] = csum[hm - 1:, :]

    _interleave([rows(g) for g in range(INPROJ_SPLIT)])
    carry_ref[...] = carry["c"]


def _inproj(x2d, prm, tabs, seq, tm):
    T, D = x2d.shape
    nm = prm["w_main"].shape[1]
    hm = tm // INPROJ_SPLIT
    row = lambda w: pl.BlockSpec((tm, w), lambda i: (i, 0))
    in_specs = [row(D), _full((1, D)), _full((D, nm)), _full((1, LANES)), _full((hm, hm)),
                _full((LANES, LANES)), _full((LANES, LANES)),
                _full((1, FOX_WIDTH)), _full((1, FOX_WIDTH)), _full((1, MLA_Q_RANK)), _full((1, MLA_KV_RANK)),
                _full((MLA_Q_RANK, MLA_HEADS * MLA_PAD)), _full((MLA_KV_RANK, MLA_HEADS * MLA_PAD)),
                _full((MLA_KV_RANK, MLA_WIDTH)), _full((LANES, MLA_HEADS * MLA_PAD)),
                _full((1, MLA_HEADS * MLA_PAD)), _full((1, MLA_HEADS * MLA_PAD)),
                _full((1, DIL_WIDTH)), _full((1, DIL_WIDTH))] + [row(LANES)] * 6
    outs = [(FOX_WIDTH, BF16), (FOX_WIDTH, BF16), (FOX_WIDTH, BF16), (LANES, F32),
            (MLA_HEADS * MLA_PAD, BF16), (MLA_HEADS * MLA_PAD, BF16), (MLA_WIDTH, BF16),
            (DIL_WIDTH, F32), (DIL_WIDTH, F32), (DIL_WIDTH, F32)]
    return pl.pallas_call(
        functools.partial(_inproj_kernel, tiles_per_batch=seq // tm),
        out_shape=[jax.ShapeDtypeStruct((T, w), dt) for w, dt in outs],
        grid=(T // tm,),
        in_specs=in_specs,
        out_specs=[row(w) for w, _ in outs],
        scratch_shapes=[pltpu.VMEM((1, LANES), F32)],
        compiler_params=_cparams(("arbitrary",)),
        name="inproj",
    )(x2d, prm["g_mix"], prm["w_main"], prm["b_misc"], tabs["tri_incl"], tabs["bd64"], tabs["ones"],
      prm["g_qa"], prm["g_ka"], prm["g_cq"], prm["g_ckv"], prm["w_uq"], prm["w_uk"], prm["w_uv"], tabs["rep"],
      prm["g_qb"], prm["g_kb"], prm["g_qc"], prm["g_kc"],
      tabs["cm"], tabs["sdm"], tabs["sum"], tabs["cd"], tabs["sdd"], tabs["sud"])


def _causal_kernel(*refs, seq, tq, split_heads, with_bias):
    if with_bias:
        q_ref, k_ref, v_ref, b_ref, o_ref = refs
    else:
        q_ref, k_ref, v_ref, o_ref = refs
        b_ref = None
    lane = lax.broadcasted_iota(jnp.int32, (tq, LANES), 1)
    lo_half = lane < HEAD_DIM
    r = lax.broadcasted_iota(jnp.int32, (tq, tq), 0)
    c = lax.broadcasted_iota(jnp.int32, (tq, tq), 1)
    keep = c <= r
    for qi in range(seq // tq):
        q = q_ref[0, qi * tq:(qi + 1) * tq, :]
        n_keys = (qi + 1) * tq
        v = v_ref[0, :n_keys, :]
        res = []
        for hh in range(2):
            if split_heads:
                qh = q[:, hh * LANES:(hh + 1) * LANES]
                k = k_ref[0, :n_keys, hh * LANES:(hh + 1) * LANES]
            else:
                zero = jnp.zeros_like(q)
                qh = jnp.where(lo_half, q, zero) if hh == 0 else jnp.where(lo_half, zero, q)
                k = k_ref[0, :n_keys, :]
            s = _dot_nt(qh, k)
            if with_bias:
                s = s + b_ref[0, 0, hh:hh + 1, :n_keys]
            s_diag = jnp.where(keep, s[:, n_keys - tq:], NEG_BIG)
            s = jnp.concatenate([s[:, :n_keys - tq], s_diag], axis=1) if qi else s_diag
            p = jnp.exp2(s - jnp.max(s, axis=-1, keepdims=True))
            l = jnp.sum(p, axis=-1, keepdims=True)
            res.append(_dot(p.astype(BF16), v) / l)
        o_ref[0, qi * tq:(qi + 1) * tq, :] = jnp.where(lo_half, res[0], res[1]).astype(o_ref.dtype)


def _causal_attention(q, k, v, kbias, *, tq, split_heads):
    B, S, _ = q.shape
    P = v.shape[2] // LANES
    wq = 2 * LANES if split_heads else LANES
    in_specs = [pl.BlockSpec((1, S, wq), lambda b, p: (b, 0, p)),
                pl.BlockSpec((1, S, wq), lambda b, p: (b, 0, p)),
                pl.BlockSpec((1, S, LANES), lambda b, p: (b, 0, p))]
    args = [q, k, v]
    if kbias is not None:
        in_specs.append(pl.BlockSpec((1, 1, 2, S), lambda b, p: (b, p, 0, 0)))
        args.append(kbias)
    return pl.pallas_call(
        functools.partial(_causal_kernel, seq=S, tq=tq, split_heads=split_heads, with_bias=kbias is not None),
        out_shape=jax.ShapeDtypeStruct((B, S, P * LANES), BF16),
        grid=(B, P),
        in_specs=in_specs,
        out_specs=pl.BlockSpec((1, S, LANES), lambda b, p: (b, 0, p)),
        compiler_params=_cparams(("parallel", "parallel")),
        name="causal_split" if split_heads else "causal_shared",
    )(*args)


def _dilated_kernel(q_ref, k_ref, v_ref, o_ref, acc_ref, m_ref, l_ref, *, seq, n, dilations):
    lane = lax.broadcasted_iota(jnp.int32, (1, 1, LANES), 2)
    lo_half = lane < HEAD_DIM
    row = lax.broadcasted_iota(jnp.int32, (n, n), 0)
    col = lax.broadcasted_iota(jnp.int32, (n, n), 1)
    own_keep = (col <= row)[None]
    prev_keep = (col >= row)[None]

    nblk = seq // n
    for pi, d in enumerate(dilations):
        L = seq // d
        nb = L // n

        def load(ref, d=d, L=L, nb=nb):
            if d == 1:
                return ref[0].reshape(nblk, n, LANES).astype(BF16)
            parts = [ref[0, pl.ds(r, L, stride=d), :].reshape(nb, n, LANES) for r in range(d)]
            return jnp.concatenate(parts, axis=0).astype(BF16)

        qb, kb, vb = load(q_ref), load(k_ref), load(v_ref)
        if nb > 1:
            zero = jnp.zeros((1, n, LANES), BF16)
            kp = jnp.concatenate([zero, kb[:nblk - 1]], axis=0)
            vp = jnp.concatenate([zero, vb[:nblk - 1]], axis=0)
            blk = lax.broadcasted_iota(jnp.int32, (nblk, n, n), 0)
            prev_ok = jnp.logical_and(prev_keep, lax.rem(blk, nb) != 0)
        zq = jnp.zeros_like(qb)
        heads = []
        for hh in range(2):
            qh = jnp.where(lo_half, qb, zq) if hh == 0 else jnp.where(lo_half, zq, qb)
            s_own = jnp.einsum("bqd,bkd->bqk", qh, kb, preferred_element_type=F32)
            s_own = jnp.where(own_keep, s_own, NEG_BIG)
            m = jnp.max(s_own, axis=-1, keepdims=True)
            if nb > 1:
                s_prev = jnp.einsum("bqd,bkd->bqk", qh, kp, preferred_element_type=F32)
                s_prev = jnp.where(prev_ok, s_prev, NEG_BIG)
                m = jnp.maximum(m, jnp.max(s_prev, axis=-1, keepdims=True))
            p_own = jnp.exp2(s_own - m)
            l = jnp.sum(p_own, axis=-1, keepdims=True)
            pv = jnp.einsum("bqk,bkd->bqd", p_own.astype(BF16), vb, preferred_element_type=F32)
            if nb > 1:
                p_prev = jnp.exp2(s_prev - m)
                l = l + jnp.sum(p_prev, axis=-1, keepdims=True)
                pv = pv + jnp.einsum("bqk,bkd->bqd", p_prev.astype(BF16), vp, preferred_element_type=F32)
            heads.append((m, l, pv))
        m_f = jnp.where(lo_half, heads[0][0], heads[1][0])
        l_f = jnp.where(lo_half, heads[0][1], heads[1][1])
        a_f = jnp.where(lo_half, heads[0][2], heads[1][2])
        for dst, val in ((acc_ref, a_f), (m_ref, m_f), (l_ref, l_f)):
            if d == 1:
                dst[pi] = val.reshape(seq, LANES)
            else:
                for r in range(d):
                    dst[pi, pl.ds(r, L, stride=d), :] = val[r * nb:(r + 1) * nb].reshape(L, LANES)

    np_ = len(dilations)
    m_all = m_ref[0]
    for pi in range(1, np_):
        m_all = jnp.maximum(m_all, m_ref[pi])
    num = jnp.zeros((seq, LANES), F32)
    den = jnp.zeros((seq, LANES), F32)
    for pi in range(np_):
        w = jnp.exp2(m_ref[pi] - m_all)
        num = num + w * acc_ref[pi]
        den = den + w * l_ref[pi]
    o_ref[0] = num / den


def _dilated_attention(qc, kc, vc, seq):
    B = qc.shape[0]
    n = DILATED_PATTERNS[0][0] // DILATED_PATTERNS[0][1]
    dil = tuple(d for _, d in DILATED_PATTERNS)
    for w, d in DILATED_PATTERNS:
        assert w // d == n and seq % (d * n) == 0
    spec = pl.BlockSpec((1, seq, LANES), lambda b, p: (b, 0, p))
    np_ = len(dil)
    return pl.pallas_call(
        functools.partial(_dilated_kernel, seq=seq, n=n, dilations=dil),
        out_shape=jax.ShapeDtypeStruct((B, seq, DIL_WIDTH), F32),
        grid=(B, DIL_WIDTH // LANES),
        in_specs=[spec, spec, spec],
        out_specs=spec,
        scratch_shapes=[pltpu.VMEM((np_, seq, LANES), F32)] * 3,
        compiler_params=_cparams(("parallel", "parallel")),
        name="dilated",
    )(qc, kc, vc)


def _memkv_kernel(mem_ref, g_ref, w_ref, gk_ref, bd64_ref, k_ref, v_ref):
    mn = _rms_rows(mem_ref[0], g_ref[...]).astype(BF16)
    kv = _dot(mn, w_ref[...])
    k = jnp.concatenate(_group_norm_tiles(kv[:, :MEM_WIDTH], bd64_ref[...], 1.0 / HEAD_DIM), axis=1)
    k_ref[0] = (k * gk_ref[...]).astype(BF16)
    v_ref[0] = kv[:, MEM_WIDTH:].astype(BF16)


def _memkv(mem, prm, tabs):
    B, M, D = mem.shape
    return pl.pallas_call(
        _memkv_kernel,
        out_shape=[jax.ShapeDtypeStruct((B, M, MEM_WIDTH), BF16)] * 2,
        grid=(B,),
        in_specs=[pl.BlockSpec((1, M, D), lambda b: (b, 0, 0)), _full((1, D)), _full((D, 2 * MEM_WIDTH)),
                  _full((1, MEM_WIDTH)), _full((LANES, LANES))],
        out_specs=[pl.BlockSpec((1, M, MEM_WIDTH), lambda b: (b, 0, 0))] * 2,
        compiler_params=_cparams(("parallel",)),
        name="memkv",
    )(mem, prm["g_mem"], prm["w_mkv"], prm["g_km"], tabs["bd64"])


def _postattn_kernel(x_ref, oa_ref, ob_ref, oc_ref, gg_ref, wout_ref, gx_ref, wmq_ref, gqm_ref, bd64_ref,
                     km_ref, vm_ref, wmo_ref, gf_ref, wrh_ref, wrl_ref, br_ref, tri_ref,
                     x2_ref, ti_ref, tw_ref, rk_ref, cnt_ref, carry_ref, *, n_experts):
    i = pl.program_id(0)
    tm = x_ref.shape[0]
    hm = tm // POSTATTN_SPLIT
    a0, a1 = FOX_WIDTH, FOX_WIDTH + MLA_WIDTH
    lane = lax.broadcasted_iota(jnp.int32, (hm, LANES), 1)
    lane_f = lane.astype(F32)
    lo_half = lane < HEAD_DIM
    tri = tri_ref[...]

    @pl.when(i == 0)
    def _():
        carry_ref[...] = jnp.zeros_like(carry_ref)

    carry = {"n": carry_ref[...]}

    def rows(g):
        rs = slice(g * hm, (g + 1) * hm)
        ma = _rms_rows(oa_ref[rs, :].astype(F32), gg_ref[:, :a0]).astype(BF16)
        mb = _rms_rows(ob_ref[rs, :].astype(F32), gg_ref[:, a0:a1]).astype(BF16)
        mc = _rms_rows(oc_ref[rs, :], gg_ref[:, a1:]).astype(BF16)
        yield
        x1 = (x_ref[rs, :] + _dot(ma, wout_ref[:a0, :]) + _dot(mb, wout_ref[a0:a1, :])
              + _dot(mc, wout_ref[a1:, :]))
        yield
        h2 = _rms_rows(x1, gx_ref[...]).astype(BF16)
        yield
        qm = _dot(h2, wmq_ref[...])
        yield
        qm_t = _group_norm_tiles(qm, bd64_ref[...], 1.0 / HEAD_DIM)
        outs = []
        for p in range(MEM_WIDTH // LANES):
            sl = slice(p * LANES, (p + 1) * LANES)
            q2 = (qm_t[p] * gqm_ref[:, sl]).astype(BF16)
            k2 = km_ref[0, :, sl]
            v2 = vm_ref[0, :, sl]
            zero = jnp.zeros_like(q2)
            res = []
            for hh in range(2):
                qh = jnp.where(lo_half, q2, zero) if hh == 0 else jnp.where(lo_half, zero, q2)
                s = _dot_nt(qh, k2)
                pr = jnp.exp2(s - jnp.max(s, axis=-1, keepdims=True))
                l = jnp.sum(pr, axis=-1, keepdims=True)
                res.append(_dot(pr.astype(BF16), v2) / l)
            outs.append(jnp.where(lo_half, res[0], res[1]))
            yield
        om = jnp.concatenate(outs, axis=1).astype(BF16)
        x2 = x1 + _dot(om, wmo_ref[...])
        x2_ref[rs, :] = x2
        yield
        h3 = _rms_rows(x2, gf_ref[...])
        h3_hi = h3.astype(BF16)
        h3_lo = (h3 - h3_hi.astype(F32)).astype(BF16)
        yield
        logits = _dot(h3_hi, wrh_ref[...]) + _dot(h3_hi, wrl_ref[...]) + _dot(h3_lo, wrh_ref[...]) + br_ref[...]
        yield
        logits = jnp.where(lane < n_experts, logits, NEG_BIG)
        work = logits
        sel = jnp.zeros((hm, LANES), F32)
        vals, idxs = [], []
        for _ in range(TOP_K):
            mv = jnp.max(work, axis=-1, keepdims=True)
            ix = jnp.min(jnp.where(work == mv, lane_f, float(LANES)), axis=-1, keepdims=True)
            hit = lane_f == ix
            work = jnp.where(hit, NEG_BIG, work)
            sel = jnp.where(hit, 1.0, sel)
            vals.append(mv)
            idxs.append(ix)
        es = [jnp.exp(v - vals[0]) for v in vals]
        den = es[0] + es[1] + es[2] + es[3]
        yield
        before = _dot(tri, sel.astype(BF16))
        yield
        before = before + carry["n"]
        carry["n"] = before[hm - 1:, :] + sel[hm - 1:, :]
        ti = jnp.zeros((hm, LANES), F32)
        tw = jnp.zeros((hm, LANES), F32)
        rk = jnp.zeros((hm, LANES), F32)
        for kk in range(TOP_K):
            at = lane == kk
            ti = jnp.where(at, idxs[kk], ti)
            tw = jnp.where(at, es[kk] / den, tw)
            r_k = jnp.sum(jnp.where(lane_f == idxs[kk], before, 0.0), axis=-1, keepdims=True)
            rk = jnp.where(at, r_k, rk)
        ti_ref[rs, :] = ti.astype(jnp.int32)
        tw_ref[rs, :] = tw
        rk_ref[rs, :] = rk.astype(jnp.int32)

    _interleave([rows(g) for g in range(POSTATTN_SPLIT)])
    carry_ref[...] = carry["n"]
    cnt_ref[...] = carry["n"]


def _postattn(x2d, oa, ob, oc, kmem, vmem, prm, tabs, seq, tm, n_experts):
    T, D = x2d.shape
    M = kmem.shape[1]
    tpb = seq // tm
    hm = tm // POSTATTN_SPLIT
    row = lambda w: pl.BlockSpec((tm, w), lambda i: (i, 0))
    memspec = pl.BlockSpec((1, M, MEM_WIDTH), lambda i: (i // tpb, 0, 0))
    mix = FOX_WIDTH + MLA_WIDTH + DIL_WIDTH
    return pl.pallas_call(
        functools.partial(_postattn_kernel, n_experts=n_experts),
        out_shape=[jax.ShapeDtypeStruct((T, D), F32),
                   jax.ShapeDtypeStruct((T, LANES), jnp.int32), jax.ShapeDtypeStruct((T, LANES), F32),
                   jax.ShapeDtypeStruct((T, LANES), jnp.int32), jax.ShapeDtypeStruct((1, LANES), F32)],
        grid=(T // tm,),
        in_specs=[row(D), row(FOX_WIDTH), row(MLA_WIDTH), row(DIL_WIDTH), _full((1, mix)), _full((mix, D)),
                  _full((1, D)), _full((D, MEM_WIDTH)), _full((1, MEM_WIDTH)), _full((LANES, LANES)),
                  memspec, memspec, _full((MEM_WIDTH, D)), _full((1, D)),
                  _full((D, LANES)), _full((D, LANES)), _full((1, LANES)), _full((hm, hm))],
        out_specs=[row(D), row(LANES), row(LANES), row(LANES), _full((1, LANES))],
        scratch_shapes=[pltpu.VMEM((1, LANES), F32)],
        compiler_params=_cparams(("arbitrary",)),
        name="postattn",
    )(x2d, oa, ob, oc, prm["g_group"], prm["w_out"], prm["g_xmem"], prm["w_mq"], prm["g_qm"], tabs["bd64"],
      kmem, vmem, prm["w_mo"], prm["g_ffn"], prm["w_r_hi"], prm["w_r_lo"], prm["b_router"], tabs["tri_strict"])


def _row_copy(src_ref, src_row, dst_ref, dst_row, sem):
    return pltpu.make_async_copy(src_ref.at[pl.ds(src_row, 1), :], dst_ref.at[pl.ds(dst_row, 1), :], sem)


def _dispatch_kernel(fill_ref, pos_ref, x_ref, xs_ref, zero_ref, sem, zsem, *, tm, tr):
    @pl.when(pl.program_id(0) == 0)
    def _():
        n_experts = fill_ref.shape[0] - 1
        n_rows = xs_ref.shape[0]
        zero_ref[...] = jnp.zeros_like(zero_ref)
        starts = [(fill_ref[e], fill_ref[e] >= 0) for e in range(n_experts)]
        for j in range(n_experts + 1):
            start = fill_ref[n_experts] + j * tr
            starts.append((start, start < n_rows))

        def fill(start):
            return pltpu.make_async_copy(zero_ref, xs_ref.at[pl.ds(pl.multiple_of(start, tr), tr), :], zsem)

        for start, live in starts:
            @pl.when(live)
            def _(start=start):
                fill(start).start()
        for start, live in starts:
            @pl.when(live)
            def _(start=start):
                fill(start).wait()

    def issue(r, _):
        for kk in range(TOP_K):
            _row_copy(x_ref, r, xs_ref, pos_ref[0, 0, r * TOP_K + kk], sem).start(priority=kk % 2)
        return 0
    lax.fori_loop(0, tm, issue, 0, unroll=ROW_DMA_UNROLL)
    n = tm * TOP_K
    pltpu.make_async_copy(xs_ref.at[pl.ds(0, n), :], xs_ref.at[pl.ds(0, n), :], sem).wait()


def _dispatch(x2, pos, pad_start, n_rows, tm, tr):
    T, D = x2.shape
    return pl.pallas_call(
        functools.partial(_dispatch_kernel, tm=tm, tr=tr),
        out_shape=jax.ShapeDtypeStruct((n_rows, D), x2.dtype),
        grid_spec=pltpu.PrefetchScalarGridSpec(
            num_scalar_prefetch=1,
            grid=(T // tm,),
            in_specs=[pl.BlockSpec((1, 1, tm * TOP_K), lambda i, pad: (i, 0, 0), memory_space=pltpu.SMEM),
                      pl.BlockSpec((tm, D), lambda i, pad: (i, 0))],
            out_specs=pl.BlockSpec(memory_space=pl.ANY),
            scratch_shapes=[pltpu.VMEM((tr, D), x2.dtype), pltpu.SemaphoreType.DMA, pltpu.SemaphoreType.DMA],
        ),
        compiler_params=_cparams(("arbitrary",)),
        name="dispatch",
    )(pad_start, pos.reshape(T // tm, 1, tm * TOP_K), x2)


def _experts_kernel(te_ref, nu_ref, xs_ref, gf_ref, w1_ref, b1g_ref, b1l_ref, w2_ref, b2_ref, perm_ref,
                    ys_ref, w1s_ref, w2s_ref):
    i = pl.program_id(0)
    two_f = w1_ref.shape[3]
    n_blk = two_f // DEINT_BLOCK
    half = DEINT_BLOCK // 2

    @pl.when(i < nu_ref[0])
    def _():
        @pl.when(jnp.logical_or(i == 0, te_ref[i] != te_ref[jnp.maximum(i - 1, 0)]))
        def _():
            perm = perm_ref[...]
            for b in range(n_blk):
                sl = slice(b * DEINT_BLOCK, (b + 1) * DEINT_BLOCK)
                w1s_ref[:, sl] = _dot(w1_ref[0, 0, :, sl].astype(BF16), perm).astype(BF16)
            w2s_ref[...] = w2_ref[0, 0].astype(BF16)

        x = _rms_rows(xs_ref[...], gf_ref[...]).astype(BF16)
        h = _dot(x, w1s_ref[...])
        acts = []
        for b in range(n_blk):
            glu = h[:, b * DEINT_BLOCK:b * DEINT_BLOCK + half] + b1g_ref[0, :, b * half:(b + 1) * half]
            lin = h[:, b * DEINT_BLOCK + half:(b + 1) * DEINT_BLOCK] + b1l_ref[0, :, b * half:(b + 1) * half]
            glu = jnp.minimum(glu, SWIGLU_LIMIT)
            lin = jnp.clip(lin, -SWIGLU_LIMIT, SWIGLU_LIMIT)
            acts.append((glu * (1.0 / (1.0 + jnp.exp(-SWIGLU_ALPHA * glu))) * (lin + 1.0)).astype(BF16))
        act = jnp.concatenate(acts, axis=1) if n_blk > 1 else acts[0]
        ys_ref[...] = _dot(act, w2s_ref[...]) + b2_ref[0]

    @pl.when(i >= nu_ref[0])
    def _():
        ys_ref[...] = jnp.zeros_like(ys_ref)


def _experts(tile_expert, n_used, xs, w_e1, w_e2, layer, prm, tabs, tr):
    n_rows, D = xs.shape
    two_f = w_e1.shape[3]
    F = two_f // 2
    assert two_f % DEINT_BLOCK == 0
    n_tiles = n_rows // tr
    rows = pl.BlockSpec((tr, D), lambda i, te, nu: (i, 0))
    wspec = lambda s: pl.BlockSpec((1,) + s, lambda i, te, nu: (te[i], 0, 0))
    lspec = lambda s: pl.BlockSpec((1, 1) + s, lambda i, te, nu: (layer, te[i], 0, 0))
    return pl.pallas_call(
        _experts_kernel,
        out_shape=jax.ShapeDtypeStruct((n_rows, D), F32),
        grid_spec=pltpu.PrefetchScalarGridSpec(
            num_scalar_prefetch=2,
            grid=(n_tiles,),
            in_specs=[rows, pl.BlockSpec((1, D), lambda i, te, nu: (0, 0)),
                      lspec((D, two_f)), wspec((1, F)), wspec((1, F)), lspec((F, D)), wspec((1, D)),
                      pl.BlockSpec((DEINT_BLOCK, DEINT_BLOCK), lambda i, te, nu: (0, 0))],
            out_specs=rows,
            scratch_shapes=[pltpu.VMEM((D, two_f), BF16), pltpu.VMEM((F, D), BF16)],
        ),
        compiler_params=_cparams(("arbitrary",)),
        name="experts",
    )(tile_expert, n_used, xs, prm["g_ffn"], w_e1, prm["b1g"], prm["b1l"], w_e2, prm["b2"], tabs["deint"])


def _combine_kernel(pos_ref, x_ref, tw_ref, ys_ref, o_ref, buf, sem, *, tm):
    def issue(r, _):
        for kk in range(TOP_K):
            _row_copy(ys_ref, pos_ref[0, 0, r * TOP_K + kk], buf.at[kk], r, sem).start(priority=kk % 2)
        return 0
    lax.fori_loop(0, tm, issue, 0, unroll=ROW_DMA_UNROLL)
    pltpu.make_async_copy(buf, buf, sem).wait()

    tw = tw_ref[...]
    acc = x_ref[...]
    for kk in range(TOP_K):
        acc = acc + tw[:, kk:kk + 1] * buf[kk]
    o_ref[...] = acc


def _combine(x2, tw, ys, pos, tm):
    T, D = x2.shape
    W = ys.shape[1]
    return pl.pallas_call(
        functools.partial(_combine_kernel, tm=tm),
        out_shape=jax.ShapeDtypeStruct((T, D), F32),
        grid=(T // tm,),
        in_specs=[pl.BlockSpec((1, 1, tm * TOP_K), lambda i: (i, 0, 0), memory_space=pltpu.SMEM),
                  pl.BlockSpec((tm, D), lambda i: (i, 0)),
                  pl.BlockSpec((tm, LANES), lambda i: (i, 0)),
                  pl.BlockSpec(memory_space=pl.ANY)],
        out_specs=pl.BlockSpec((tm, D), lambda i: (i, 0)),
        scratch_shapes=[pltpu.VMEM((TOP_K, tm, W), ys.dtype), pltpu.SemaphoreType.DMA],
        compiler_params=_cparams(("arbitrary",)),
        name="combine",
    )(pos.reshape(T // tm, 1, tm * TOP_K), x2, tw, ys)


def _tile_gain(g, reps, scale=1.0):
    return (jnp.tile(g.astype(F32), reps) * scale)[None, :]


def _prep_layer(l, D, w_in, b_forget, g_mix, g_qa, g_ka, g_cq, g_ckv, w_uq, w_ukv, g_qb, g_kb, g_qc, g_kc,
                g_group, w_out, g_xmem, g_mem, w_mq, w_mkv, g_qm, g_km, w_mo, g_ffn, w_router, b_router,
                w_e1, b_e1, w_e2, b_e2):
    wi = w_in[l]
    sizes = (FOX_WIDTH, FOX_WIDTH, FOX_WIDTH, FOX_HEADS, MLA_Q_RANK, MLA_KV_RANK, MLA_ROPE_DIM,
             DIL_WIDTH, DIL_WIDTH, DIL_WIDTH)
    pts = np.cumsum((0,) + sizes)
    qa, ka, va, fa, cq, ckv, kr, qc, kc, vc = [wi[:, pts[j]:pts[j + 1]] for j in range(len(sizes))]
    misc = jnp.zeros((D, LANES), F32).at[:, :FOX_HEADS].set(fa).at[:, 32:32 + MLA_ROPE_DIM].set(kr)
    w_main = jnp.concatenate([qa, ka, va, cq, ckv, qc, kc, vc, misc], axis=1).astype(BF16)
    b_misc = jnp.zeros((1, LANES), F32).at[0, :FOX_HEADS].set(b_forget[l])

    uq = w_uq[l].reshape(MLA_Q_RANK, MLA_HEADS, MLA_QK_DIM)
    uq = jnp.pad(uq, ((0, 0), (0, 0), (0, MLA_PAD - MLA_QK_DIM))).reshape(MLA_Q_RANK, MLA_HEADS * MLA_PAD)
    ukv = w_ukv[l].reshape(MLA_KV_RANK, MLA_HEADS, MLA_NOPE_DIM + MLA_V_DIM)
    uk = jnp.pad(ukv[:, :, :MLA_NOPE_DIM], ((0, 0), (0, 0), (0, MLA_PAD - MLA_NOPE_DIM)))
    uk = uk.reshape(MLA_KV_RANK, MLA_HEADS * MLA_PAD)
    uv = ukv[:, :, MLA_NOPE_DIM:].reshape(MLA_KV_RANK, MLA_WIDTH)
    pad_gain = lambda g: jnp.pad(g.astype(F32), (0, MLA_PAD - MLA_QK_DIM))

    E = w_e1.shape[1]
    wr = jnp.zeros((D, LANES), F32).at[:, :E].set(w_router[l])
    wr_hi = wr.astype(BF16)
    wr_lo = (wr - wr_hi.astype(F32)).astype(BF16)
    q_scale = lambda dim: dim ** -0.5 * LOG2E
    return dict(
        g_mix=g_mix[l][None], w_main=w_main, b_misc=b_misc,
        g_qa=_tile_gain(g_qa[l], FOX_HEADS, q_scale(HEAD_DIM)), g_ka=_tile_gain(g_ka[l], FOX_HEADS),
        g_cq=g_cq[l][None], g_ckv=g_ckv[l][None],
        w_uq=uq.astype(BF16), w_uk=uk.astype(BF16), w_uv=uv.astype(BF16),
        g_qb=_tile_gain(pad_gain(g_qb[l]), MLA_HEADS, q_scale(MLA_QK_DIM)),
        g_kb=_tile_gain(pad_gain(g_kb[l]), MLA_HEADS),
        g_qc=_tile_gain(g_qc[l], DIL_HEADS, q_scale(HEAD_DIM)), g_kc=_tile_gain(g_kc[l], DIL_HEADS),
        g_group=g_group[l][None], w_out=w_out[l].astype(BF16),
        g_xmem=g_xmem[l][None], g_mem=g_mem[l][None],
        w_mq=w_mq[l].astype(BF16), w_mkv=w_mkv[l].astype(BF16),
        g_qm=_tile_gain(g_qm[l], MEM_HEADS, q_scale(HEAD_DIM)), g_km=_tile_gain(g_km[l], MEM_HEADS),
        w_mo=w_mo[l].astype(BF16), g_ffn=g_ffn[l][None],
        w_r_hi=wr_hi, w_r_lo=wr_lo,
        b_router=jnp.zeros((1, LANES), F32).at[0, :E].set(b_router[l]),
        b1g=b_e1[l][:, None, 0::2], b1l=b_e1[l][:, None, 1::2], b2=b_e2[l][:, None, :],
    )


def _rope_lane_tables(positions, rot_dim, period, offset):
    half = rot_dim // 2
    inv = ROPE_THETA ** (-jnp.arange(0, rot_dim, 2, dtype=F32) / rot_dim)
    ang = positions.astype(F32).reshape(-1)[:, None] * inv
    cos, sin = jnp.cos(ang), jnp.sin(ang)
    lane = np.arange(LANES) % period - offset
    idx = np.clip(lane % half if half else lane, 0, half - 1)
    first = (lane >= 0) & (lane < half)
    second = (lane >= half) & (lane < rot_dim)
    c = jnp.where(first | second, cos[:, idx], 1.0)
    s_dn = jnp.where(second, sin[:, idx], 0.0)
    s_up = jnp.where(first, -sin[:, idx], 0.0)
    return c, s_dn, s_up


def _tables(positions, n_incl, n_strict):
    ri, rs = np.arange(n_incl), np.arange(n_strict)
    bd = (np.arange(LANES)[:, None] // HEAD_DIM) == (np.arange(LANES)[None, :] // HEAD_DIM)
    rep = np.zeros((LANES, MLA_HEADS * MLA_PAD), np.float32)
    for h in range(MLA_HEADS):
        for j in range(MLA_ROPE_DIM):
            rep[32 + j, h * MLA_PAD + MLA_NOPE_DIM + j] = 1.0
    half = DEINT_BLOCK // 2
    deint = np.zeros((DEINT_BLOCK, DEINT_BLOCK), np.float32)
    deint[2 * np.arange(half), np.arange(half)] = 1.0
    deint[2 * np.arange(half) + 1, half + np.arange(half)] = 1.0
    cm, sdm, sum_ = _rope_lane_tables(positions, MLA_ROPE_DIM, LANES, MLA_NOPE_DIM)
    cd, sdd, sud = _rope_lane_tables(positions, ROT_DIM, HEAD_DIM, 0)
    return dict(
        tri_incl=jnp.asarray(ri[:, None] >= ri[None, :], BF16),
        tri_strict=jnp.asarray(rs[:, None] > rs[None, :], BF16),
        bd64=jnp.asarray(bd, BF16), ones=jnp.ones((LANES, LANES), BF16), rep=jnp.asarray(rep, BF16),
        deint=jnp.asarray(deint, BF16),
        cm=cm, sdm=sdm, sum=sum_, cd=cd, sdd=sdd, sud=sud,
    )


def _routing_tables(ti, rk, cnt, n_experts, tr, n_tiles):
    counts = cnt[0, :n_experts].astype(jnp.int32)
    padded = ((counts + tr - 1) // tr) * tr
    ends = jnp.cumsum(padded)
    starts = ends - padded
    idx = ti[:, :TOP_K]
    hit = idx[:, :, None] == jnp.arange(n_experts, dtype=jnp.int32)
    pos = rk[:, :TOP_K] + jnp.sum(jnp.where(hit, starts, 0), axis=-1)
    tile_start = jnp.arange(n_tiles, dtype=jnp.int32) * tr
    tile_expert = jnp.sum((ends[None, :] <= tile_start[:, None]).astype(jnp.int32), axis=1)
    tile_expert = jnp.minimum(tile_expert, n_experts - 1)
    n_used = (ends[n_experts - 1:] // tr).astype(jnp.int32)
    pad_fill = jnp.where(padded > 0, ends - tr, -1)
    fill_start = jnp.concatenate([pad_fill, ends[n_experts - 1:]])
    return pos.astype(jnp.int32), tile_expert.astype(jnp.int32), fill_start.astype(jnp.int32), n_used


def _pick_tile(seq, want):
    t = min(want, seq)
    while seq % t:
        t //= 2
    return t


def kernel(x, mem, positions, g_mix, w_in, b_forget, g_qa, g_ka, g_cq, g_ckv, w_uq, w_ukv, g_qb, g_kb, g_qc, g_kc, g_group, w_out, g_xmem, g_mem, w_mq, w_mkv, g_qm, g_km, w_mo, g_ffn, w_router, b_router, w_e1, b_e1, w_e2, b_e2):
    B, S, D = x.shape
    T = B * S
    depth = w_in.shape[0]
    E = w_e1.shape[1]
    tm_in = _pick_tile(S, INPROJ_TILE)
    tm_post = _pick_tile(S, POSTATTN_TILE)
    tq = _pick_tile(S, ATTN_TILE)
    tr = EXPERT_ROW_TILE
    tmoe = _pick_tile(S, MOE_TOKEN_TILE)
    n_rows = T * TOP_K + (E + 1) * tr
    n_tiles = n_rows // tr

    tabs = _tables(positions, tm_in // INPROJ_SPLIT, tm_post // POSTATTN_SPLIT)
    x2d = x.reshape(T, D)
    for l in range(depth):
        prm = _prep_layer(l, D, w_in, b_forget, g_mix, g_qa, g_ka, g_cq, g_ckv, w_uq, w_ukv, g_qb, g_kb,
                          g_qc, g_kc, g_group, w_out, g_xmem, g_mem, w_mq, w_mkv, g_qm, g_km, w_mo, g_ffn,
                          w_router, b_router, w_e1, b_e1, w_e2, b_e2)
        qa, ka, va, csum, qb, kb, vb, qc, kc, vc = _inproj(x2d, prm, tabs, S, tm_in)
        kbias = -LOG2E * csum[:, :FOX_HEADS].reshape(B, S, FOX_HEADS // 2, 2).transpose(0, 2, 3, 1)
        r3 = lambda a: a.reshape(B, S, a.shape[-1])
        oa = _causal_attention(r3(qa), r3(ka), r3(va), kbias, tq=tq, split_heads=False)
        ob = _causal_attention(r3(qb), r3(kb), r3(vb), None, tq=tq, split_heads=True)
        oc = _dilated_attention(r3(qc), r3(kc), r3(vc), S)
        kmem, vmem = _memkv(mem, prm, tabs)
        x2, ti, tw, rk, cnt = _postattn(x2d, oa.reshape(T, -1), ob.reshape(T, -1), oc.reshape(T, -1),
                                        kmem, vmem, prm, tabs, S, tm_post, E)
        pos, tile_expert, pad_start, n_used = _routing_tables(ti, rk, cnt, E, tr, n_tiles)
        xs = _dispatch(x2, pos, pad_start, n_rows, tmoe, tr)
        ys = _experts(tile_expert, n_used, xs, w_e1, w_e2, l, prm, tabs, tr)
        x2d = _combine(x2, tw, ys, pos, tmoe)
    return x2d.reshape(B, S, D)
```

```python
import functools
import math

import jax
import jax.numpy as jnp
import numpy as np
from jax import lax
from jax.experimental import pallas as pl
from jax.experimental.pallas import tpu as pltpu

HEAD_DIM = 64
FOX_HEADS = 4
MLA_HEADS = 6
DIL_HEADS = 6
MLA_Q_RANK = 384
MLA_KV_RANK = 256
MLA_NOPE_DIM = 64
MLA_ROPE_DIM = 32
MLA_QK_DIM = MLA_NOPE_DIM + MLA_ROPE_DIM
MLA_V_DIM = 64
ROT_DIM = HEAD_DIM // 4
ROPE_THETA = 500000.0
DILATED_PATTERNS = ((128, 1), (512, 4), (2048, 16))
FOX_WIDTH = FOX_HEADS * HEAD_DIM
MLA_WIDTH = MLA_HEADS * MLA_V_DIM
DIL_WIDTH = DIL_HEADS * HEAD_DIM
MEM_HEADS = 4
MEM_WIDTH = MEM_HEADS * HEAD_DIM
TOP_K = 4
SWIGLU_ALPHA = 1.702
SWIGLU_LIMIT = 7.0
RMS_EPS = 1e-6
LOG2E = math.log2(math.e)

LANES = 128
VMEM_LIMIT_BYTES = 56 * 1024 * 1024

INPROJ_TILE = 512
INPROJ_SPLIT = 1
POSTATTN_TILE = 1024
POSTATTN_SPLIT = 2
ATTN_TILE = 512
EXPERT_ROW_TILE = 512
MOE_TOKEN_TILE = 1024
ROW_DMA_UNROLL = 8

NEG_BIG = -1e30
MLA_PAD = 128
DEINT_BLOCK = 256

F32 = jnp.float32
BF16 = jnp.bfloat16


def _cparams(sem):
    return pltpu.CompilerParams(dimension_semantics=sem, vmem_limit_bytes=VMEM_LIMIT_BYTES)


def _full(shape):
    nd = len(shape)
    return pl.BlockSpec(shape, lambda *_: (0,) * nd)


def _dot(a, b):
    return jnp.dot(a, b, preferred_element_type=F32)


def _dot_nt(a, b):
    return lax.dot_general(a, b, (((1,), (1,)), ((), ())), preferred_element_type=F32)


def _rms_rows(x, g):
    return x * lax.rsqrt(jnp.mean(x * x, axis=-1, keepdims=True) + RMS_EPS) * g


def _group_norm_tiles(y, ones_mat, inv_n):
    outs = []
    for j in range(y.shape[1] // LANES):
        yj = y[:, j * LANES:(j + 1) * LANES]
        ss = _dot((yj * yj).astype(BF16), ones_mat)
        outs.append(yj * lax.rsqrt(ss * inv_n + RMS_EPS))
    return outs


def _rope_tile(y, half, c, s_dn, s_up):
    return y * c + pltpu.roll(y, half, 1) * s_dn + pltpu.roll(y, LANES - half, 1) * s_up


def _interleave(gens):
    live = list(gens)
    while live:
        for g in list(live):
            try:
                next(g)
            except StopIteration:
                live.remove(g)


def _inproj_kernel(x_ref, gmix_ref, w_ref, bmisc_ref, tri_ref, bd64_ref, ones_ref,
                   gqa_ref, gka_ref, gcq_ref, gckv_ref, wuq_ref, wuk_ref, wuv_ref, rep_ref,
                   gqb_ref, gkb_ref, gqc_ref, gkc_ref,
                   cm_ref, sdm_ref, sum_ref, cd_ref, sdd_ref, sud_ref,
                   qa_ref, ka_ref, va_ref, c_ref, qb_ref, kb_ref, vb_ref, qc_ref, kc_ref, vc_ref,
                   carry_ref, *, tiles_per_batch):
    i = pl.program_id(0)
    tm = x_ref.shape[0]
    hm = tm // INPROJ_SPLIT
    bd64 = bd64_ref[...]
    ones = ones_ref[...]
    tri = tri_ref[...]
    half_b = MLA_ROPE_DIM // 2
    half_c = ROT_DIM // 2
    offs = np.cumsum((0, FOX_WIDTH, FOX_WIDTH, FOX_WIDTH, MLA_Q_RANK, MLA_KV_RANK,
                      DIL_WIDTH, DIL_WIDTH, DIL_WIDTH, LANES))

    @pl.when(i % tiles_per_batch == 0)
    def _():
        carry_ref[...] = jnp.zeros_like(carry_ref)

    carry = {"c": carry_ref[...]}

    def rows(g):
        rs = slice(g * hm, (g + 1) * hm)
        h = _rms_rows(x_ref[rs, :], gmix_ref[...]).astype(BF16)
        seg = lambda j: _dot(h, w_ref[:, offs[j]:offs[j + 1]])
        yield
        qa, ka, va, cq = seg(0), seg(1), seg(2), seg(3)
        yield
        qa_ref[rs, :] = (jnp.concatenate(_group_norm_tiles(qa, bd64, 1.0 / HEAD_DIM), axis=1)
                         * gqa_ref[...]).astype(BF16)
        ka_ref[rs, :] = (jnp.concatenate(_group_norm_tiles(ka, bd64, 1.0 / HEAD_DIM), axis=1)
                         * gka_ref[...]).astype(BF16)
        va_ref[rs, :] = va.astype(BF16)
        cqn = _rms_rows(cq, gcq_ref[...]).astype(BF16)
        yield
        ckv, qc = seg(4), seg(5)
        qb_pre = _dot(cqn, wuq_ref[...])
        yield
        ckvn = _rms_rows(ckv, gckv_ref[...]).astype(BF16)
        cd, sdd, sud = cd_ref[rs, :], sdd_ref[rs, :], sud_ref[rs, :]
        qc_t = _group_norm_tiles(qc, bd64, 1.0 / HEAD_DIM)
        for j in range(DIL_WIDTH // LANES):
            sl = slice(j * LANES, (j + 1) * LANES)
            qc_ref[rs, sl] = _rope_tile(qc_t[j] * gqc_ref[:, sl], half_c, cd, sdd, sud)
        yield
        kc, vc, misc = seg(6), seg(7), seg(8)
        kb_nope = _dot(ckvn, wuk_ref[...])
        vb_ref[rs, :] = _dot(ckvn, wuv_ref[...]).astype(BF16)
        yield
        cm, sdm, sum_ = cm_ref[rs, :], sdm_ref[rs, :], sum_ref[rs, :]
        qb_t = _group_norm_tiles(qb_pre, ones, 1.0 / MLA_QK_DIM)
        for j in range(MLA_HEADS):
            sl = slice(j * LANES, (j + 1) * LANES)
            qb_ref[rs, sl] = _rope_tile(qb_t[j] * gqb_ref[:, sl], half_b, cm, sdm, sum_).astype(BF16)
        kc_t = _group_norm_tiles(kc, bd64, 1.0 / HEAD_DIM)
        for j in range(DIL_WIDTH // LANES):
            sl = slice(j * LANES, (j + 1) * LANES)
            kc_ref[rs, sl] = _rope_tile(kc_t[j] * gkc_ref[:, sl], half_c, cd, sdd, sud)
        vc_ref[rs, :] = vc
        z = misc + bmisc_ref[...]
        logf = jnp.minimum(z, 0.0) - jnp.log(1.0 + jnp.exp(-jnp.abs(z)))
        lane = lax.broadcasted_iota(jnp.int32, logf.shape, 1)
        logf = jnp.where(lane < FOX_HEADS, logf, 0.0)
        p0 = logf.astype(BF16)
        r0 = logf - p0.astype(F32)
        p1 = r0.astype(BF16)
        p2 = (r0 - p1.astype(F32)).astype(BF16)
        yield
        csum = _dot(tri, p0) + _dot(tri, p1) + _dot(tri, p2)
        kb_pre = kb_nope + _dot(misc.astype(BF16), rep_ref[...])
        yield
        kb_t = _group_norm_tiles(kb_pre, ones, 1.0 / MLA_QK_DIM)
        for j in range(MLA_HEADS):
            sl = slice(j * LANES, (j + 1) * LANES)
            kb_ref[rs, sl] = _rope_tile(kb_t[j] * gkb_ref[:, sl], half_b, cm, sdm, sum_).astype(BF16)
        csum = csum + carry["c"]
        c_ref[rs, :] = csum
        carry["c"] = csum[hm - 1:, :]

    _interleave([rows(g) for g in range(INPROJ_SPLIT)])
    carry_ref[...] = carry["c"]


def _inproj(x2d, prm, tabs, seq, tm):
    T, D = x2d.shape
    nm = prm["w_main"].shape[1]
    hm = tm // INPROJ_SPLIT
    row = lambda w: pl.BlockSpec((tm, w), lambda i: (i, 0))
    in_specs = [row(D), _full((1, D)), _full((D, nm)), _full((1, LANES)), _full((hm, hm)),
                _full((LANES, LANES)), _full((LANES, LANES)),
                _full((1, FOX_WIDTH)), _full((1, FOX_WIDTH)), _full((1, MLA_Q_RANK)), _full((1, MLA_KV_RANK)),
                _full((MLA_Q_RANK, MLA_HEADS * MLA_PAD)), _full((MLA_KV_RANK, MLA_HEADS * MLA_PAD)),
                _full((MLA_KV_RANK, MLA_WIDTH)), _full((LANES, MLA_HEADS * MLA_PAD)),
                _full((1, MLA_HEADS * MLA_PAD)), _full((1, MLA_HEADS * MLA_PAD)),
                _full((1, DIL_WIDTH)), _full((1, DIL_WIDTH))] + [row(LANES)] * 6
    outs = [(FOX_WIDTH, BF16), (FOX_WIDTH, BF16), (FOX_WIDTH, BF16), (LANES, F32),
            (MLA_HEADS * MLA_PAD, BF16), (MLA_HEADS * MLA_PAD, BF16), (MLA_WIDTH, BF16),
            (DIL_WIDTH, F32), (DIL_WIDTH, F32), (DIL_WIDTH, F32)]
    return pl.pallas_call(
        functools.partial(_inproj_kernel, tiles_per_batch=seq // tm),
        out_shape=[jax.ShapeDtypeStruct((T, w), dt) for w, dt in outs],
        grid=(T // tm,),
        in_specs=in_specs,
        out_specs=[row(w) for w, _ in outs],
        scratch_shapes=[pltpu.VMEM((1, LANES), F32)],
        compiler_params=_cparams(("arbitrary",)),
        name="inproj",
    )(x2d, prm["g_mix"], prm["w_main"], prm["b_misc"], tabs["tri_incl"], tabs["bd64"], tabs["ones"],
      prm["g_qa"], prm["g_ka"], prm["g_cq"], prm["g_ckv"], prm["w_uq"], prm["w_uk"], prm["w_uv"], tabs["rep"],
      prm["g_qb"], prm["g_kb"], prm["g_qc"], prm["g_kc"],
      tabs["cm"], tabs["sdm"], tabs["sum"], tabs["cd"], tabs["sdd"], tabs["sud"])


def _causal_kernel(*refs, seq, tq, split_heads, with_bias):
    if with_bias:
        q_ref, k_ref, v_ref, b_ref, o_ref = refs
    else:
        q_ref, k_ref, v_ref, o_ref = refs
        b_ref = None
    lane = lax.broadcasted_iota(jnp.int32, (tq, LANES), 1)
    lo_half = lane < HEAD_DIM
    r = lax.broadcasted_iota(jnp.int32, (tq, tq), 0)
    c = lax.broadcasted_iota(jnp.int32, (tq, tq), 1)
    keep = c <= r
    for qi in range(seq // tq):
        q = q_ref[0, qi * tq:(qi + 1) * tq, :]
        n_keys = (qi + 1) * tq
        v = v_ref[0, :n_keys, :]
        res = []
        for hh in range(2):
            if split_heads:
                qh = q[:, hh * LANES:(hh + 1) * LANES]
                k = k_ref[0, :n_keys, hh * LANES:(hh + 1) * LANES]
            else:
                zero = jnp.zeros_like(q)
                qh = jnp.where(lo_half, q, zero) if hh == 0 else jnp.where(lo_half, zero, q)
                k = k_ref[0, :n_keys, :]
            s = _dot_nt(qh, k)
            if with_bias:
                s = s + b_ref[0, 0, hh:hh + 1, :n_keys]
            s_diag = jnp.where(keep, s[:, n_keys - tq:], NEG_BIG)
            s = jnp.concatenate([s[:, :n_keys - tq], s_diag], axis=1) if qi else s_diag
            p = jnp.exp2(s - jnp.max(s, axis=-1, keepdims=True))
            l = jnp.sum(p, axis=-1, keepdims=True)
            res.append(_dot(p.astype(BF16), v) / l)
        o_ref[0, qi * tq:(qi + 1) * tq, :] = jnp.where(lo_half, res[0], res[1]).astype(o_ref.dtype)


def _causal_attention(q, k, v, kbias, *, tq, split_heads):
    B, S, _ = q.shape
    P = v.shape[2] // LANES
    wq = 2 * LANES if split_heads else LANES
    in_specs = [pl.BlockSpec((1, S, wq), lambda b, p: (b, 0, p)),
                pl.BlockSpec((1, S, wq), lambda b, p: (b, 0, p)),
                pl.BlockSpec((1, S, LANES), lambda b, p: (b, 0, p))]
    args = [q, k, v]
    if kbias is not None:
        in_specs.append(pl.BlockSpec((1, 1, 2, S), lambda b, p: (b, p, 0, 0)))
        args.append(kbias)
    return pl.pallas_call(
        functools.partial(_causal_kernel, seq=S, tq=tq, split_heads=split_heads, with_bias=kbias is not None),
        out_shape=jax.ShapeDtypeStruct((B, S, P * LANES), BF16),
        grid=(B, P),
        in_specs=in_specs,
        out_specs=pl.BlockSpec((1, S, LANES), lambda b, p: (b, 0, p)),
        compiler_params=_cparams(("parallel", "parallel")),
        name="causal_split" if split_heads else "causal_shared",
    )(*args)


def _dilated_kernel(q_ref, k_ref, v_ref, o_ref, acc_ref, m_ref, l_ref, *, seq, n, dilations):
    lane = lax.broadcasted_iota(jnp.int32, (1, 1, LANES), 2)
    lo_half = lane < HEAD_DIM
    row = lax.broadcasted_iota(jnp.int32, (n, n), 0)
    col = lax.broadcasted_iota(jnp.int32, (n, n), 1)
    own_keep = (col <= row)[None]
    prev_keep = (col >= row)[None]

    nblk = seq // n
    for pi, d in enumerate(dilations):
        L = seq // d
        nb = L // n

        def load(ref, d=d, L=L, nb=nb):
            if d == 1:
                return ref[0].reshape(nblk, n, LANES).astype(BF16)
            parts = [ref[0, pl.ds(r, L, stride=d), :].reshape(nb, n, LANES) for r in range(d)]
            return jnp.concatenate(parts, axis=0).astype(BF16)

        qb, kb, vb = load(q_ref), load(k_ref), load(v_ref)
        if nb > 1:
            zero = jnp.zeros((1, n, LANES), BF16)
            kp = jnp.concatenate([zero, kb[:nblk - 1]], axis=0)
            vp = jnp.concatenate([zero, vb[:nblk - 1]], axis=0)
            blk = lax.broadcasted_iota(jnp.int32, (nblk, n, n), 0)
            prev_ok = jnp.logical_and(prev_keep, lax.rem(blk, nb) != 0)
        zq = jnp.zeros_like(qb)
        heads = []
        for hh in range(2):
            qh = jnp.where(lo_half, qb, zq) if hh == 0 else jnp.where(lo_half, zq, qb)
            s_own = jnp.einsum("bqd,bkd->bqk", qh, kb, preferred_element_type=F32)
            s_own = jnp.where(own_keep, s_own, NEG_BIG)
            m = jnp.max(s_own, axis=-1, keepdims=True)
            if nb > 1:
                s_prev = jnp.einsum("bqd,bkd->bqk", qh, kp, preferred_element_type=F32)
                s_prev = jnp.where(prev_ok, s_prev, NEG_BIG)
                m = jnp.maximum(m, jnp.max(s_prev, axis=-1, keepdims=True))
            p_own = jnp.exp2(s_own - m)
            l = jnp.sum(p_own, axis=-1, keepdims=True)
            pv = jnp.einsum("bqk,bkd->bqd", p_own.astype(BF16), vb, preferred_element_type=F32)
            if nb > 1:
                p_prev = jnp.exp2(s_prev - m)
                l = l + jnp.sum(p_prev, axis=-1, keepdims=True)
                pv = pv + jnp.einsum("bqk,bkd->bqd", p_prev.astype(BF16), vp, preferred_element_type=F32)
            heads.append((m, l, pv))
        m_f = jnp.where(lo_half, heads[0][0], heads[1][0])
        l_f = jnp.where(lo_half, heads[0][1], heads[1][1])
        a_f = jnp.where(lo_half, heads[0][2], heads[1][2])
        for dst, val in ((acc_ref, a_f), (m_ref, m_f), (l_ref, l_f)):
            if d == 1:
                dst[pi] = val.reshape(seq, LANES)
            else:
                for r in range(d):
                    dst[pi, pl.ds(r, L, stride=d), :] = val[r * nb:(r + 1) * nb].reshape(L, LANES)

    np_ = len(dilations)
    m_all = m_ref[0]
    for pi in range(1, np_):
        m_all = jnp.maximum(m_all, m_ref[pi])
    num = jnp.zeros((seq, LANES), F32)
    den = jnp.zeros((seq, LANES), F32)
    for pi in range(np_):
        w = jnp.exp2(m_ref[pi] - m_all)
        num = num + w * acc_ref[pi]
        den = den + w * l_ref[pi]
    o_ref[0] = num / den


def _dilated_attention(qc, kc, vc, seq):
    B = qc.shape[0]
    n = DILATED_PATTERNS[0][0] // DILATED_PATTERNS[0][1]
    dil = tuple(d for _, d in DILATED_PATTERNS)
    for w, d in DILATED_PATTERNS:
        assert w // d == n and seq % (d * n) == 0
    spec = pl.BlockSpec((1, seq, LANES), lambda b, p: (b, 0, p))
    np_ = len(dil)
    return pl.pallas_call(
        functools.partial(_dilated_kernel, seq=seq, n=n, dilations=dil),
        out_shape=jax.ShapeDtypeStruct((B, seq, DIL_WIDTH), F32),
        grid=(B, DIL_WIDTH // LANES),
        in_specs=[spec, spec, spec],
        out_specs=spec,
        scratch_shapes=[pltpu.VMEM((np_, seq, LANES), F32)] * 3,
        compiler_params=_cparams(("parallel", "parallel")),
        name="dilated",
    )(qc, kc, vc)


def _memkv_kernel(mem_ref, g_ref, w_ref, gk_ref, bd64_ref, k_ref, v_ref):
    mn = _rms_rows(mem_ref[0], g_ref[...]).astype(BF16)
    kv = _dot(mn, w_ref[...])
    k = jnp.concatenate(_group_norm_tiles(kv[:, :MEM_WIDTH], bd64_ref[...], 1.0 / HEAD_DIM), axis=1)
    k_ref[0] = (k * gk_ref[...]).astype(BF16)
    v_ref[0] = kv[:, MEM_WIDTH:].astype(BF16)


def _memkv(mem, prm, tabs):
    B, M, D = mem.shape
    return pl.pallas_call(
        _memkv_kernel,
        out_shape=[jax.ShapeDtypeStruct((B, M, MEM_WIDTH), BF16)] * 2,
        grid=(B,),
        in_specs=[pl.BlockSpec((1, M, D), lambda b: (b, 0, 0)), _full((1, D)), _full((D, 2 * MEM_WIDTH)),
                  _full((1, MEM_WIDTH)), _full((LANES, LANES))],
        out_specs=[pl.BlockSpec((1, M, MEM_WIDTH), lambda b: (b, 0, 0))] * 2,
        compiler_params=_cparams(("parallel",)),
        name="memkv",
    )(mem, prm["g_mem"], prm["w_mkv"], prm["g_km"], tabs["bd64"])


def _postattn_kernel(x_ref, oa_ref, ob_ref, oc_ref, gg_ref, wout_ref, gx_ref, wmq_ref, gqm_ref, bd64_ref,
                     km_ref, vm_ref, wmo_ref, gf_ref, wrh_ref, wrl_ref, br_ref, tri_ref,
                     x2_ref, ti_ref, tw_ref, rk_ref, cnt_ref, carry_ref, *, n_experts):
    i = pl.program_id(0)
    tm = x_ref.shape[0]
    hm = tm // POSTATTN_SPLIT
    a0, a1 = FOX_WIDTH, FOX_WIDTH + MLA_WIDTH
    lane = lax.broadcasted_iota(jnp.int32, (hm, LANES), 1)
    lane_f = lane.astype(F32)
    lo_half = lane < HEAD_DIM
    tri = tri_ref[...]

    @pl.when(i == 0)
    def _():
        carry_ref[...] = jnp.zeros_like(carry_ref)

    carry = {"n": carry_ref[...]}

    def rows(g):
        rs = slice(g * hm, (g + 1) * hm)
        ma = _rms_rows(oa_ref[rs, :].astype(F32), gg_ref[:, :a0]).astype(BF16)
        mb = _rms_rows(ob_ref[rs, :].astype(F32), gg_ref[:, a0:a1]).astype(BF16)
        mc = _rms_rows(oc_ref[rs, :], gg_ref[:, a1:]).astype(BF16)
        yield
        x1 = (x_ref[rs, :] + _dot(ma, wout_ref[:a0, :]) + _dot(mb, wout_ref[a0:a1, :])
              + _dot(mc, wout_ref[a1:, :]))
        yield
        h2 = _rms_rows(x1, gx_ref[...]).astype(BF16)
        yield
        qm = _dot(h2, wmq_ref[...])
        yield
        qm_t = _group_norm_tiles(qm, bd64_ref[...], 1.0 / HEAD_DIM)
        outs = []
        for p in range(MEM_WIDTH // LANES):
            sl = slice(p * LANES, (p + 1) * LANES)
            q2 = (qm_t[p] * gqm_ref[:, sl]).astype(BF16)
            k2 = km_ref[0, :, sl]
            v2 = vm_ref[0, :, sl]
            zero = jnp.zeros_like(q2)
            res = []
            for hh in range(2):
                qh = jnp.where(lo_half, q2, zero) if hh == 0 else jnp.where(lo_half, zero, q2)
                s = _dot_nt(qh, k2)
                pr = jnp.exp2(s - jnp.max(s, axis=-1, keepdims=True))
                l = jnp.sum(pr, axis=-1, keepdims=True)
                res.append(_dot(pr.astype(BF16), v2) / l)
            outs.append(jnp.where(lo_half, res[0], res[1]))
            yield
        om = jnp.concatenate(outs, axis=1).astype(BF16)
        x2 = x1 + _dot(om, wmo_ref[...])
        x2_ref[rs, :] = x2
        yield
        h3 = _rms_rows(x2, gf_ref[...])
        h3_hi = h3.astype(BF16)
        h3_lo = (h3 - h3_hi.astype(F32)).astype(BF16)
        yield
        logits = _dot(h3_hi, wrh_ref[...]) + _dot(h3_hi, wrl_ref[...]) + _dot(h3_lo, wrh_ref[...]) + br_ref[...]
        yield
        logits = jnp.where(lane < n_experts, logits, NEG_BIG)
        work = logits
        sel = jnp.zeros((hm, LANES), F32)
        vals, idxs = [], []
        for _ in range(TOP_K):
            mv = jnp.max(work, axis=-1, keepdims=True)
            ix = jnp.min(jnp.where(work == mv, lane_f, float(LANES)), axis=-1, keepdims=True)
            hit = lane_f == ix
            work = jnp.where(hit, NEG_BIG, work)
            sel = jnp.where(hit, 1.0, sel)
            vals.append(mv)
            idxs.append(ix)
        es = [jnp.exp(v - vals[0]) for v in vals]
        den = es[0] + es[1] + es[2] + es[3]
        yield
        before = _dot(tri, sel.astype(BF16))
        yield
        before = before + carry["n"]
        carry["n"] = before[hm - 1:, :] + sel[hm - 1:, :]
        ti = jnp.zeros((hm, LANES), F32)
        tw = jnp.zeros((hm, LANES), F32)
        rk = jnp.zeros((hm, LANES), F32)
        for kk in range(TOP_K):
            at = lane == kk
            ti = jnp.where(at, idxs[kk], ti)
            tw = jnp.where(at, es[kk] / den, tw)
            r_k = jnp.sum(jnp.where(lane_f == idxs[kk], before, 0.0), axis=-1, keepdims=True)
            rk = jnp.where(at, r_k, rk)
        ti_ref[rs, :] = ti.astype(jnp.int32)
        tw_ref[rs, :] = tw
        rk_ref[rs, :] = rk.astype(jnp.int32)

    _interleave([rows(g) for g in range(POSTATTN_SPLIT)])
    carry_ref[...] = carry["n"]
    cnt_ref[...] = carry["n"]


def _postattn(x2d, oa, ob, oc, kmem, vmem, prm, tabs, seq, tm, n_experts):
    T, D = x2d.shape
    M = kmem.shape[1]
    tpb = seq // tm
    hm = tm // POSTATTN_SPLIT
    row = lambda w: pl.BlockSpec((tm, w), lambda i: (i, 0))
    memspec = pl.BlockSpec((1, M, MEM_WIDTH), lambda i: (i // tpb, 0, 0))
    mix = FOX_WIDTH + MLA_WIDTH + DIL_WIDTH
    return pl.pallas_call(
        functools.partial(_postattn_kernel, n_experts=n_experts),
        out_shape=[jax.ShapeDtypeStruct((T, D), F32),
                   jax.ShapeDtypeStruct((T, LANES), jnp.int32), jax.ShapeDtypeStruct((T, LANES), F32),
                   jax.ShapeDtypeStruct((T, LANES), jnp.int32), jax.ShapeDtypeStruct((1, LANES), F32)],
        grid=(T // tm,),
        in_specs=[row(D), row(FOX_WIDTH), row(MLA_WIDTH), row(DIL_WIDTH), _full((1, mix)), _full((mix, D)),
                  _full((1, D)), _full((D, MEM_WIDTH)), _full((1, MEM_WIDTH)), _full((LANES, LANES)),
                  memspec, memspec, _full((MEM_WIDTH, D)), _full((1, D)),
                  _full((D, LANES)), _full((D, LANES)), _full((1, LANES)), _full((hm, hm))],
        out_specs=[row(D), row(LANES), row(LANES), row(LANES), _full((1, LANES))],
        scratch_shapes=[pltpu.VMEM((1, LANES), F32)],
        compiler_params=_cparams(("arbitrary",)),
        name="postattn",
    )(x2d, oa, ob, oc, prm["g_group"], prm["w_out"], prm["g_xmem"], prm["w_mq"], prm["g_qm"], tabs["bd64"],
      kmem, vmem, prm["w_mo"], prm["g_ffn"], prm["w_r_hi"], prm["w_r_lo"], prm["b_router"], tabs["tri_strict"])


def _row_copy(src_ref, src_row, dst_ref, dst_row, sem):
    return pltpu.make_async_copy(src_ref.at[pl.ds(src_row, 1), :], dst_ref.at[pl.ds(dst_row, 1), :], sem)


def _dispatch_kernel(fill_ref, pos_ref, x_ref, xs_ref, zero_ref, sem, zsem, *, tm, tr):
    @pl.when(pl.program_id(0) == 0)
    def _():
        n_experts = fill_ref.shape[0] - 1
        n_rows = xs_ref.shape[0]
        zero_ref[...] = jnp.zeros_like(zero_ref)
        starts = [(fill_ref[e], fill_ref[e] >= 0) for e in range(n_experts)]
        for j in range(n_experts + 1):
            start = fill_ref[n_experts] + j * tr
            starts.append((start, start < n_rows))

        def fill(start):
            return pltpu.make_async_copy(zero_ref, xs_ref.at[pl.ds(pl.multiple_of(start, tr), tr), :], zsem)

        for start, live in starts:
            @pl.when(live)
            def _(start=start):
                fill(start).start()
        for start, live in starts:
            @pl.when(live)
            def _(start=start):
                fill(start).wait()

    def issue(r, _):
        for kk in range(TOP_K):
            _row_copy(x_ref, r, xs_ref, pos_ref[0, 0, r * TOP_K + kk], sem).start(priority=kk % 2)
        return 0
    lax.fori_loop(0, tm, issue, 0, unroll=ROW_DMA_UNROLL)
    n = tm * TOP_K
    pltpu.make_async_copy(xs_ref.at[pl.ds(0, n), :], xs_ref.at[pl.ds(0, n), :], sem).wait()


def _dispatch(x2, pos, pad_start, n_rows, tm, tr):
    T, D = x2.shape
    return pl.pallas_call(
        functools.partial(_dispatch_kernel, tm=tm, tr=tr),
        out_shape=jax.ShapeDtypeStruct((n_rows, D), x2.dtype),
        grid_spec=pltpu.PrefetchScalarGridSpec(
            num_scalar_prefetch=1,
            grid=(T // tm,),
            in_specs=[pl.BlockSpec((1, 1, tm * TOP_K), lambda i, pad: (i, 0, 0), memory_space=pltpu.SMEM),
                      pl.BlockSpec((tm, D), lambda i, pad: (i, 0))],
            out_specs=pl.BlockSpec(memory_space=pl.ANY),
            scratch_shapes=[pltpu.VMEM((tr, D), x2.dtype), pltpu.SemaphoreType.DMA, pltpu.SemaphoreType.DMA],
        ),
        compiler_params=_cparams(("arbitrary",)),
        name="dispatch",
    )(pad_start, pos.reshape(T // tm, 1, tm * TOP_K), x2)


def _experts_kernel(te_ref, nu_ref, xs_ref, gf_ref, w1_ref, b1g_ref, b1l_ref, w2_ref, b2_ref, perm_ref,
                    ys_ref, w1s_ref, w2s_ref):
    i = pl.program_id(0)
    two_f = w1_ref.shape[3]
    n_blk = two_f // DEINT_BLOCK
    half = DEINT_BLOCK // 2

    @pl.when(i < nu_ref[0])
    def _():
        @pl.when(jnp.logical_or(i == 0, te_ref[i] != te_ref[jnp.maximum(i - 1, 0)]))
        def _():
            perm = perm_ref[...]
            for b in range(n_blk):
                sl = slice(b * DEINT_BLOCK, (b + 1) * DEINT_BLOCK)
                w1s_ref[:, sl] = _dot(w1_ref[0, 0, :, sl].astype(BF16), perm).astype(BF16)
            w2s_ref[...] = w2_ref[0, 0].astype(BF16)

        x = _rms_rows(xs_ref[...], gf_ref[...]).astype(BF16)
        h = _dot(x, w1s_ref[...])
        acts = []
        for b in range(n_blk):
            glu = h[:, b * DEINT_BLOCK:b * DEINT_BLOCK + half] + b1g_ref[0, :, b * half:(b + 1) * half]
            lin = h[:, b * DEINT_BLOCK + half:(b + 1) * DEINT_BLOCK] + b1l_ref[0, :, b * half:(b + 1) * half]
            glu = jnp.minimum(glu, SWIGLU_LIMIT)
            lin = jnp.clip(lin, -SWIGLU_LIMIT, SWIGLU_LIMIT)
            acts.append((glu * (1.0 / (1.0 + jnp.exp(-SWIGLU_ALPHA * glu))) * (lin + 1.0)).astype(BF16))
        act = jnp.concatenate(acts, axis=1) if n_blk > 1 else acts[0]
        ys_ref[...] = _dot(act, w2s_ref[...]) + b2_ref[0]

    @pl.when(i >= nu_ref[0])
    def _():
        ys_ref[...] = jnp.zeros_like(ys_ref)


def _experts(tile_expert, n_used, xs, w_e1, w_e2, layer, prm, tabs, tr):
    n_rows, D = xs.shape
    two_f = w_e1.shape[3]
    F = two_f // 2
    assert two_f % DEINT_BLOCK == 0
    n_tiles = n_rows // tr
    rows = pl.BlockSpec((tr, D), lambda i, te, nu: (i, 0))
    wspec = lambda s: pl.BlockSpec((1,) + s, lambda i, te, nu: (te[i], 0, 0))
    lspec = lambda s: pl.BlockSpec((1, 1) + s, lambda i, te, nu: (layer, te[i], 0, 0))
    return pl.pallas_call(
        _experts_kernel,
        out_shape=jax.ShapeDtypeStruct((n_rows, D), F32),
        grid_spec=pltpu.PrefetchScalarGridSpec(
            num_scalar_prefetch=2,
            grid=(n_tiles,),
            in_specs=[rows, pl.BlockSpec((1, D), lambda i, te, nu: (0, 0)),
                      lspec((D, two_f)), wspec((1, F)), wspec((1, F)), lspec((F, D)), wspec((1, D)),
                      pl.BlockSpec((DEINT_BLOCK, DEINT_BLOCK), lambda i, te, nu: (0, 0))],
            out_specs=rows,
            scratch_shapes=[pltpu.VMEM((D, two_f), BF16), pltpu.VMEM((F, D), BF16)],
        ),
        compiler_params=_cparams(("arbitrary",)),
        name="experts",
    )(tile_expert, n_used, xs, prm["g_ffn"], w_e1, prm["b1g"], prm["b1l"], w_e2, prm["b2"], tabs["deint"])


def _combine_kernel(pos_ref, x_ref, tw_ref, ys_ref, o_ref, buf, sem, *, tm):
    def issue(r, _):
        for kk in range(TOP_K):
            _row_copy(ys_ref, pos_ref[0, 0, r * TOP_K + kk], buf.at[kk], r, sem).start(priority=kk % 2)
        return 0
    lax.fori_loop(0, tm, issue, 0, unroll=ROW_DMA_UNROLL)
    pltpu.make_async_copy(buf, buf, sem).wait()

    tw = tw_ref[...]
    acc = x_ref[...]
    for kk in range(TOP_K):
        acc = acc + tw[:, kk:kk + 1] * buf[kk]
    o_ref[...] = acc


def _combine(x2, tw, ys, pos, tm):
    T, D = x2.shape
    W = ys.shape[1]
    return pl.pallas_call(
        functools.partial(_combine_kernel, tm=tm),
        out_shape=jax.ShapeDtypeStruct((T, D), F32),
        grid=(T // tm,),
        in_specs=[pl.BlockSpec((1, 1, tm * TOP_K), lambda i: (i, 0, 0), memory_space=pltpu.SMEM),
                  pl.BlockSpec((tm, D), lambda i: (i, 0)),
                  pl.BlockSpec((tm, LANES), lambda i: (i, 0)),
                  pl.BlockSpec(memory_space=pl.ANY)],
        out_specs=pl.BlockSpec((tm, D), lambda i: (i, 0)),
        scratch_shapes=[pltpu.VMEM((TOP_K, tm, W), ys.dtype), pltpu.SemaphoreType.DMA],
        compiler_params=_cparams(("arbitrary",)),
        name="combine",
    )(pos.reshape(T // tm, 1, tm * TOP_K), x2, tw, ys)


def _tile_gain(g, reps, scale=1.0):
    return (jnp.tile(g.astype(F32), reps) * scale)[None, :]


def _prep_layer(l, D, w_in, b_forget, g_mix, g_qa, g_ka, g_cq, g_ckv, w_uq, w_ukv, g_qb, g_kb, g_qc, g_kc,
                g_group, w_out, g_xmem, g_mem, w_mq, w_mkv, g_qm, g_km, w_mo, g_ffn, w_router, b_router,
                w_e1, b_e1, w_e2, b_e2):
    wi = w_in[l]
    sizes = (FOX_WIDTH, FOX_WIDTH, FOX_WIDTH, FOX_HEADS, MLA_Q_RANK, MLA_KV_RANK, MLA_ROPE_DIM,
             DIL_WIDTH, DIL_WIDTH, DIL_WIDTH)
    pts = np.cumsum((0,) + sizes)
    qa, ka, va, fa, cq, ckv, kr, qc, kc, vc = [wi[:, pts[j]:pts[j + 1]] for j in range(len(sizes))]
    misc = jnp.zeros((D, LANES), F32).at[:, :FOX_HEADS].set(fa).at[:, 32:32 + MLA_ROPE_DIM].set(kr)
    w_main = jnp.concatenate([qa, ka, va, cq, ckv, qc, kc, vc, misc], axis=1).astype(BF16)
    b_misc = jnp.zeros((1, LANES), F32).at[0, :FOX_HEADS].set(b_forget[l])

    uq = w_uq[l].reshape(MLA_Q_RANK, MLA_HEADS, MLA_QK_DIM)
    uq = jnp.pad(uq, ((0, 0), (0, 0), (0, MLA_PAD - MLA_QK_DIM))).reshape(MLA_Q_RANK, MLA_HEADS * MLA_PAD)
    ukv = w_ukv[l].reshape(MLA_KV_RANK, MLA_HEADS, MLA_NOPE_DIM + MLA_V_DIM)
    uk = jnp.pad(ukv[:, :, :MLA_NOPE_DIM], ((0, 0), (0, 0), (0, MLA_PAD - MLA_NOPE_DIM)))
    uk = uk.reshape(MLA_KV_RANK, MLA_HEADS * MLA_PAD)
    uv = ukv[:, :, MLA_NOPE_DIM:].reshape(MLA_KV_RANK, MLA_WIDTH)
    pad_gain = lambda g: jnp.pad(g.astype(F32), (0, MLA_PAD - MLA_QK_DIM))

    E = w_e1.shape[1]
    wr = jnp.zeros((D, LANES), F32).at[:, :E].set(w_router[l])
    wr_hi = wr.astype(BF16)
    wr_lo = (wr - wr_hi.astype(F32)).astype(BF16)
    q_scale = lambda dim: dim ** -0.5 * LOG2E
    return dict(
        g_mix=g_mix[l][None], w_main=w_main, b_misc=b_misc,
        g_qa=_tile_gain(g_qa[l], FOX_HEADS, q_scale(HEAD_DIM)), g_ka=_tile_gain(g_ka[l], FOX_HEADS),
        g_cq=g_cq[l][None], g_ckv=g_ckv[l][None],
        w_uq=uq.astype(BF16), w_uk=uk.astype(BF16), w_uv=uv.astype(BF16),
        g_qb=_tile_gain(pad_gain(g_qb[l]), MLA_HEADS, q_scale(MLA_QK_DIM)),
        g_kb=_tile_gain(pad_gain(g_kb[l]), MLA_HEADS),
        g_qc=_tile_gain(g_qc[l], DIL_HEADS, q_scale(HEAD_DIM)), g_kc=_tile_gain(g_kc[l], DIL_HEADS),
        g_group=g_group[l][None], w_out=w_out[l].astype(BF16),
        g_xmem=g_xmem[l][None], g_mem=g_mem[l][None],
        w_mq=w_mq[l].astype(BF16), w_mkv=w_mkv[l].astype(BF16),
        g_qm=_tile_gain(g_qm[l], MEM_HEADS, q_scale(HEAD_DIM)), g_km=_tile_gain(g_km[l], MEM_HEADS),
        w_mo=w_mo[l].astype(BF16), g_ffn=g_ffn[l][None],
        w_r_hi=wr_hi, w_r_lo=wr_lo,
        b_router=jnp.zeros((1, LANES), F32).at[0, :E].set(b_router[l]),
        b1g=b_e1[l][:, None, 0::2], b1l=b_e1[l][:, None, 1::2], b2=b_e2[l][:, None, :],
    )


def _rope_lane_tables(positions, rot_dim, period, offset):
    half = rot_dim // 2
    assert offset % half == 0 and period % half == 0 and LANES % half == 0
    inv = ROPE_THETA ** (-jnp.arange(0, rot_dim, 2, dtype=F32) / rot_dim)
    ang = positions.astype(F32).reshape(-1)[:, None] * inv
    cos = jnp.tile(jnp.cos(ang), (1, LANES // half))
    sin = jnp.tile(jnp.sin(ang), (1, LANES // half))
    lane = np.arange(LANES) % period - offset
    first = (lane >= 0) & (lane < half)
    second = (lane >= half) & (lane < rot_dim)
    c = jnp.where(first | second, cos, 1.0)
    s_dn = jnp.where(second, sin, 0.0)
    s_up = jnp.where(first, -sin, 0.0)
    return c, s_dn, s_up


def _tables(positions, n_incl, n_strict):
    ri, rs = np.arange(n_incl), np.arange(n_strict)
    bd = (np.arange(LANES)[:, None] // HEAD_DIM) == (np.arange(LANES)[None, :] // HEAD_DIM)
    rep = np.zeros((LANES, MLA_HEADS * MLA_PAD), np.float32)
    for h in range(MLA_HEADS):
        for j in range(MLA_ROPE_DIM):
            rep[32 + j, h * MLA_PAD + MLA_NOPE_DIM + j] = 1.0
    half = DEINT_BLOCK // 2
    deint = np.zeros((DEINT_BLOCK, DEINT_BLOCK), np.float32)
    deint[2 * np.arange(half), np.arange(half)] = 1.0
    deint[2 * np.arange(half) + 1, half + np.arange(half)] = 1.0
    cm, sdm, sum_ = _rope_lane_tables(positions, MLA_ROPE_DIM, LANES, MLA_NOPE_DIM)
    cd, sdd, sud = _rope_lane_tables(positions, ROT_DIM, HEAD_DIM, 0)
    return dict(
        tri_incl=jnp.asarray(ri[:, None] >= ri[None, :], BF16),
        tri_strict=jnp.asarray(rs[:, None] > rs[None, :], BF16),
        bd64=jnp.asarray(bd, BF16), ones=jnp.ones((LANES, LANES), BF16), rep=jnp.asarray(rep, BF16),
        deint=jnp.asarray(deint, BF16),
        cm=cm, sdm=sdm, sum=sum_, cd=cd, sdd=sdd, sud=sud,
    )


def _routing_tables(ti, rk, cnt, n_experts, tr, n_tiles):
    counts = cnt[0, :n_experts].astype(jnp.int32)
    padded = ((counts + tr - 1) // tr) * tr
    ends = jnp.cumsum(padded)
    starts = ends - padded
    idx = ti[:, :TOP_K]
    hit = idx[:, :, None] == jnp.arange(n_experts, dtype=jnp.int32)
    pos = rk[:, :TOP_K] + jnp.sum(jnp.where(hit, starts, 0), axis=-1)
    tile_start = jnp.arange(n_tiles, dtype=jnp.int32) * tr
    tile_expert = jnp.sum((ends[None, :] <= tile_start[:, None]).astype(jnp.int32), axis=1)
    tile_expert = jnp.minimum(tile_expert, n_experts - 1)
    n_used = (ends[n_experts - 1:] // tr).astype(jnp.int32)
    pad_fill = jnp.where(padded > 0, ends - tr, -1)
    fill_start = jnp.concatenate([pad_fill, ends[n_experts - 1:]])
    return pos.astype(jnp.int32), tile_expert.astype(jnp.int32), fill_start.astype(jnp.int32), n_used


def _pick_tile(seq, want):
    t = min(want, seq)
    while seq % t:
        t //= 2
    return t


def kernel(x, mem, positions, g_mix, w_in, b_forget, g_qa, g_ka, g_cq, g_ckv, w_uq, w_ukv, g_qb, g_kb, g_qc, g_kc, g_group, w_out, g_xmem, g_mem, w_mq, w_mkv, g_qm, g_km, w_mo, g_ffn, w_router, b_router, w_e1, b_e1, w_e2, b_e2):
    B, S, D = x.shape
    T = B * S
    depth = w_in.shape[0]
    E = w_e1.shape[1]
    tm_in = _pick_tile(S, INPROJ_TILE)
    tm_post = _pick_tile(S, POSTATTN_TILE)
    tq = _pick_tile(S, ATTN_TILE)
    tr = EXPERT_ROW_TILE
    tmoe = _pick_tile(S, MOE_TOKEN_TILE)
    n_rows = T * TOP_K + (E + 1) * tr
    n_tiles = n_rows // tr

    tabs = _tables(positions, tm_in // INPROJ_SPLIT, tm_post // POSTATTN_SPLIT)
    x2d = x.reshape(T, D)
    for l in range(depth):
        prm = _prep_layer(l, D, w_in, b_forget, g_mix, g_qa, g_ka, g_cq, g_ckv, w_uq, w_ukv, g_qb, g_kb,
                          g_qc, g_kc, g_group, w_out, g_xmem, g_mem, w_mq, w_mkv, g_qm, g_km, w_mo, g_ffn,
                          w_router, b_router, w_e1, b_e1, w_e2, b_e2)
        qa, ka, va, csum, qb, kb, vb, qc, kc, vc = _inproj(x2d, prm, tabs, S, tm_in)
        kbias = -LOG2E * csum[:, :FOX_HEADS].reshape(B, S, FOX_HEADS // 2, 2).transpose(0, 2, 3, 1)
        r3 = lambda a: a.reshape(B, S, a.shape[-1])
        oa = _causal_attention(r3(qa), r3(ka), r3(va), kbias, tq=tq, split_heads=False)
        ob = _causal_attention(r3(qb), r3(kb), r3(vb), None, tq=tq, split_heads=True)
        oc = _dilated_attention(r3(qc), r3(kc), r3(vc), S)
        kmem, vmem = _memkv(mem, prm, tabs)
        x2, ti, tw, rk, cnt = _postattn(x2d, oa.reshape(T, -1), ob.reshape(T, -1), oc.reshape(T, -1),
                                        kmem, vmem, prm, tabs, S, tm_post, E)
        pos, tile_expert, pad_start, n_used = _routing_tables(ti, rk, cnt, E, tr, n_tiles)
        xs = _dispatch(x2, pos, pad_start, n_rows, tmoe, tr)
        ys = _experts(tile_expert, n_used, xs, w_e1, w_e2, l, prm, tabs, tr)
        x2d = _combine(x2, tw, ys, pos, tmoe)
    return x2d.reshape(B, S, D)
```

```python
import functools
import math

import jax
import jax.numpy as jnp
import numpy as np
from jax import lax
from jax.experimental import pallas as pl
from jax.experimental.pallas import tpu as pltpu

HEAD_DIM = 64
FOX_HEADS = 4
MLA_HEADS = 6
DIL_HEADS = 6
MLA_Q_RANK = 384
MLA_KV_RANK = 256
MLA_NOPE_DIM = 64
MLA_ROPE_DIM = 32
MLA_QK_DIM = MLA_NOPE_DIM + MLA_ROPE_DIM
MLA_V_DIM = 64
ROT_DIM = HEAD_DIM // 4
ROPE_THETA = 500000.0
DILATED_PATTERNS = ((128, 1), (512, 4), (2048, 16))
FOX_WIDTH = FOX_HEADS * HEAD_DIM
MLA_WIDTH = MLA_HEADS * MLA_V_DIM
DIL_WIDTH = DIL_HEADS * HEAD_DIM
MEM_HEADS = 4
MEM_WIDTH = MEM_HEADS * HEAD_DIM
TOP_K = 4
SWIGLU_ALPHA = 1.702
SWIGLU_LIMIT = 7.0
RMS_EPS = 1e-6
LOG2E = math.log2(math.e)

LANES = 128
VMEM_LIMIT_BYTES = 56 * 1024 * 1024

INPROJ_TILE = 512
INPROJ_SPLIT = 1
POSTATTN_TILE = 1024
POSTATTN_SPLIT = 2
ATTN_TILE = 512
EXPERT_ROW_TILE = 512
EXPERT_SPLIT = 2
MOE_TOKEN_TILE = 1024
ROW_DMA_UNROLL = 8

NEG_BIG = -1e30
MLA_PAD = 128
DEINT_BLOCK = 256

F32 = jnp.float32
BF16 = jnp.bfloat16


def _cparams(sem):
    return pltpu.CompilerParams(dimension_semantics=sem, vmem_limit_bytes=VMEM_LIMIT_BYTES)


def _full(shape):
    nd = len(shape)
    return pl.BlockSpec(shape, lambda *_: (0,) * nd)


def _dot(a, b):
    return jnp.dot(a, b, preferred_element_type=F32)


def _dot_nt(a, b):
    return lax.dot_general(a, b, (((1,), (1,)), ((), ())), preferred_element_type=F32)


def _rms_rows(x, g):
    return x * lax.rsqrt(jnp.mean(x * x, axis=-1, keepdims=True) + RMS_EPS) * g


def _group_norm_tiles(y, ones_mat, inv_n):
    outs = []
    for j in range(y.shape[1] // LANES):
        yj = y[:, j * LANES:(j + 1) * LANES]
        ss = _dot((yj * yj).astype(BF16), ones_mat)
        outs.append(yj * lax.rsqrt(ss * inv_n + RMS_EPS))
    return outs


def _rope_tile(y, half, c, s_dn, s_up):
    return y * c + pltpu.roll(y, half, 1) * s_dn + pltpu.roll(y, LANES - half, 1) * s_up


def _interleave(gens):
    live = list(gens)
    while live:
        for g in list(live):
            try:
                next(g)
            except StopIteration:
                live.remove(g)


def _inproj_kernel(x_ref, gmix_ref, w_ref, bmisc_ref, tri_ref, bd64_ref, ones_ref,
                   gqa_ref, gka_ref, gcq_ref, gckv_ref, wuq_ref, wuk_ref, wuv_ref, rep_ref,
                   gqb_ref, gkb_ref, gqc_ref, gkc_ref,
                   cm_ref, sdm_ref, sum_ref, cd_ref, sdd_ref, sud_ref,
                   qa_ref, ka_ref, va_ref, c_ref, qb_ref, kb_ref, vb_ref, qc_ref, kc_ref, vc_ref,
                   carry_ref, *, tiles_per_batch):
    i = pl.program_id(0)
    tm = x_ref.shape[0]
    hm = tm // INPROJ_SPLIT
    bd64 = bd64_ref[...]
    ones = ones_ref[...]
    tri = tri_ref[...]
    half_b = MLA_ROPE_DIM // 2
    half_c = ROT_DIM // 2
    offs = np.cumsum((0, FOX_WIDTH, FOX_WIDTH, FOX_WIDTH, MLA_Q_RANK, MLA_KV_RANK,
                      DIL_WIDTH, DIL_WIDTH, DIL_WIDTH, LANES))

    @pl.when(i % tiles_per_batch == 0)
    def _():
        carry_ref[...] = jnp.zeros_like(carry_ref)

    carry = {"c": carry_ref[...]}

    def rows(g):
        rs = slice(g * hm, (g + 1) * hm)
        h = _rms_rows(x_ref[rs, :], gmix_ref[...]).astype(BF16)
        seg = lambda j: _dot(h, w_ref[:, offs[j]:offs[j + 1]])
        yield
        qa, ka, va, cq = seg(0), seg(1), seg(2), seg(3)
        yield
        qa_ref[rs, :] = (jnp.concatenate(_group_norm_tiles(qa, bd64, 1.0 / HEAD_DIM), axis=1)
                         * gqa_ref[...]).astype(BF16)
        ka_ref[rs, :] = (jnp.concatenate(_group_norm_tiles(ka, bd64, 1.0 / HEAD_DIM), axis=1)
                         * gka_ref[...]).astype(BF16)
        va_ref[rs, :] = va.astype(BF16)
        cqn = _rms_rows(cq, gcq_ref[...]).astype(BF16)
        yield
        ckv, qc = seg(4), seg(5)
        qb_pre = _dot(cqn, wuq_ref[...])
        yield
        ckvn = _rms_rows(ckv, gckv_ref[...]).astype(BF16)
        cd, sdd, sud = cd_ref[rs, :], sdd_ref[rs, :], sud_ref[rs, :]
        qc_t = _group_norm_tiles(qc, bd64, 1.0 / HEAD_DIM)
        for j in range(DIL_WIDTH // LANES):
            sl = slice(j * LANES, (j + 1) * LANES)
            qc_ref[rs, sl] = _rope_tile(qc_t[j] * gqc_ref[:, sl], half_c, cd, sdd, sud)
        yield
        kc, vc, misc = seg(6), seg(7), seg(8)
        kb_nope = _dot(ckvn, wuk_ref[...])
        vb_ref[rs, :] = _dot(ckvn, wuv_ref[...]).astype(BF16)
        yield
        cm, sdm, sum_ = cm_ref[rs, :], sdm_ref[rs, :], sum_ref[rs, :]
        qb_t = _group_norm_tiles(qb_pre, ones, 1.0 / MLA_QK_DIM)
        for j in range(MLA_HEADS):
            sl = slice(j * LANES, (j + 1) * LANES)
            qb_ref[rs, sl] = _rope_tile(qb_t[j] * gqb_ref[:, sl], half_b, cm, sdm, sum_).astype(BF16)
        kc_t = _group_norm_tiles(kc, bd64, 1.0 / HEAD_DIM)
        for j in range(DIL_WIDTH // LANES):
            sl = slice(j * LANES, (j + 1) * LANES)
            kc_ref[rs, sl] = _rope_tile(kc_t[j] * gkc_ref[:, sl], half_c, cd, sdd, sud)
        vc_ref[rs, :] = vc
        z = misc + bmisc_ref[...]
        logf = jnp.minimum(z, 0.0) - jnp.log(1.0 + jnp.exp(-jnp.abs(z)))
        lane = lax.broadcasted_iota(jnp.int32, logf.shape, 1)
        logf = jnp.where(lane < FOX_HEADS, logf, 0.0)
        p0 = logf.astype(BF16)
        r0 = logf - p0.astype(F32)
        p1 = r0.astype(BF16)
        p2 = (r0 - p1.astype(F32)).astype(BF16)
        yield
        csum = _dot(tri, p0) + _dot(tri, p1) + _dot(tri, p2)
        kb_pre = kb_nope + _dot(misc.astype(BF16), rep_ref[...])
        yield
        kb_t = _group_norm_tiles(kb_pre, ones, 1.0 / MLA_QK_DIM)
        for j in range(MLA_HEADS):
            sl = slice(j * LANES, (j + 1) * LANES)
            kb_ref[rs, sl] = _rope_tile(kb_t[j] * gkb_ref[:, sl], half_b, cm, sdm, sum_).astype(BF16)
        csum = csum + carry["c"]
        c_ref[rs, :] = csum
        carry["c"] = csum[hm - 1:, :]

    _interleave([rows(g) for g in range(INPROJ_SPLIT)])
    carry_ref[...] = carry["c"]


def _inproj(x2d, prm, tabs, seq, tm):
    T, D = x2d.shape
    nm = prm["w_main"].shape[1]
    hm = tm // INPROJ_SPLIT
    row = lambda w: pl.BlockSpec((tm, w), lambda i: (i, 0))
    in_specs = [row(D), _full((1, D)), _full((D, nm)), _full((1, LANES)), _full((hm, hm)),
                _full((LANES, LANES)), _full((LANES, LANES)),
                _full((1, FOX_WIDTH)), _full((1, FOX_WIDTH)), _full((1, MLA_Q_RANK)), _full((1, MLA_KV_RANK)),
                _full((MLA_Q_RANK, MLA_HEADS * MLA_PAD)), _full((MLA_KV_RANK, MLA_HEADS * MLA_PAD)),
                _full((MLA_KV_RANK, MLA_WIDTH)), _full((LANES, MLA_HEADS * MLA_PAD)),
                _full((1, MLA_HEADS * MLA_PAD)), _full((1, MLA_HEADS * MLA_PAD)),
                _full((1, DIL_WIDTH)), _full((1, DIL_WIDTH))] + [row(LANES)] * 6
    outs = [(FOX_WIDTH, BF16), (FOX_WIDTH, BF16), (FOX_WIDTH, BF16), (LANES, F32),
            (MLA_HEADS * MLA_PAD, BF16), (MLA_HEADS * MLA_PAD, BF16), (MLA_WIDTH, BF16),
            (DIL_WIDTH, F32), (DIL_WIDTH, F32), (DIL_WIDTH, F32)]
    return pl.pallas_call(
        functools.partial(_inproj_kernel, tiles_per_batch=seq // tm),
        out_shape=[jax.ShapeDtypeStruct((T, w), dt) for w, dt in outs],
        grid=(T // tm,),
        in_specs=in_specs,
        out_specs=[row(w) for w, _ in outs],
        scratch_shapes=[pltpu.VMEM((1, LANES), F32)],
        compiler_params=_cparams(("arbitrary",)),
        name="inproj",
    )(x2d, prm["g_mix"], prm["w_main"], prm["b_misc"], tabs["tri_incl"], tabs["bd64"], tabs["ones"],
      prm["g_qa"], prm["g_ka"], prm["g_cq"], prm["g_ckv"], prm["w_uq"], prm["w_uk"], prm["w_uv"], tabs["rep"],
      prm["g_qb"], prm["g_kb"], prm["g_qc"], prm["g_kc"],
      tabs["cm"], tabs["sdm"], tabs["sum"], tabs["cd"], tabs["sdd"], tabs["sud"])


def _causal_kernel(*refs, seq, tq, split_heads, with_bias):
    if with_bias:
        q_ref, k_ref, v_ref, b_ref, o_ref = refs
    else:
        q_ref, k_ref, v_ref, o_ref = refs
        b_ref = None
    lane = lax.broadcasted_iota(jnp.int32, (tq, LANES), 1)
    lo_half = lane < HEAD_DIM
    r = lax.broadcasted_iota(jnp.int32, (tq, tq), 0)
    c = lax.broadcasted_iota(jnp.int32, (tq, tq), 1)
    keep = c <= r
    for qi in range(seq // tq):
        q = q_ref[0, qi * tq:(qi + 1) * tq, :]
        n_keys = (qi + 1) * tq
        v = v_ref[0, :n_keys, :]
        res = []
        for hh in range(2):
            if split_heads:
                qh = q[:, hh * LANES:(hh + 1) * LANES]
                k = k_ref[0, :n_keys, hh * LANES:(hh + 1) * LANES]
            else:
                zero = jnp.zeros_like(q)
                qh = jnp.where(lo_half, q, zero) if hh == 0 else jnp.where(lo_half, zero, q)
                k = k_ref[0, :n_keys, :]
            s = _dot_nt(qh, k)
            if with_bias:
                s = s + b_ref[0, 0, hh:hh + 1, :n_keys]
            s_diag = jnp.where(keep, s[:, n_keys - tq:], NEG_BIG)
            s = jnp.concatenate([s[:, :n_keys - tq], s_diag], axis=1) if qi else s_diag
            p = jnp.exp2(s - jnp.max(s, axis=-1, keepdims=True))
            l = jnp.sum(p, axis=-1, keepdims=True)
            res.append(_dot(p.astype(BF16), v) / l)
        o_ref[0, qi * tq:(qi + 1) * tq, :] = jnp.where(lo_half, res[0], res[1]).astype(o_ref.dtype)


def _causal_attention(q, k, v, kbias, *, tq, split_heads):
    B, S, _ = q.shape
    P = v.shape[2] // LANES
    wq = 2 * LANES if split_heads else LANES
    in_specs = [pl.BlockSpec((1, S, wq), lambda b, p: (b, 0, p)),
                pl.BlockSpec((1, S, wq), lambda b, p: (b, 0, p)),
                pl.BlockSpec((1, S, LANES), lambda b, p: (b, 0, p))]
    args = [q, k, v]
    if kbias is not None:
        in_specs.append(pl.BlockSpec((1, 1, 2, S), lambda b, p: (b, p, 0, 0)))
        args.append(kbias)
    return pl.pallas_call(
        functools.partial(_causal_kernel, seq=S, tq=tq, split_heads=split_heads, with_bias=kbias is not None),
        out_shape=jax.ShapeDtypeStruct((B, S, P * LANES), BF16),
        grid=(B, P),
        in_specs=in_specs,
        out_specs=pl.BlockSpec((1, S, LANES), lambda b, p: (b, 0, p)),
        compiler_params=_cparams(("parallel", "parallel")),
        name="causal_split" if split_heads else "causal_shared",
    )(*args)


def _dilated_kernel(q_ref, k_ref, v_ref, o_ref, acc_ref, m_ref, l_ref, *, seq, n, dilations):
    lane = lax.broadcasted_iota(jnp.int32, (1, 1, LANES), 2)
    lo_half = lane < HEAD_DIM
    row = lax.broadcasted_iota(jnp.int32, (n, n), 0)
    col = lax.broadcasted_iota(jnp.int32, (n, n), 1)
    own_keep = (col <= row)[None]
    prev_keep = (col >= row)[None]

    nblk = seq // n
    for pi, d in enumerate(dilations):
        L = seq // d
        nb = L // n

        def load(ref, d=d, L=L, nb=nb):
            if d == 1:
                return ref[0].reshape(nblk, n, LANES).astype(BF16)
            parts = [ref[0, pl.ds(r, L, stride=d), :].reshape(nb, n, LANES) for r in range(d)]
            return jnp.concatenate(parts, axis=0).astype(BF16)

        qb, kb, vb = load(q_ref), load(k_ref), load(v_ref)
        if nb > 1:
            zero = jnp.zeros((1, n, LANES), BF16)
            kp = jnp.concatenate([zero, kb[:nblk - 1]], axis=0)
            vp = jnp.concatenate([zero, vb[:nblk - 1]], axis=0)
            blk = lax.broadcasted_iota(jnp.int32, (nblk, n, n), 0)
            prev_ok = jnp.logical_and(prev_keep, lax.rem(blk, nb) != 0)
        zq = jnp.zeros_like(qb)
        heads = []
        for hh in range(2):
            qh = jnp.where(lo_half, qb, zq) if hh == 0 else jnp.where(lo_half, zq, qb)
            s_own = jnp.einsum("bqd,bkd->bqk", qh, kb, preferred_element_type=F32)
            s_own = jnp.where(own_keep, s_own, NEG_BIG)
            m = jnp.max(s_own, axis=-1, keepdims=True)
            if nb > 1:
                s_prev = jnp.einsum("bqd,bkd->bqk", qh, kp, preferred_element_type=F32)
                s_prev = jnp.where(prev_ok, s_prev, NEG_BIG)
                m = jnp.maximum(m, jnp.max(s_prev, axis=-1, keepdims=True))
            p_own = jnp.exp2(s_own - m)
            l = jnp.sum(p_own, axis=-1, keepdims=True)
            pv = jnp.einsum("bqk,bkd->bqd", p_own.astype(BF16), vb, preferred_element_type=F32)
            if nb > 1:
                p_prev = jnp.exp2(s_prev - m)
                l = l + jnp.sum(p_prev, axis=-1, keepdims=True)
                pv = pv + jnp.einsum("bqk,bkd->bqd", p_prev.astype(BF16), vp, preferred_element_type=F32)
            heads.append((m, l, pv))
        m_f = jnp.where(lo_half, heads[0][0], heads[1][0])
        l_f = jnp.where(lo_half, heads[0][1], heads[1][1])
        a_f = jnp.where(lo_half, heads[0][2], heads[1][2])
        for dst, val in ((acc_ref, a_f), (m_ref, m_f), (l_ref, l_f)):
            if d == 1:
                dst[pi] = val.reshape(seq, LANES)
            else:
                for r in range(d):
                    dst[pi, pl.ds(r, L, stride=d), :] = val[r * nb:(r + 1) * nb].reshape(L, LANES)

    np_ = len(dilations)
    m_all = m_ref[0]
    for pi in range(1, np_):
        m_all = jnp.maximum(m_all, m_ref[pi])
    num = jnp.zeros((seq, LANES), F32)
    den = jnp.zeros((seq, LANES), F32)
    for pi in range(np_):
        w = jnp.exp2(m_ref[pi] - m_all)
        num = num + w * acc_ref[pi]
        den = den + w * l_ref[pi]
    o_ref[0] = num / den


def _dilated_attention(qc, kc, vc, seq):
    B = qc.shape[0]
    n = DILATED_PATTERNS[0][0] // DILATED_PATTERNS[0][1]
    dil = tuple(d for _, d in DILATED_PATTERNS)
    for w, d in DILATED_PATTERNS:
        assert w // d == n and seq % (d * n) == 0
    spec = pl.BlockSpec((1, seq, LANES), lambda b, p: (b, 0, p))
    np_ = len(dil)
    return pl.pallas_call(
        functools.partial(_dilated_kernel, seq=seq, n=n, dilations=dil),
        out_shape=jax.ShapeDtypeStruct((B, seq, DIL_WIDTH), F32),
        grid=(B, DIL_WIDTH // LANES),
        in_specs=[spec, spec, spec],
        out_specs=spec,
        scratch_shapes=[pltpu.VMEM((np_, seq, LANES), F32)] * 3,
        compiler_params=_cparams(("parallel", "parallel")),
        name="dilated",
    )(qc, kc, vc)


def _memkv_kernel(mem_ref, g_ref, w_ref, gk_ref, bd64_ref, k_ref, v_ref):
    mn = _rms_rows(mem_ref[0], g_ref[...]).astype(BF16)
    kv = _dot(mn, w_ref[...])
    k = jnp.concatenate(_group_norm_tiles(kv[:, :MEM_WIDTH], bd64_ref[...], 1.0 / HEAD_DIM), axis=1)
    k_ref[0] = (k * gk_ref[...]).astype(BF16)
    v_ref[0] = kv[:, MEM_WIDTH:].astype(BF16)


def _memkv(mem, prm, tabs):
    B, M, D = mem.shape
    return pl.pallas_call(
        _memkv_kernel,
        out_shape=[jax.ShapeDtypeStruct((B, M, MEM_WIDTH), BF16)] * 2,
        grid=(B,),
        in_specs=[pl.BlockSpec((1, M, D), lambda b: (b, 0, 0)), _full((1, D)), _full((D, 2 * MEM_WIDTH)),
                  _full((1, MEM_WIDTH)), _full((LANES, LANES))],
        out_specs=[pl.BlockSpec((1, M, MEM_WIDTH), lambda b: (b, 0, 0))] * 2,
        compiler_params=_cparams(("parallel",)),
        name="memkv",
    )(mem, prm["g_mem"], prm["w_mkv"], prm["g_km"], tabs["bd64"])


def _postattn_kernel(x_ref, oa_ref, ob_ref, oc_ref, gg_ref, wout_ref, gx_ref, wmq_ref, gqm_ref, bd64_ref,
                     km_ref, vm_ref, wmo_ref, gf_ref, wrh_ref, wrl_ref, br_ref, tri_ref,
                     x2_ref, ti_ref, tw_ref, rk_ref, cnt_ref, carry_ref, *, n_experts):
    i = pl.program_id(0)
    tm = x_ref.shape[0]
    hm = tm // POSTATTN_SPLIT
    a0, a1 = FOX_WIDTH, FOX_WIDTH + MLA_WIDTH
    lane = lax.broadcasted_iota(jnp.int32, (hm, LANES), 1)
    lane_f = lane.astype(F32)
    lo_half = lane < HEAD_DIM
    tri = tri_ref[...]

    @pl.when(i == 0)
    def _():
        carry_ref[...] = jnp.zeros_like(carry_ref)

    carry = {"n": carry_ref[...]}

    def rows(g):
        rs = slice(g * hm, (g + 1) * hm)
        ma = _rms_rows(oa_ref[rs, :].astype(F32), gg_ref[:, :a0]).astype(BF16)
        mb = _rms_rows(ob_ref[rs, :].astype(F32), gg_ref[:, a0:a1]).astype(BF16)
        mc = _rms_rows(oc_ref[rs, :], gg_ref[:, a1:]).astype(BF16)
        yield
        x1 = (x_ref[rs, :] + _dot(ma, wout_ref[:a0, :]) + _dot(mb, wout_ref[a0:a1, :])
              + _dot(mc, wout_ref[a1:, :]))
        yield
        h2 = _rms_rows(x1, gx_ref[...]).astype(BF16)
        yield
        qm = _dot(h2, wmq_ref[...])
        yield
        qm_t = _group_norm_tiles(qm, bd64_ref[...], 1.0 / HEAD_DIM)
        outs = []
        for p in range(MEM_WIDTH // LANES):
            sl = slice(p * LANES, (p + 1) * LANES)
            q2 = (qm_t[p] * gqm_ref[:, sl]).astype(BF16)
            k2 = km_ref[0, :, sl]
            v2 = vm_ref[0, :, sl]
            zero = jnp.zeros_like(q2)
            res = []
            for hh in range(2):
                qh = jnp.where(lo_half, q2, zero) if hh == 0 else jnp.where(lo_half, zero, q2)
                s = _dot_nt(qh, k2)
                pr = jnp.exp2(s - jnp.max(s, axis=-1, keepdims=True))
                l = jnp.sum(pr, axis=-1, keepdims=True)
                res.append(_dot(pr.astype(BF16), v2) / l)
            outs.append(jnp.where(lo_half, res[0], res[1]))
            yield
        om = jnp.concatenate(outs, axis=1).astype(BF16)
        x2 = x1 + _dot(om, wmo_ref[...])
        x2_ref[rs, :] = x2
        yield
        h3 = _rms_rows(x2, gf_ref[...])
        h3_hi = h3.astype(BF16)
        h3_lo = (h3 - h3_hi.astype(F32)).astype(BF16)
        yield
        logits = _dot(h3_hi, wrh_ref[...]) + _dot(h3_hi, wrl_ref[...]) + _dot(h3_lo, wrh_ref[...]) + br_ref[...]
        yield
        logits = jnp.where(lane < n_experts, logits, NEG_BIG)
        work = logits
        sel = jnp.zeros((hm, LANES), F32)
        vals, idxs = [], []
        for _ in range(TOP_K):
            mv = jnp.max(work, axis=-1, keepdims=True)
            ix = jnp.min(jnp.where(work == mv, lane_f, float(LANES)), axis=-1, keepdims=True)
            hit = lane_f == ix
            work = jnp.where(hit, NEG_BIG, work)
            sel = jnp.where(hit, 1.0, sel)
            vals.append(mv)
            idxs.append(ix)
        es = [jnp.exp(v - vals[0]) for v in vals]
        den = es[0] + es[1] + es[2] + es[3]
        yield
        before = _dot(tri, sel.astype(BF16))
        yield
        before = before + carry["n"]
        carry["n"] = before[hm - 1:, :] + sel[hm - 1:, :]
        ti = jnp.zeros((hm, LANES), F32)
        tw = jnp.zeros((hm, LANES), F32)
        rk = jnp.zeros((hm, LANES), F32)
        for kk in range(TOP_K):
            at = lane == kk
            ti = jnp.where(at, idxs[kk], ti)
            tw = jnp.where(at, es[kk] / den, tw)
            r_k = jnp.sum(jnp.where(lane_f == idxs[kk], before, 0.0), axis=-1, keepdims=True)
            rk = jnp.where(at, r_k, rk)
        ti_ref[rs, :] = ti.astype(jnp.int32)
        tw_ref[rs, :] = tw
        rk_ref[rs, :] = rk.astype(jnp.int32)

    _interleave([rows(g) for g in range(POSTATTN_SPLIT)])
    carry_ref[...] = carry["n"]
    cnt_ref[...] = carry["n"]


def _postattn(x2d, oa, ob, oc, kmem, vmem, prm, tabs, seq, tm, n_experts):
    T, D = x2d.shape
    M = kmem.shape[1]
    tpb = seq // tm
    hm = tm // POSTATTN_SPLIT
    row = lambda w: pl.BlockSpec((tm, w), lambda i: (i, 0))
    memspec = pl.BlockSpec((1, M, MEM_WIDTH), lambda i: (i // tpb, 0, 0))
    mix = FOX_WIDTH + MLA_WIDTH + DIL_WIDTH
    return pl.pallas_call(
        functools.partial(_postattn_kernel, n_experts=n_experts),
        out_shape=[jax.ShapeDtypeStruct((T, D), F32),
                   jax.ShapeDtypeStruct((T, LANES), jnp.int32), jax.ShapeDtypeStruct((T, LANES), F32),
                   jax.ShapeDtypeStruct((T, LANES), jnp.int32), jax.ShapeDtypeStruct((1, LANES), F32)],
        grid=(T // tm,),
        in_specs=[row(D), row(FOX_WIDTH), row(MLA_WIDTH), row(DIL_WIDTH), _full((1, mix)), _full((mix, D)),
                  _full((1, D)), _full((D, MEM_WIDTH)), _full((1, MEM_WIDTH)), _full((LANES, LANES)),
                  memspec, memspec, _full((MEM_WIDTH, D)), _full((1, D)),
                  _full((D, LANES)), _full((D, LANES)), _full((1, LANES)), _full((hm, hm))],
        out_specs=[row(D), row(LANES), row(LANES), row(LANES), _full((1, LANES))],
        scratch_shapes=[pltpu.VMEM((1, LANES), F32)],
        compiler_params=_cparams(("arbitrary",)),
        name="postattn",
    )(x2d, oa, ob, oc, prm["g_group"], prm["w_out"], prm["g_xmem"], prm["w_mq"], prm["g_qm"], tabs["bd64"],
      kmem, vmem, prm["w_mo"], prm["g_ffn"], prm["w_r_hi"], prm["w_r_lo"], prm["b_router"], tabs["tri_strict"])


def _row_copy(src_ref, src_row, dst_ref, dst_row, sem):
    return pltpu.make_async_copy(src_ref.at[pl.ds(src_row, 1), :], dst_ref.at[pl.ds(dst_row, 1), :], sem)


def _dispatch_kernel(fill_ref, pos_ref, x_ref, xs_ref, zero_ref, sem, zsem, *, tm, tr):
    @pl.when(pl.program_id(0) == 0)
    def _():
        n_experts = fill_ref.shape[0] - 1
        n_rows = xs_ref.shape[0]
        zero_ref[...] = jnp.zeros_like(zero_ref)
        starts = [(fill_ref[e], fill_ref[e] >= 0) for e in range(n_experts)]
        for j in range(n_experts + 1):
            start = fill_ref[n_experts] + j * tr
            starts.append((start, start < n_rows))

        def fill(start):
            return pltpu.make_async_copy(zero_ref, xs_ref.at[pl.ds(pl.multiple_of(start, tr), tr), :], zsem)

        for start, live in starts:
            @pl.when(live)
            def _(start=start):
                fill(start).start()
        for start, live in starts:
            @pl.when(live)
            def _(start=start):
                fill(start).wait()

    def issue(r, _):
        for kk in range(TOP_K):
            _row_copy(x_ref, r, xs_ref, pos_ref[0, 0, r * TOP_K + kk], sem).start(priority=kk % 2)
        return 0
    lax.fori_loop(0, tm, issue, 0, unroll=ROW_DMA_UNROLL)
    n = tm * TOP_K
    pltpu.make_async_copy(xs_ref.at[pl.ds(0, n), :], xs_ref.at[pl.ds(0, n), :], sem).wait()


def _dispatch(x2, pos, pad_start, n_rows, tm, tr):
    T, D = x2.shape
    return pl.pallas_call(
        functools.partial(_dispatch_kernel, tm=tm, tr=tr),
        out_shape=jax.ShapeDtypeStruct((n_rows, D), x2.dtype),
        grid_spec=pltpu.PrefetchScalarGridSpec(
            num_scalar_prefetch=1,
            grid=(T // tm,),
            in_specs=[pl.BlockSpec((1, 1, tm * TOP_K), lambda i, pad: (i, 0, 0), memory_space=pltpu.SMEM),
                      pl.BlockSpec((tm, D), lambda i, pad: (i, 0))],
            out_specs=pl.BlockSpec(memory_space=pl.ANY),
            scratch_shapes=[pltpu.VMEM((tr, D), x2.dtype), pltpu.SemaphoreType.DMA, pltpu.SemaphoreType.DMA],
        ),
        compiler_params=_cparams(("arbitrary",)),
        name="dispatch",
    )(pad_start, pos.reshape(T // tm, 1, tm * TOP_K), x2)


def _experts_kernel(te_ref, nu_ref, xs_ref, gf_ref, w1_ref, b1g_ref, b1l_ref, w2_ref, b2_ref, perm_ref,
                    ys_ref, w1s_ref, w2s_ref):
    i = pl.program_id(0)
    two_f = w1_ref.shape[3]
    n_blk = two_f // DEINT_BLOCK
    half = DEINT_BLOCK // 2

    @pl.when(i < nu_ref[0])
    def _():
        @pl.when(jnp.logical_or(i == 0, te_ref[i] != te_ref[jnp.maximum(i - 1, 0)]))
        def _():
            perm = perm_ref[...]
            for b in range(n_blk):
                sl = slice(b * DEINT_BLOCK, (b + 1) * DEINT_BLOCK)
                w1s_ref[:, sl] = _dot(w1_ref[0, 0, :, sl].astype(BF16), perm).astype(BF16)
            w2s_ref[...] = w2_ref[0, 0].astype(BF16)

        hm = xs_ref.shape[0] // EXPERT_SPLIT

        def rows(g):
            rs = slice(g * hm, (g + 1) * hm)
            x = _rms_rows(xs_ref[rs, :], gf_ref[...]).astype(BF16)
            yield
            h = _dot(x, w1s_ref[...])
            yield
            acts = []
            for b in range(n_blk):
                glu = h[:, b * DEINT_BLOCK:b * DEINT_BLOCK + half] + b1g_ref[0, :, b * half:(b + 1) * half]
                lin = h[:, b * DEINT_BLOCK + half:(b + 1) * DEINT_BLOCK] + b1l_ref[0, :, b * half:(b + 1) * half]
                glu = jnp.minimum(glu, SWIGLU_LIMIT)
                lin = jnp.clip(lin, -SWIGLU_LIMIT, SWIGLU_LIMIT)
                acts.append((glu * (1.0 / (1.0 + jnp.exp(-SWIGLU_ALPHA * glu))) * (lin + 1.0)).astype(BF16))
            act = jnp.concatenate(acts, axis=1) if n_blk > 1 else acts[0]
            yield
            ys_ref[rs, :] = _dot(act, w2s_ref[...]) + b2_ref[0]

        _interleave([rows(g) for g in range(EXPERT_SPLIT)])

    @pl.when(i >= nu_ref[0])
    def _():
        ys_ref[...] = jnp.zeros_like(ys_ref)


def _experts(tile_expert, n_used, xs, w_e1, w_e2, layer, prm, tabs, tr):
    n_rows, D = xs.shape
    two_f = w_e1.shape[3]
    F = two_f // 2
    assert two_f % DEINT_BLOCK == 0
    n_tiles = n_rows // tr
    rows = pl.BlockSpec((tr, D), lambda i, te, nu: (i, 0))
    wspec = lambda s: pl.BlockSpec((1,) + s, lambda i, te, nu: (te[i], 0, 0))
    lspec = lambda s: pl.BlockSpec((1, 1) + s, lambda i, te, nu: (layer, te[i], 0, 0))
    return pl.pallas_call(
        _experts_kernel,
        out_shape=jax.ShapeDtypeStruct((n_rows, D), F32),
        grid_spec=pltpu.PrefetchScalarGridSpec(
            num_scalar_prefetch=2,
            grid=(n_tiles,),
            in_specs=[rows, pl.BlockSpec((1, D), lambda i, te, nu: (0, 0)),
                      lspec((D, two_f)), wspec((1, F)), wspec((1, F)), lspec((F, D)), wspec((1, D)),
                      pl.BlockSpec((DEINT_BLOCK, DEINT_BLOCK), lambda i, te, nu: (0, 0))],
            out_specs=rows,
            scratch_shapes=[pltpu.VMEM((D, two_f), BF16), pltpu.VMEM((F, D), BF16)],
        ),
        compiler_params=_cparams(("arbitrary",)),
        name="experts",
    )(tile_expert, n_used, xs, prm["g_ffn"], w_e1, prm["b1g"], prm["b1l"], w_e2, prm["b2"], tabs["deint"])


def _combine_kernel(pos_ref, x_ref, tw_ref, ys_ref, o_ref, buf, sem, *, tm):
    def issue(r, _):
        for kk in range(TOP_K):
            _row_copy(ys_ref, pos_ref[0, 0, r * TOP_K + kk], buf.at[kk], r, sem).start(priority=kk % 2)
        return 0
    lax.fori_loop(0, tm, issue, 0, unroll=ROW_DMA_UNROLL)
    pltpu.make_async_copy(buf, buf, sem).wait()

    tw = tw_ref[...]
    acc = x_ref[...]
    for kk in range(TOP_K):
        acc = acc + tw[:, kk:kk + 1] * buf[kk]
    o_ref[...] = acc


def _combine(x2, tw, ys, pos, tm):
    T, D = x2.shape
    W = ys.shape[1]
    return pl.pallas_call(
        functools.partial(_combine_kernel, tm=tm),
        out_shape=jax.ShapeDtypeStruct((T, D), F32),
        grid=(T // tm,),
        in_specs=[pl.BlockSpec((1, 1, tm * TOP_K), lambda i: (i, 0, 0), memory_space=pltpu.SMEM),
                  pl.BlockSpec((tm, D), lambda i: (i, 0)),
                  pl.BlockSpec((tm, LANES), lambda i: (i, 0)),
                  pl.BlockSpec(memory_space=pl.ANY)],
        out_specs=pl.BlockSpec((tm, D), lambda i: (i, 0)),
        scratch_shapes=[pltpu.VMEM((TOP_K, tm, W), ys.dtype), pltpu.SemaphoreType.DMA],
        compiler_params=_cparams(("arbitrary",)),
        name="combine",
    )(pos.reshape(T // tm, 1, tm * TOP_K), x2, tw, ys)


def _tile_gain(g, reps, scale=1.0):
    return (jnp.tile(g.astype(F32), reps) * scale)[None, :]


def _prep_layer(l, D, w_in, b_forget, g_mix, g_qa, g_ka, g_cq, g_ckv, w_uq, w_ukv, g_qb, g_kb, g_qc, g_kc,
                g_group, w_out, g_xmem, g_mem, w_mq, w_mkv, g_qm, g_km, w_mo, g_ffn, w_router, b_router,
                w_e1, b_e1, w_e2, b_e2):
    wi = w_in[l]
    sizes = (FOX_WIDTH, FOX_WIDTH, FOX_WIDTH, FOX_HEADS, MLA_Q_RANK, MLA_KV_RANK, MLA_ROPE_DIM,
             DIL_WIDTH, DIL_WIDTH, DIL_WIDTH)
    pts = np.cumsum((0,) + sizes)
    qa, ka, va, fa, cq, ckv, kr, qc, kc, vc = [wi[:, pts[j]:pts[j + 1]] for j in range(len(sizes))]
    misc = jnp.zeros((D, LANES), F32).at[:, :FOX_HEADS].set(fa).at[:, 32:32 + MLA_ROPE_DIM].set(kr)
    w_main = jnp.concatenate([qa, ka, va, cq, ckv, qc, kc, vc, misc], axis=1).astype(BF16)
    b_misc = jnp.zeros((1, LANES), F32).at[0, :FOX_HEADS].set(b_forget[l])

    uq = w_uq[l].reshape(MLA_Q_RANK, MLA_HEADS, MLA_QK_DIM)
    uq = jnp.pad(uq, ((0, 0), (0, 0), (0, MLA_PAD - MLA_QK_DIM))).reshape(MLA_Q_RANK, MLA_HEADS * MLA_PAD)
    ukv = w_ukv[l].reshape(MLA_KV_RANK, MLA_HEADS, MLA_NOPE_DIM + MLA_V_DIM)
    uk = jnp.pad(ukv[:, :, :MLA_NOPE_DIM], ((0, 0), (0, 0), (0, MLA_PAD - MLA_NOPE_DIM)))
    uk = uk.reshape(MLA_KV_RANK, MLA_HEADS * MLA_PAD)
    uv = ukv[:, :, MLA_NOPE_DIM:].reshape(MLA_KV_RANK, MLA_WIDTH)
    pad_gain = lambda g: jnp.pad(g.astype(F32), (0, MLA_PAD - MLA_QK_DIM))

    E = w_e1.shape[1]
    wr = jnp.zeros((D, LANES), F32).at[:, :E].set(w_router[l])
    wr_hi = wr.astype(BF16)
    wr_lo = (wr - wr_hi.astype(F32)).astype(BF16)
    q_scale = lambda dim: dim ** -0.5 * LOG2E
    return dict(
        g_mix=g_mix[l][None], w_main=w_main, b_misc=b_misc,
        g_qa=_tile_gain(g_qa[l], FOX_HEADS, q_scale(HEAD_DIM)), g_ka=_tile_gain(g_ka[l], FOX_HEADS),
        g_cq=g_cq[l][None], g_ckv=g_ckv[l][None],
        w_uq=uq.astype(BF16), w_uk=uk.astype(BF16), w_uv=uv.astype(BF16),
        g_qb=_tile_gain(pad_gain(g_qb[l]), MLA_HEADS, q_scale(MLA_QK_DIM)),
        g_kb=_tile_gain(pad_gain(g_kb[l]), MLA_HEADS),
        g_qc=_tile_gain(g_qc[l], DIL_HEADS, q_scale(HEAD_DIM)), g_kc=_tile_gain(g_kc[l], DIL_HEADS),
        g_group=g_group[l][None], w_out=w_out[l].astype(BF16),
        g_xmem=g_xmem[l][None], g_mem=g_mem[l][None],
        w_mq=w_mq[l].astype(BF16), w_mkv=w_mkv[l].astype(BF16),
        g_qm=_tile_gain(g_qm[l], MEM_HEADS, q_scale(HEAD_DIM)), g_km=_tile_gain(g_km[l], MEM_HEADS),
        w_mo=w_mo[l].astype(BF16), g_ffn=g_ffn[l][None],
        w_r_hi=wr_hi, w_r_lo=wr_lo,
        b_router=jnp.zeros((1, LANES), F32).at[0, :E].set(b_router[l]),
        b1g=b_e1[l][:, None, 0::2], b1l=b_e1[l][:, None, 1::2], b2=b_e2[l][:, None, :],
    )


def _rope_lane_tables(positions, rot_dim, period, offset):
    half = rot_dim // 2
    assert offset % half == 0 and period % half == 0 and LANES % half == 0
    inv = ROPE_THETA ** (-jnp.arange(0, rot_dim, 2, dtype=F32) / rot_dim)
    ang = positions.astype(F32).reshape(-1)[:, None] * inv
    cos = jnp.tile(jnp.cos(ang), (1, LANES // half))
    sin = jnp.tile(jnp.sin(ang), (1, LANES // half))
    lane = np.arange(LANES) % period - offset
    first = (lane >= 0) & (lane < half)
    second = (lane >= half) & (lane < rot_dim)
    c = jnp.where(first | second, cos, 1.0)
    s_dn = jnp.where(second, sin, 0.0)
    s_up = jnp.where(first, -sin, 0.0)
    return c, s_dn, s_up


def _tables(positions, n_incl, n_strict):
    ri, rs = np.arange(n_incl), np.arange(n_strict)
    bd = (np.arange(LANES)[:, None] // HEAD_DIM) == (np.arange(LANES)[None, :] // HEAD_DIM)
    rep = np.zeros((LANES, MLA_HEADS * MLA_PAD), np.float32)
    for h in range(MLA_HEADS):
        for j in range(MLA_ROPE_DIM):
            rep[32 + j, h * MLA_PAD + MLA_NOPE_DIM + j] = 1.0
    half = DEINT_BLOCK // 2
    deint = np.zeros((DEINT_BLOCK, DEINT_BLOCK), np.float32)
    deint[2 * np.arange(half), np.arange(half)] = 1.0
    deint[2 * np.arange(half) + 1, half + np.arange(half)] = 1.0
    cm, sdm, sum_ = _rope_lane_tables(positions, MLA_ROPE_DIM, LANES, MLA_NOPE_DIM)
    cd, sdd, sud = _rope_lane_tables(positions, ROT_DIM, HEAD_DIM, 0)
    return dict(
        tri_incl=jnp.asarray(ri[:, None] >= ri[None, :], BF16),
        tri_strict=jnp.asarray(rs[:, None] > rs[None, :], BF16),
        bd64=jnp.asarray(bd, BF16), ones=jnp.ones((LANES, LANES), BF16), rep=jnp.asarray(rep, BF16),
        deint=jnp.asarray(deint, BF16),
        cm=cm, sdm=sdm, sum=sum_, cd=cd, sdd=sdd, sud=sud,
    )


def _routing_tables(ti, rk, cnt, n_experts, tr, n_tiles):
    counts = cnt[0, :n_experts].astype(jnp.int32)
    padded = ((counts + tr - 1) // tr) * tr
    ends = jnp.cumsum(padded)
    starts = ends - padded
    idx = ti[:, :TOP_K]
    hit = idx[:, :, None] == jnp.arange(n_experts, dtype=jnp.int32)
    pos = rk[:, :TOP_K] + jnp.sum(jnp.where(hit, starts, 0), axis=-1)
    tile_start = jnp.arange(n_tiles, dtype=jnp.int32) * tr
    tile_expert = jnp.sum((ends[None, :] <= tile_start[:, None]).astype(jnp.int32), axis=1)
    tile_expert = jnp.minimum(tile_expert, n_experts - 1)
    n_used = (ends[n_experts - 1:] // tr).astype(jnp.int32)
    pad_fill = jnp.where(padded > 0, ends - tr, -1)
    fill_start = jnp.concatenate([pad_fill, ends[n_experts - 1:]])
    return pos.astype(jnp.int32), tile_expert.astype(jnp.int32), fill_start.astype(jnp.int32), n_used


def _pick_tile(seq, want):
    t = min(want, seq)
    while seq % t:
        t //= 2
    return t


def kernel(x, mem, positions, g_mix, w_in, b_forget, g_qa, g_ka, g_cq, g_ckv, w_uq, w_ukv, g_qb, g_kb, g_qc, g_kc, g_group, w_out, g_xmem, g_mem, w_mq, w_mkv, g_qm, g_km, w_mo, g_ffn, w_router, b_router, w_e1, b_e1, w_e2, b_e2):
    B, S, D = x.shape
    T = B * S
    depth = w_in.shape[0]
    E = w_e1.shape[1]
    tm_in = _pick_tile(S, INPROJ_TILE)
    tm_post = _pick_tile(S, POSTATTN_TILE)
    tq = _pick_tile(S, ATTN_TILE)
    tr = EXPERT_ROW_TILE
    tmoe = _pick_tile(S, MOE_TOKEN_TILE)
    n_rows = T * TOP_K + (E + 1) * tr
    n_tiles = n_rows // tr

    tabs = _tables(positions, tm_in // INPROJ_SPLIT, tm_post // POSTATTN_SPLIT)
    x2d = x.reshape(T, D)
    for l in range(depth):
        prm = _prep_layer(l, D, w_in, b_forget, g_mix, g_qa, g_ka, g_cq, g_ckv, w_uq, w_ukv, g_qb, g_kb,
                          g_qc, g_kc, g_group, w_out, g_xmem, g_mem, w_mq, w_mkv, g_qm, g_km, w_mo, g_ffn,
                          w_router, b_router, w_e1, b_e1, w_e2, b_e2)
        qa, ka, va, csum, qb, kb, vb, qc, kc, vc = _inproj(x2d, prm, tabs, S, tm_in)
        kbias = -LOG2E * csum[:, :FOX_HEADS].reshape(B, S, FOX_HEADS // 2, 2).transpose(0, 2, 3, 1)
        r3 = lambda a: a.reshape(B, S, a.shape[-1])
        oa = _causal_attention(r3(qa), r3(ka), r3(va), kbias, tq=tq, split_heads=False)
        ob = _causal_attention(r3(qb), r3(kb), r3(vb), None, tq=tq, split_heads=True)
        oc = _dilated_attention(r3(qc), r3(kc), r3(vc), S)
        kmem, vmem = _memkv(mem, prm, tabs)
        x2, ti, tw, rk, cnt = _postattn(x2d, oa.reshape(T, -1), ob.reshape(T, -1), oc.reshape(T, -1),
                                        kmem, vmem, prm, tabs, S, tm_post, E)
        pos, tile_expert, pad_start, n_used = _routing_tables(ti, rk, cnt, E, tr, n_tiles)
        xs = _dispatch(x2, pos, pad_start, n_rows, tmoe, tr)
        ys = _experts(tile_expert, n_used, xs, w_e1, w_e2, l, prm, tabs, tr)
        x2d = _combine(x2, tw, ys, pos, tmoe)
    return x2d.reshape(B, S, D)
```

```python
import functools
import math

import jax
import jax.numpy as jnp
import numpy as np
from jax import lax
from jax.experimental import pallas as pl
from jax.experimental.pallas import tpu as pltpu

HEAD_DIM = 64
FOX_HEADS = 4
MLA_HEADS = 6
DIL_HEADS = 6
MLA_Q_RANK = 384
MLA_KV_RANK = 256
MLA_NOPE_DIM = 64
MLA_ROPE_DIM = 32
MLA_QK_DIM = MLA_NOPE_DIM + MLA_ROPE_DIM
MLA_V_DIM = 64
ROT_DIM = HEAD_DIM // 4
ROPE_THETA = 500000.0
DILATED_PATTERNS = ((128, 1), (512, 4), (2048, 16))
FOX_WIDTH = FOX_HEADS * HEAD_DIM
MLA_WIDTH = MLA_HEADS * MLA_V_DIM
DIL_WIDTH = DIL_HEADS * HEAD_DIM
MEM_HEADS = 4
MEM_WIDTH = MEM_HEADS * HEAD_DIM
TOP_K = 4
SWIGLU_ALPHA = 1.702
SWIGLU_LIMIT = 7.0
RMS_EPS = 1e-6
LOG2E = math.log2(math.e)

LANES = 128
VMEM_LIMIT_BYTES = 56 * 1024 * 1024

INPROJ_TILE = 512
INPROJ_SPLIT = 1
POSTATTN_TILE = 1024
POSTATTN_SPLIT = 2
ATTN_TILE = 512
EXPERT_ROW_TILE = 512
EXPERT_SPLIT = 2
MOE_TOKEN_TILE = 1024
ROW_DMA_UNROLL = 8

NEG_BIG = -1e30
MLA_PAD = 128
DEINT_BLOCK = 256

F32 = jnp.float32
BF16 = jnp.bfloat16


def _cparams(sem):
    return pltpu.CompilerParams(dimension_semantics=sem, vmem_limit_bytes=VMEM_LIMIT_BYTES)


def _full(shape):
    nd = len(shape)
    return pl.BlockSpec(shape, lambda *_: (0,) * nd)


def _dot(a, b):
    return jnp.dot(a, b, preferred_element_type=F32)


def _dot_nt(a, b):
    return lax.dot_general(a, b, (((1,), (1,)), ((), ())), preferred_element_type=F32)


def _rms_rows(x, g):
    return x * lax.rsqrt(jnp.mean(x * x, axis=-1, keepdims=True) + RMS_EPS) * g


def _group_norm_tiles(y, ones_mat, inv_n):
    outs = []
    for j in range(y.shape[1] // LANES):
        yj = y[:, j * LANES:(j + 1) * LANES]
        ss = _dot((yj * yj).astype(BF16), ones_mat)
        outs.append(yj * lax.rsqrt(ss * inv_n + RMS_EPS))
    return outs


def _rope_tile(y, half, c, s_dn, s_up):
    return y * c + pltpu.roll(y, half, 1) * s_dn + pltpu.roll(y, LANES - half, 1) * s_up


def _interleave(gens):
    live = list(gens)
    while live:
        for g in list(live):
            try:
                next(g)
            except StopIteration:
                live.remove(g)


def _inproj_kernel(x_ref, gmix_ref, w_ref, bmisc_ref, tri_ref, bd64_ref, ones_ref,
                   gqa_ref, gka_ref, gcq_ref, gckv_ref, wuq_ref, wuk_ref, wuv_ref, rep_ref,
                   gqb_ref, gkb_ref, gqc_ref, gkc_ref,
                   cm_ref, sdm_ref, sum_ref, cd_ref, sdd_ref, sud_ref,
                   qa_ref, ka_ref, va_ref, c_ref, qb_ref, kb_ref, vb_ref, qc_ref, kc_ref, vc_ref,
                   carry_ref, *, tiles_per_batch):
    i = pl.program_id(0)
    tm = x_ref.shape[0]
    hm = tm // INPROJ_SPLIT
    bd64 = bd64_ref[...]
    ones = ones_ref[...]
    tri = tri_ref[...]
    half_b = MLA_ROPE_DIM // 2
    half_c = ROT_DIM // 2
    offs = np.cumsum((0, FOX_WIDTH, FOX_WIDTH, FOX_WIDTH, MLA_Q_RANK, MLA_KV_RANK,
                      DIL_WIDTH, DIL_WIDTH, DIL_WIDTH, LANES))

    @pl.when(i % tiles_per_batch == 0)
    def _():
        carry_ref[...] = jnp.zeros_like(carry_ref)

    carry = {"c": carry_ref[...]}

    def rows(g):
        rs = slice(g * hm, (g + 1) * hm)
        h = _rms_rows(x_ref[rs, :], gmix_ref[...]).astype(BF16)
        seg = lambda j: _dot(h, w_ref[:, offs[j]:offs[j + 1]])
        yield
        qa, ka, va, cq = seg(0), seg(1), seg(2), seg(3)
        yield
        qa_ref[rs, :] = (jnp.concatenate(_group_norm_tiles(qa, bd64, 1.0 / HEAD_DIM), axis=1)
                         * gqa_ref[...]).astype(BF16)
        ka_ref[rs, :] = (jnp.concatenate(_group_norm_tiles(ka, bd64, 1.0 / HEAD_DIM), axis=1)
                         * gka_ref[...]).astype(BF16)
        va_ref[rs, :] = va.astype(BF16)
        cqn = _rms_rows(cq, gcq_ref[...]).astype(BF16)
        yield
        ckv, qc = seg(4), seg(5)
        qb_pre = _dot(cqn, wuq_ref[...])
        yield
        ckvn = _rms_rows(ckv, gckv_ref[...]).astype(BF16)
        cd, sdd, sud = cd_ref[rs, :], sdd_ref[rs, :], sud_ref[rs, :]
        qc_t = _group_norm_tiles(qc, bd64, 1.0 / HEAD_DIM)
        for j in range(DIL_WIDTH // LANES):
            sl = slice(j * LANES, (j + 1) * LANES)
            qc_ref[rs, sl] = _rope_tile(qc_t[j] * gqc_ref[:, sl], half_c, cd, sdd, sud)
        yield
        kc, vc, misc = seg(6), seg(7), seg(8)
        kb_nope = _dot(ckvn, wuk_ref[...])
        vb_ref[rs, :] = _dot(ckvn, wuv_ref[...]).astype(BF16)
        yield
        cm, sdm, sum_ = cm_ref[rs, :], sdm_ref[rs, :], sum_ref[rs, :]
        qb_t = _group_norm_tiles(qb_pre, ones, 1.0 / MLA_QK_DIM)
        for j in range(MLA_HEADS):
            sl = slice(j * LANES, (j + 1) * LANES)
            qb_ref[rs, sl] = _rope_tile(qb_t[j] * gqb_ref[:, sl], half_b, cm, sdm, sum_).astype(BF16)
        kc_t = _group_norm_tiles(kc, bd64, 1.0 / HEAD_DIM)
        for j in range(DIL_WIDTH // LANES):
            sl = slice(j * LANES, (j + 1) * LANES)
            kc_ref[rs, sl] = _rope_tile(kc_t[j] * gkc_ref[:, sl], half_c, cd, sdd, sud)
        vc_ref[rs, :] = vc
        z = misc + bmisc_ref[...]
        logf = jnp.minimum(z, 0.0) - jnp.log(1.0 + jnp.exp(-jnp.abs(z)))
        lane = lax.broadcasted_iota(jnp.int32, logf.shape, 1)
        logf = jnp.where(lane < FOX_HEADS, logf, 0.0)
        p0 = logf.astype(BF16)
        r0 = logf - p0.astype(F32)
        p1 = r0.astype(BF16)
        p2 = (r0 - p1.astype(F32)).astype(BF16)
        yield
        csum = _dot(tri, p0) + _dot(tri, p1) + _dot(tri, p2)
        kb_pre = kb_nope + _dot(misc.astype(BF16), rep_ref[...])
        yield
        kb_t = _group_norm_tiles(kb_pre, ones, 1.0 / MLA_QK_DIM)
        for j in range(MLA_HEADS):
            sl = slice(j * LANES, (j + 1) * LANES)
            kb_ref[rs, sl] = _rope_tile(kb_t[j] * gkb_ref[:, sl], half_b, cm, sdm, sum_).astype(BF16)
        csum = csum + carry["c"]
        c_ref[rs, :] = csum
        carry["c"] = csum[hm - 1:, :]

    _interleave([rows(g) for g in range(INPROJ_SPLIT)])
    carry_ref[...] = carry["c"]


def _inproj(x2d, prm, tabs, seq, tm):
    T, D = x2d.shape
    nm = prm["w_main"].shape[1]
    hm = tm // INPROJ_SPLIT
    row = lambda w: pl.BlockSpec((tm, w), lambda i: (i, 0))
    in_specs = [row(D), _full((1, D)), _full((D, nm)), _full((1, LANES)), _full((hm, hm)),
                _full((LANES, LANES)), _full((LANES, LANES)),
                _full((1, FOX_WIDTH)), _full((1, FOX_WIDTH)), _full((1, MLA_Q_RANK)), _full((1, MLA_KV_RANK)),
                _full((MLA_Q_RANK, MLA_HEADS * MLA_PAD)), _full((MLA_KV_RANK, MLA_HEADS * MLA_PAD)),
                _full((MLA_KV_RANK, MLA_WIDTH)), _full((LANES, MLA_HEADS * MLA_PAD)),
                _full((1, MLA_HEADS * MLA_PAD)), _full((1, MLA_HEADS * MLA_PAD)),
                _full((1, DIL_WIDTH)), _full((1, DIL_WIDTH))] + [row(LANES)] * 6
    outs = [(FOX_WIDTH, BF16), (FOX_WIDTH, BF16), (FOX_WIDTH, BF16), (LANES, F32),
            (MLA_HEADS * MLA_PAD, BF16), (MLA_HEADS * MLA_PAD, BF16), (MLA_WIDTH, BF16),
            (DIL_WIDTH, F32), (DIL_WIDTH, F32), (DIL_WIDTH, F32)]
    return pl.pallas_call(
        functools.partial(_inproj_kernel, tiles_per_batch=seq // tm),
        out_shape=[jax.ShapeDtypeStruct((T, w), dt) for w, dt in outs],
        grid=(T // tm,),
        in_specs=in_specs,
        out_specs=[row(w) for w, _ in outs],
        scratch_shapes=[pltpu.VMEM((1, LANES), F32)],
        compiler_params=_cparams(("arbitrary",)),
        name="inproj",
    )(x2d, prm["g_mix"], prm["w_main"], prm["b_misc"], tabs["tri_incl"], tabs["bd64"], tabs["ones"],
      prm["g_qa"], prm["g_ka"], prm["g_cq"], prm["g_ckv"], prm["w_uq"], prm["w_uk"], prm["w_uv"], tabs["rep"],
      prm["g_qb"], prm["g_kb"], prm["g_qc"], prm["g_kc"],
      tabs["cm"], tabs["sdm"], tabs["sum"], tabs["cd"], tabs["sdd"], tabs["sud"])


def _causal_kernel(*refs, seq, tq, split_heads, with_bias):
    if with_bias:
        q_ref, k_ref, v_ref, b_ref, o_ref = refs
    else:
        q_ref, k_ref, v_ref, o_ref = refs
        b_ref = None
    lane = lax.broadcasted_iota(jnp.int32, (tq, LANES), 1)
    lo_half = lane < HEAD_DIM
    r = lax.broadcasted_iota(jnp.int32, (tq, tq), 0)
    c = lax.broadcasted_iota(jnp.int32, (tq, tq), 1)
    keep = c <= r
    for qi in range(seq // tq):
        q = q_ref[0, qi * tq:(qi + 1) * tq, :]
        n_keys = (qi + 1) * tq
        v = v_ref[0, :n_keys, :]
        res = [None, None]

        def head(hh, qi=qi, q=q, n_keys=n_keys, v=v, res=res):
            if split_heads:
                qh = q[:, hh * LANES:(hh + 1) * LANES]
                k = k_ref[0, :n_keys, hh * LANES:(hh + 1) * LANES]
            else:
                zero = jnp.zeros_like(q)
                qh = jnp.where(lo_half, q, zero) if hh == 0 else jnp.where(lo_half, zero, q)
                k = k_ref[0, :n_keys, :]
            s = _dot_nt(qh, k)
            yield
            if with_bias:
                s = s + b_ref[0, 0, hh:hh + 1, :n_keys]
            s_diag = jnp.where(keep, s[:, n_keys - tq:], NEG_BIG)
            s = jnp.concatenate([s[:, :n_keys - tq], s_diag], axis=1) if qi else s_diag
            p = jnp.exp2(s - jnp.max(s, axis=-1, keepdims=True))
            l = jnp.sum(p, axis=-1, keepdims=True)
            yield
            res[hh] = _dot(p.astype(BF16), v) / l

        _interleave([head(0), head(1)])
        o_ref[0, qi * tq:(qi + 1) * tq, :] = jnp.where(lo_half, res[0], res[1]).astype(o_ref.dtype)


def _causal_attention(q, k, v, kbias, *, tq, split_heads):
    B, S, _ = q.shape
    P = v.shape[2] // LANES
    wq = 2 * LANES if split_heads else LANES
    in_specs = [pl.BlockSpec((1, S, wq), lambda b, p: (b, 0, p)),
                pl.BlockSpec((1, S, wq), lambda b, p: (b, 0, p)),
                pl.BlockSpec((1, S, LANES), lambda b, p: (b, 0, p))]
    args = [q, k, v]
    if kbias is not None:
        in_specs.append(pl.BlockSpec((1, 1, 2, S), lambda b, p: (b, p, 0, 0)))
        args.append(kbias)
    return pl.pallas_call(
        functools.partial(_causal_kernel, seq=S, tq=tq, split_heads=split_heads, with_bias=kbias is not None),
        out_shape=jax.ShapeDtypeStruct((B, S, P * LANES), BF16),
        grid=(B, P),
        in_specs=in_specs,
        out_specs=pl.BlockSpec((1, S, LANES), lambda b, p: (b, 0, p)),
        compiler_params=_cparams(("parallel", "parallel")),
        name="causal_split" if split_heads else "causal_shared",
    )(*args)


def _dilated_kernel(q_ref, k_ref, v_ref, o_ref, acc_ref, m_ref, l_ref, *, seq, n, dilations):
    lane = lax.broadcasted_iota(jnp.int32, (1, 1, LANES), 2)
    lo_half = lane < HEAD_DIM
    row = lax.broadcasted_iota(jnp.int32, (n, n), 0)
    col = lax.broadcasted_iota(jnp.int32, (n, n), 1)
    own_keep = (col <= row)[None]
    prev_keep = (col >= row)[None]

    nblk = seq // n
    for pi, d in enumerate(dilations):
        L = seq // d
        nb = L // n

        def load(ref, d=d, L=L, nb=nb):
            if d == 1:
                return ref[0].reshape(nblk, n, LANES).astype(BF16)
            parts = [ref[0, pl.ds(r, L, stride=d), :].reshape(nb, n, LANES) for r in range(d)]
            return jnp.concatenate(parts, axis=0).astype(BF16)

        qb, kb, vb = load(q_ref), load(k_ref), load(v_ref)
        if nb > 1:
            zero = jnp.zeros((1, n, LANES), BF16)
            kp = jnp.concatenate([zero, kb[:nblk - 1]], axis=0)
            vp = jnp.concatenate([zero, vb[:nblk - 1]], axis=0)
            blk = lax.broadcasted_iota(jnp.int32, (nblk, n, n), 0)
            prev_ok = jnp.logical_and(prev_keep, lax.rem(blk, nb) != 0)
        zq = jnp.zeros_like(qb)
        heads = []
        for hh in range(2):
            qh = jnp.where(lo_half, qb, zq) if hh == 0 else jnp.where(lo_half, zq, qb)
            s_own = jnp.einsum("bqd,bkd->bqk", qh, kb, preferred_element_type=F32)
            s_own = jnp.where(own_keep, s_own, NEG_BIG)
            m = jnp.max(s_own, axis=-1, keepdims=True)
            if nb > 1:
                s_prev = jnp.einsum("bqd,bkd->bqk", qh, kp, preferred_element_type=F32)
                s_prev = jnp.where(prev_ok, s_prev, NEG_BIG)
                m = jnp.maximum(m, jnp.max(s_prev, axis=-1, keepdims=True))
            p_own = jnp.exp2(s_own - m)
            l = jnp.sum(p_own, axis=-1, keepdims=True)
            pv = jnp.einsum("bqk,bkd->bqd", p_own.astype(BF16), vb, preferred_element_type=F32)
            if nb > 1:
                p_prev = jnp.exp2(s_prev - m)
                l = l + jnp.sum(p_prev, axis=-1, keepdims=True)
                pv = pv + jnp.einsum("bqk,bkd->bqd", p_prev.astype(BF16), vp, preferred_element_type=F32)
            heads.append((m, l, pv))
        m_f = jnp.where(lo_half, heads[0][0], heads[1][0])
        l_f = jnp.where(lo_half, heads[0][1], heads[1][1])
        a_f = jnp.where(lo_half, heads[0][2], heads[1][2])
        for dst, val in ((acc_ref, a_f), (m_ref, m_f), (l_ref, l_f)):
            if d == 1:
                dst[pi] = val.reshape(seq, LANES)
            else:
                for r in range(d):
                    dst[pi, pl.ds(r, L, stride=d), :] = val[r * nb:(r + 1) * nb].reshape(L, LANES)

    np_ = len(dilations)
    m_all = m_ref[0]
    for pi in range(1, np_):
        m_all = jnp.maximum(m_all, m_ref[pi])
    num = jnp.zeros((seq, LANES), F32)
    den = jnp.zeros((seq, LANES), F32)
    for pi in range(np_):
        w = jnp.exp2(m_ref[pi] - m_all)
        num = num + w * acc_ref[pi]
        den = den + w * l_ref[pi]
    o_ref[0] = num / den


def _dilated_attention(qc, kc, vc, seq):
    B = qc.shape[0]
    n = DILATED_PATTERNS[0][0] // DILATED_PATTERNS[0][1]
    dil = tuple(d for _, d in DILATED_PATTERNS)
    for w, d in DILATED_PATTERNS:
        assert w // d == n and seq % (d * n) == 0
    spec = pl.BlockSpec((1, seq, LANES), lambda b, p: (b, 0, p))
    np_ = len(dil)
    return pl.pallas_call(
        functools.partial(_dilated_kernel, seq=seq, n=n, dilations=dil),
        out_shape=jax.ShapeDtypeStruct((B, seq, DIL_WIDTH), F32),
        grid=(B, DIL_WIDTH // LANES),
        in_specs=[spec, spec, spec],
        out_specs=spec,
        scratch_shapes=[pltpu.VMEM((np_, seq, LANES), F32)] * 3,
        compiler_params=_cparams(("parallel", "parallel")),
        name="dilated",
    )(qc, kc, vc)


def _memkv_kernel(mem_ref, g_ref, w_ref, gk_ref, bd64_ref, k_ref, v_ref):
    mn = _rms_rows(mem_ref[0], g_ref[...]).astype(BF16)
    kv = _dot(mn, w_ref[...])
    k = jnp.concatenate(_group_norm_tiles(kv[:, :MEM_WIDTH], bd64_ref[...], 1.0 / HEAD_DIM), axis=1)
    k_ref[0] = (k * gk_ref[...]).astype(BF16)
    v_ref[0] = kv[:, MEM_WIDTH:].astype(BF16)


def _memkv(mem, prm, tabs):
    B, M, D = mem.shape
    return pl.pallas_call(
        _memkv_kernel,
        out_shape=[jax.ShapeDtypeStruct((B, M, MEM_WIDTH), BF16)] * 2,
        grid=(B,),
        in_specs=[pl.BlockSpec((1, M, D), lambda b: (b, 0, 0)), _full((1, D)), _full((D, 2 * MEM_WIDTH)),
                  _full((1, MEM_WIDTH)), _full((LANES, LANES))],
        out_specs=[pl.BlockSpec((1, M, MEM_WIDTH), lambda b: (b, 0, 0))] * 2,
        compiler_params=_cparams(("parallel",)),
        name="memkv",
    )(mem, prm["g_mem"], prm["w_mkv"], prm["g_km"], tabs["bd64"])


def _postattn_kernel(x_ref, oa_ref, ob_ref, oc_ref, gg_ref, wout_ref, gx_ref, wmq_ref, gqm_ref, bd64_ref,
                     km_ref, vm_ref, wmo_ref, gf_ref, wrh_ref, wrl_ref, br_ref, tri_ref,
                     x2_ref, ti_ref, tw_ref, rk_ref, cnt_ref, carry_ref, *, n_experts):
    i = pl.program_id(0)
    tm = x_ref.shape[0]
    hm = tm // POSTATTN_SPLIT
    a0, a1 = FOX_WIDTH, FOX_WIDTH + MLA_WIDTH
    lane = lax.broadcasted_iota(jnp.int32, (hm, LANES), 1)
    lane_f = lane.astype(F32)
    lo_half = lane < HEAD_DIM
    tri = tri_ref[...]

    @pl.when(i == 0)
    def _():
        carry_ref[...] = jnp.zeros_like(carry_ref)

    carry = {"n": carry_ref[...]}

    def rows(g):
        rs = slice(g * hm, (g + 1) * hm)
        ma = _rms_rows(oa_ref[rs, :].astype(F32), gg_ref[:, :a0]).astype(BF16)
        mb = _rms_rows(ob_ref[rs, :].astype(F32), gg_ref[:, a0:a1]).astype(BF16)
        mc = _rms_rows(oc_ref[rs, :], gg_ref[:, a1:]).astype(BF16)
        yield
        x1 = (x_ref[rs, :] + _dot(ma, wout_ref[:a0, :]) + _dot(mb, wout_ref[a0:a1, :])
              + _dot(mc, wout_ref[a1:, :]))
        yield
        h2 = _rms_rows(x1, gx_ref[...]).astype(BF16)
        yield
        qm = _dot(h2, wmq_ref[...])
        yield
        qm_t = _group_norm_tiles(qm, bd64_ref[...], 1.0 / HEAD_DIM)
        outs = []
        for p in range(MEM_WIDTH // LANES):
            sl = slice(p * LANES, (p + 1) * LANES)
            q2 = (qm_t[p] * gqm_ref[:, sl]).astype(BF16)
            k2 = km_ref[0, :, sl]
            v2 = vm_ref[0, :, sl]
            zero = jnp.zeros_like(q2)
            res = []
            for hh in range(2):
                qh = jnp.where(lo_half, q2, zero) if hh == 0 else jnp.where(lo_half, zero, q2)
                s = _dot_nt(qh, k2)
                pr = jnp.exp2(s - jnp.max(s, axis=-1, keepdims=True))
                l = jnp.sum(pr, axis=-1, keepdims=True)
                res.append(_dot(pr.astype(BF16), v2) / l)
            outs.append(jnp.where(lo_half, res[0], res[1]))
            yield
        om = jnp.concatenate(outs, axis=1).astype(BF16)
        x2 = x1 + _dot(om, wmo_ref[...])
        x2_ref[rs, :] = x2
        yield
        h3 = _rms_rows(x2, gf_ref[...])
        h3_hi = h3.astype(BF16)
        h3_lo = (h3 - h3_hi.astype(F32)).astype(BF16)
        yield
        logits = _dot(h3_hi, wrh_ref[...]) + _dot(h3_hi, wrl_ref[...]) + _dot(h3_lo, wrh_ref[...]) + br_ref[...]
        yield
        logits = jnp.where(lane < n_experts, logits, NEG_BIG)
        work = logits
        sel = jnp.zeros((hm, LANES), F32)
        vals, idxs = [], []
        for _ in range(TOP_K):
            mv = jnp.max(work, axis=-1, keepdims=True)
            ix = jnp.min(jnp.where(work == mv, lane_f, float(LANES)), axis=-1, keepdims=True)
            hit = lane_f == ix
            work = jnp.where(hit, NEG_BIG, work)
            sel = jnp.where(hit, 1.0, sel)
            vals.append(mv)
            idxs.append(ix)
        es = [jnp.exp(v - vals[0]) for v in vals]
        den = es[0] + es[1] + es[2] + es[3]
        yield
        before = _dot(tri, sel.astype(BF16))
        yield
        before = before + carry["n"]
        carry["n"] = before[hm - 1:, :] + sel[hm - 1:, :]
        ti = jnp.zeros((hm, LANES), F32)
        tw = jnp.zeros((hm, LANES), F32)
        rk = jnp.zeros((hm, LANES), F32)
        for kk in range(TOP_K):
            at = lane == kk
            ti = jnp.where(at, idxs[kk], ti)
            tw = jnp.where(at, es[kk] / den, tw)
            r_k = jnp.sum(jnp.where(lane_f == idxs[kk], before, 0.0), axis=-1, keepdims=True)
            rk = jnp.where(at, r_k, rk)
        ti_ref[rs, :] = ti.astype(jnp.int32)
        tw_ref[rs, :] = tw
        rk_ref[rs, :] = rk.astype(jnp.int32)

    _interleave([rows(g) for g in range(POSTATTN_SPLIT)])
    carry_ref[...] = carry["n"]
    cnt_ref[...] = carry["n"]


def _postattn(x2d, oa, ob, oc, kmem, vmem, prm, tabs, seq, tm, n_experts):
    T, D = x2d.shape
    M = kmem.shape[1]
    tpb = seq // tm
    hm = tm // POSTATTN_SPLIT
    row = lambda w: pl.BlockSpec((tm, w), lambda i: (i, 0))
    memspec = pl.BlockSpec((1, M, MEM_WIDTH), lambda i: (i // tpb, 0, 0))
    mix = FOX_WIDTH + MLA_WIDTH + DIL_WIDTH
    return pl.pallas_call(
        functools.partial(_postattn_kernel, n_experts=n_experts),
        out_shape=[jax.ShapeDtypeStruct((T, D), F32),
                   jax.ShapeDtypeStruct((T, LANES), jnp.int32), jax.ShapeDtypeStruct((T, LANES), F32),
                   jax.ShapeDtypeStruct((T, LANES), jnp.int32), jax.ShapeDtypeStruct((1, LANES), F32)],
        grid=(T // tm,),
        in_specs=[row(D), row(FOX_WIDTH), row(MLA_WIDTH), row(DIL_WIDTH), _full((1, mix)), _full((mix, D)),
                  _full((1, D)), _full((D, MEM_WIDTH)), _full((1, MEM_WIDTH)), _full((LANES, LANES)),
                  memspec, memspec, _full((MEM_WIDTH, D)), _full((1, D)),
                  _full((D, LANES)), _full((D, LANES)), _full((1, LANES)), _full((hm, hm))],
        out_specs=[row(D), row(LANES), row(LANES), row(LANES), _full((1, LANES))],
        scratch_shapes=[pltpu.VMEM((1, LANES), F32)],
        compiler_params=_cparams(("arbitrary",)),
        name="postattn",
    )(x2d, oa, ob, oc, prm["g_group"], prm["w_out"], prm["g_xmem"], prm["w_mq"], prm["g_qm"], tabs["bd64"],
      kmem, vmem, prm["w_mo"], prm["g_ffn"], prm["w_r_hi"], prm["w_r_lo"], prm["b_router"], tabs["tri_strict"])


def _row_copy(src_ref, src_row, dst_ref, dst_row, sem):
    return pltpu.make_async_copy(src_ref.at[pl.ds(src_row, 1), :], dst_ref.at[pl.ds(dst_row, 1), :], sem)


def _dispatch_kernel(fill_ref, pos_ref, x_ref, xs_ref, zero_ref, sem, zsem, *, tm, tr):
    @pl.when(pl.program_id(0) == 0)
    def _():
        n_experts = fill_ref.shape[0] - 1
        n_rows = xs_ref.shape[0]
        zero_ref[...] = jnp.zeros_like(zero_ref)
        starts = [(fill_ref[e], fill_ref[e] >= 0) for e in range(n_experts)]
        for j in range(n_experts + 1):
            start = fill_ref[n_experts] + j * tr
            starts.append((start, start < n_rows))

        def fill(start):
            return pltpu.make_async_copy(zero_ref, xs_ref.at[pl.ds(pl.multiple_of(start, tr), tr), :], zsem)

        for start, live in starts:
            @pl.when(live)
            def _(start=start):
                fill(start).start()
        for start, live in starts:
            @pl.when(live)
            def _(start=start):
                fill(start).wait()

    def issue(r, _):
        for kk in range(TOP_K):
            _row_copy(x_ref, r, xs_ref, pos_ref[0, 0, r * TOP_K + kk], sem).start(priority=kk % 2)
        return 0
    lax.fori_loop(0, tm, issue, 0, unroll=ROW_DMA_UNROLL)
    n = tm * TOP_K
    pltpu.make_async_copy(xs_ref.at[pl.ds(0, n), :], xs_ref.at[pl.ds(0, n), :], sem).wait()


def _dispatch(x2, pos, pad_start, n_rows, tm, tr):
    T, D = x2.shape
    return pl.pallas_call(
        functools.partial(_dispatch_kernel, tm=tm, tr=tr),
        out_shape=jax.ShapeDtypeStruct((n_rows, D), x2.dtype),
        grid_spec=pltpu.PrefetchScalarGridSpec(
            num_scalar_prefetch=1,
            grid=(T // tm,),
            in_specs=[pl.BlockSpec((1, 1, tm * TOP_K), lambda i, pad: (i, 0, 0), memory_space=pltpu.SMEM),
                      pl.BlockSpec((tm, D), lambda i, pad: (i, 0))],
            out_specs=pl.BlockSpec(memory_space=pl.ANY),
            scratch_shapes=[pltpu.VMEM((tr, D), x2.dtype), pltpu.SemaphoreType.DMA, pltpu.SemaphoreType.DMA],
        ),
        compiler_params=_cparams(("arbitrary",)),
        name="dispatch",
    )(pad_start, pos.reshape(T // tm, 1, tm * TOP_K), x2)


def _experts_kernel(te_ref, nu_ref, xs_ref, gf_ref, w1_ref, b1g_ref, b1l_ref, w2_ref, b2_ref, perm_ref,
                    ys_ref, w1s_ref, w2s_ref):
    i = pl.program_id(0)
    two_f = w1_ref.shape[3]
    n_blk = two_f // DEINT_BLOCK
    half = DEINT_BLOCK // 2

    @pl.when(i < nu_ref[0])
    def _():
        @pl.when(jnp.logical_or(i == 0, te_ref[i] != te_ref[jnp.maximum(i - 1, 0)]))
        def _():
            perm = perm_ref[...]
            for b in range(n_blk):
                sl = slice(b * DEINT_BLOCK, (b + 1) * DEINT_BLOCK)
                w1s_ref[:, sl] = _dot(w1_ref[0, 0, :, sl].astype(BF16), perm).astype(BF16)
            w2s_ref[...] = w2_ref[0, 0].astype(BF16)

        hm = xs_ref.shape[0] // EXPERT_SPLIT

        def rows(g):
            rs = slice(g * hm, (g + 1) * hm)
            x = _rms_rows(xs_ref[rs, :], gf_ref[...]).astype(BF16)
            yield
            h = _dot(x, w1s_ref[...])
            yield
            acts = []
            for b in range(n_blk):
                glu = h[:, b * DEINT_BLOCK:b * DEINT_BLOCK + half] + b1g_ref[0, :, b * half:(b + 1) * half]
                lin = h[:, b * DEINT_BLOCK + half:(b + 1) * DEINT_BLOCK] + b1l_ref[0, :, b * half:(b + 1) * half]
                glu = jnp.minimum(glu, SWIGLU_LIMIT)
                lin = jnp.clip(lin, -SWIGLU_LIMIT, SWIGLU_LIMIT)
                acts.append((glu * (1.0 / (1.0 + jnp.exp(-SWIGLU_ALPHA * glu))) * (lin + 1.0)).astype(BF16))
            act = jnp.concatenate(acts, axis=1) if n_blk > 1 else acts[0]
            yield
            ys_ref[rs, :] = _dot(act, w2s_ref[...]) + b2_ref[0]

        _interleave([rows(g) for g in range(EXPERT_SPLIT)])

    @pl.when(i >= nu_ref[0])
    def _():
        ys_ref[...] = jnp.zeros_like(ys_ref)


def _experts(tile_expert, n_used, xs, w_e1, w_e2, layer, prm, tabs, tr):
    n_rows, D = xs.shape
    two_f = w_e1.shape[3]
    F = two_f // 2
    assert two_f % DEINT_BLOCK == 0
    n_tiles = n_rows // tr
    rows = pl.BlockSpec((tr, D), lambda i, te, nu: (i, 0))
    wspec = lambda s: pl.BlockSpec((1,) + s, lambda i, te, nu: (te[i], 0, 0))
    lspec = lambda s: pl.BlockSpec((1, 1) + s, lambda i, te, nu: (layer, te[i], 0, 0))
    return pl.pallas_call(
        _experts_kernel,
        out_shape=jax.ShapeDtypeStruct((n_rows, D), F32),
        grid_spec=pltpu.PrefetchScalarGridSpec(
            num_scalar_prefetch=2,
            grid=(n_tiles,),
            in_specs=[rows, pl.BlockSpec((1, D), lambda i, te, nu: (0, 0)),
                      lspec((D, two_f)), wspec((1, F)), wspec((1, F)), lspec((F, D)), wspec((1, D)),
                      pl.BlockSpec((DEINT_BLOCK, DEINT_BLOCK), lambda i, te, nu: (0, 0))],
            out_specs=rows,
            scratch_shapes=[pltpu.VMEM((D, two_f), BF16), pltpu.VMEM((F, D), BF16)],
        ),
        compiler_params=_cparams(("arbitrary",)),
        name="experts",
    )(tile_expert, n_used, xs, prm["g_ffn"], w_e1, prm["b1g"], prm["b1l"], w_e2, prm["b2"], tabs["deint"])


def _combine_kernel(pos_ref, x_ref, tw_ref, ys_ref, o_ref, buf, sem, *, tm):
    def issue(r, _):
        for kk in range(TOP_K):
            _row_copy(ys_ref, pos_ref[0, 0, r * TOP_K + kk], buf.at[kk], r, sem).start(priority=kk % 2)
        return 0
    lax.fori_loop(0, tm, issue, 0, unroll=ROW_DMA_UNROLL)
    pltpu.make_async_copy(buf, buf, sem).wait()

    tw = tw_ref[...]
    acc = x_ref[...]
    for kk in range(TOP_K):
        acc = acc + tw[:, kk:kk + 1] * buf[kk]
    o_ref[...] = acc


def _combine(x2, tw, ys, pos, tm):
    T, D = x2.shape
    W = ys.shape[1]
    return pl.pallas_call(
        functools.partial(_combine_kernel, tm=tm),
        out_shape=jax.ShapeDtypeStruct((T, D), F32),
        grid=(T // tm,),
        in_specs=[pl.BlockSpec((1, 1, tm * TOP_K), lambda i: (i, 0, 0), memory_space=pltpu.SMEM),
                  pl.BlockSpec((tm, D), lambda i: (i, 0)),
                  pl.BlockSpec((tm, LANES), lambda i: (i, 0)),
                  pl.BlockSpec(memory_space=pl.ANY)],
        out_specs=pl.BlockSpec((tm, D), lambda i: (i, 0)),
        scratch_shapes=[pltpu.VMEM((TOP_K, tm, W), ys.dtype), pltpu.SemaphoreType.DMA],
        compiler_params=_cparams(("arbitrary",)),
        name="combine",
    )(pos.reshape(T // tm, 1, tm * TOP_K), x2, tw, ys)


def _tile_gain(g, reps, scale=1.0):
    return (jnp.tile(g.astype(F32), reps) * scale)[None, :]


def _prep_layer(l, D, w_in, b_forget, g_mix, g_qa, g_ka, g_cq, g_ckv, w_uq, w_ukv, g_qb, g_kb, g_qc, g_kc,
                g_group, w_out, g_xmem, g_mem, w_mq, w_mkv, g_qm, g_km, w_mo, g_ffn, w_router, b_router,
                w_e1, b_e1, w_e2, b_e2):
    wi = w_in[l]
    sizes = (FOX_WIDTH, FOX_WIDTH, FOX_WIDTH, FOX_HEADS, MLA_Q_RANK, MLA_KV_RANK, MLA_ROPE_DIM,
             DIL_WIDTH, DIL_WIDTH, DIL_WIDTH)
    pts = np.cumsum((0,) + sizes)
    qa, ka, va, fa, cq, ckv, kr, qc, kc, vc = [wi[:, pts[j]:pts[j + 1]] for j in range(len(sizes))]
    misc = jnp.zeros((D, LANES), F32).at[:, :FOX_HEADS].set(fa).at[:, 32:32 + MLA_ROPE_DIM].set(kr)
    w_main = jnp.concatenate([qa, ka, va, cq, ckv, qc, kc, vc, misc], axis=1).astype(BF16)
    b_misc = jnp.zeros((1, LANES), F32).at[0, :FOX_HEADS].set(b_forget[l])

    uq = w_uq[l].reshape(MLA_Q_RANK, MLA_HEADS, MLA_QK_DIM)
    uq = jnp.pad(uq, ((0, 0), (0, 0), (0, MLA_PAD - MLA_QK_DIM))).reshape(MLA_Q_RANK, MLA_HEADS * MLA_PAD)
    ukv = w_ukv[l].reshape(MLA_KV_RANK, MLA_HEADS, MLA_NOPE_DIM + MLA_V_DIM)
    uk = jnp.pad(ukv[:, :, :MLA_NOPE_DIM], ((0, 0), (0, 0), (0, MLA_PAD - MLA_NOPE_DIM)))
    uk = uk.reshape(MLA_KV_RANK, MLA_HEADS * MLA_PAD)
    uv = ukv[:, :, MLA_NOPE_DIM:].reshape(MLA_KV_RANK, MLA_WIDTH)
    pad_gain = lambda g: jnp.pad(g.astype(F32), (0, MLA_PAD - MLA_QK_DIM))

    E = w_e1.shape[1]
    wr = jnp.zeros((D, LANES), F32).at[:, :E].set(w_router[l])
    wr_hi = wr.astype(BF16)
    wr_lo = (wr - wr_hi.astype(F32)).astype(BF16)
    q_scale = lambda dim: dim ** -0.5 * LOG2E
    return dict(
        g_mix=g_mix[l][None], w_main=w_main, b_misc=b_misc,
        g_qa=_tile_gain(g_qa[l], FOX_HEADS, q_scale(HEAD_DIM)), g_ka=_tile_gain(g_ka[l], FOX_HEADS),
        g_cq=g_cq[l][None], g_ckv=g_ckv[l][None],
        w_uq=uq.astype(BF16), w_uk=uk.astype(BF16), w_uv=uv.astype(BF16),
        g_qb=_tile_gain(pad_gain(g_qb[l]), MLA_HEADS, q_scale(MLA_QK_DIM)),
        g_kb=_tile_gain(pad_gain(g_kb[l]), MLA_HEADS),
        g_qc=_tile_gain(g_qc[l], DIL_HEADS, q_scale(HEAD_DIM)), g_kc=_tile_gain(g_kc[l], DIL_HEADS),
        g_group=g_group[l][None], w_out=w_out[l].astype(BF16),
        g_xmem=g_xmem[l][None], g_mem=g_mem[l][None],
        w_mq=w_mq[l].astype(BF16), w_mkv=w_mkv[l].astype(BF16),
        g_qm=_tile_gain(g_qm[l], MEM_HEADS, q_scale(HEAD_DIM)), g_km=_tile_gain(g_km[l], MEM_HEADS),
        w_mo=w_mo[l].astype(BF16), g_ffn=g_ffn[l][None],
        w_r_hi=wr_hi, w_r_lo=wr_lo,
        b_router=jnp.zeros((1, LANES), F32).at[0, :E].set(b_router[l]),
        b1g=b_e1[l][:, None, 0::2], b1l=b_e1[l][:, None, 1::2], b2=b_e2[l][:, None, :],
    )


def _rope_lane_tables(positions, rot_dim, period, offset):
    half = rot_dim // 2
    assert offset % half == 0 and period % half == 0 and LANES % half == 0
    inv = ROPE_THETA ** (-jnp.arange(0, rot_dim, 2, dtype=F32) / rot_dim)
    ang = positions.astype(F32).reshape(-1)[:, None] * inv
    cos = jnp.tile(jnp.cos(ang), (1, LANES // half))
    sin = jnp.tile(jnp.sin(ang), (1, LANES // half))
    lane = np.arange(LANES) % period - offset
    first = (lane >= 0) & (lane < half)
    second = (lane >= half) & (lane < rot_dim)
    c = jnp.where(first | second, cos, 1.0)
    s_dn = jnp.where(second, sin, 0.0)
    s_up = jnp.where(first, -sin, 0.0)
    return c, s_dn, s_up


def _tables(positions, n_incl, n_strict):
    ri, rs = np.arange(n_incl), np.arange(n_strict)
    bd = (np.arange(LANES)[:, None] // HEAD_DIM) == (np.arange(LANES)[None, :] // HEAD_DIM)
    rep = np.zeros((LANES, MLA_HEADS * MLA_PAD), np.float32)
    for h in range(MLA_HEADS):
        for j in range(MLA_ROPE_DIM):
            rep[32 + j, h * MLA_PAD + MLA_NOPE_DIM + j] = 1.0
    half = DEINT_BLOCK // 2
    deint = np.zeros((DEINT_BLOCK, DEINT_BLOCK), np.float32)
    deint[2 * np.arange(half), np.arange(half)] = 1.0
    deint[2 * np.arange(half) + 1, half + np.arange(half)] = 1.0
    cm, sdm, sum_ = _rope_lane_tables(positions, MLA_ROPE_DIM, LANES, MLA_NOPE_DIM)
    cd, sdd, sud = _rope_lane_tables(positions, ROT_DIM, HEAD_DIM, 0)
    return dict(
        tri_incl=jnp.asarray(ri[:, None] >= ri[None, :], BF16),
        tri_strict=jnp.asarray(rs[:, None] > rs[None, :], BF16),
        bd64=jnp.asarray(bd, BF16), ones=jnp.ones((LANES, LANES), BF16), rep=jnp.asarray(rep, BF16),
        deint=jnp.asarray(deint, BF16),
        cm=cm, sdm=sdm, sum=sum_, cd=cd, sdd=sdd, sud=sud,
    )


def _routing_tables(ti, rk, cnt, n_experts, tr, n_tiles):
    counts = cnt[0, :n_experts].astype(jnp.int32)
    padded = ((counts + tr - 1) // tr) * tr
    ends = jnp.cumsum(padded)
    starts = ends - padded
    idx = ti[:, :TOP_K]
    hit = idx[:, :, None] == jnp.arange(n_experts, dtype=jnp.int32)
    pos = rk[:, :TOP_K] + jnp.sum(jnp.where(hit, starts, 0), axis=-1)
    tile_start = jnp.arange(n_tiles, dtype=jnp.int32) * tr
    tile_expert = jnp.sum((ends[None, :] <= tile_start[:, None]).astype(jnp.int32), axis=1)
    tile_expert = jnp.minimum(tile_expert, n_experts - 1)
    n_used = (ends[n_experts - 1:] // tr).astype(jnp.int32)
    pad_fill = jnp.where(padded > 0, ends - tr, -1)
    fill_start = jnp.concatenate([pad_fill, ends[n_experts - 1:]])
    return pos.astype(jnp.int32), tile_expert.astype(jnp.int32), fill_start.astype(jnp.int32), n_used


def _pick_tile(seq, want):
    t = min(want, seq)
    while seq % t:
        t //= 2
    return t


def kernel(x, mem, positions, g_mix, w_in, b_forget, g_qa, g_ka, g_cq, g_ckv, w_uq, w_ukv, g_qb, g_kb, g_qc, g_kc, g_group, w_out, g_xmem, g_mem, w_mq, w_mkv, g_qm, g_km, w_mo, g_ffn, w_router, b_router, w_e1, b_e1, w_e2, b_e2):
    B, S, D = x.shape
    T = B * S
    depth = w_in.shape[0]
    E = w_e1.shape[1]
    tm_in = _pick_tile(S, INPROJ_TILE)
    tm_post = _pick_tile(S, POSTATTN_TILE)
    tq = _pick_tile(S, ATTN_TILE)
    tr = EXPERT_ROW_TILE
    tmoe = _pick_tile(S, MOE_TOKEN_TILE)
    n_rows = T * TOP_K + (E + 1) * tr
    n_tiles = n_rows // tr

    tabs = _tables(positions, tm_in // INPROJ_SPLIT, tm_post // POSTATTN_SPLIT)
    x2d = x.reshape(T, D)
    for l in range(depth):
        prm = _prep_layer(l, D, w_in, b_forget, g_mix, g_qa, g_ka, g_cq, g_ckv, w_uq, w_ukv, g_qb, g_kb,
                          g_qc, g_kc, g_group, w_out, g_xmem, g_mem, w_mq, w_mkv, g_qm, g_km, w_mo, g_ffn,
                          w_router, b_router, w_e1, b_e1, w_e2, b_e2)
        qa, ka, va, csum, qb, kb, vb, qc, kc, vc = _inproj(x2d, prm, tabs, S, tm_in)
        kbias = -LOG2E * csum[:, :FOX_HEADS].reshape(B, S, FOX_HEADS // 2, 2).transpose(0, 2, 3, 1)
        r3 = lambda a: a.reshape(B, S, a.shape[-1])
        oa = _causal_attention(r3(qa), r3(ka), r3(va), kbias, tq=tq, split_heads=False)
        ob = _causal_attention(r3(qb), r3(kb), r3(vb), None, tq=tq, split_heads=True)
        oc = _dilated_attention(r3(qc), r3(kc), r3(vc), S)
        kmem, vmem = _memkv(mem, prm, tabs)
        x2, ti, tw, rk, cnt = _postattn(x2d, oa.reshape(T, -1), ob.reshape(T, -1), oc.reshape(T, -1),
                                        kmem, vmem, prm, tabs, S, tm_post, E)
        pos, tile_expert, pad_start, n_used = _routing_tables(ti, rk, cnt, E, tr, n_tiles)
        xs = _dispatch(x2, pos, pad_start, n_rows, tmoe, tr)
        ys = _experts(tile_expert, n_used, xs, w_e1, w_e2, l, prm, tabs, tr)
        x2d = _combine(x2, tw, ys, pos, tmoe)
    return x2d.reshape(B, S, D)
```

```python
import functools
import math

import jax
import jax.numpy as jnp
import numpy as np
from jax import lax
from jax.experimental import pallas as pl
from jax.experimental.pallas import tpu as pltpu

HEAD_DIM = 64
FOX_HEADS = 4
MLA_HEADS = 6
DIL_HEADS = 6
MLA_Q_RANK = 384
MLA_KV_RANK = 256
MLA_NOPE_DIM = 64
MLA_ROPE_DIM = 32
MLA_QK_DIM = MLA_NOPE_DIM + MLA_ROPE_DIM
MLA_V_DIM = 64
ROT_DIM = HEAD_DIM // 4
ROPE_THETA = 500000.0
DILATED_PATTERNS = ((128, 1), (512, 4), (2048, 16))
FOX_WIDTH = FOX_HEADS * HEAD_DIM
MLA_WIDTH = MLA_HEADS * MLA_V_DIM
DIL_WIDTH = DIL_HEADS * HEAD_DIM
MEM_HEADS = 4
MEM_WIDTH = MEM_HEADS * HEAD_DIM
TOP_K = 4
SWIGLU_ALPHA = 1.702
SWIGLU_LIMIT = 7.0
RMS_EPS = 1e-6
LOG2E = math.log2(math.e)

LANES = 128
VMEM_LIMIT_BYTES = 56 * 1024 * 1024

INPROJ_TILE = 512
INPROJ_SPLIT = 1
POSTATTN_TILE = 1024
POSTATTN_SPLIT = 2
ATTN_TILE = 512
EXPERT_ROW_TILE = 512
EXPERT_SPLIT = 2
MOE_TOKEN_TILE = 1024
ROW_DMA_UNROLL = 8

NEG_BIG = -1e30
MLA_PAD = 128
DEINT_BLOCK = 256

F32 = jnp.float32
BF16 = jnp.bfloat16


def _cparams(sem):
    return pltpu.CompilerParams(dimension_semantics=sem, vmem_limit_bytes=VMEM_LIMIT_BYTES)


def _full(shape):
    nd = len(shape)
    return pl.BlockSpec(shape, lambda *_: (0,) * nd)


def _dot(a, b):
    return jnp.dot(a, b, preferred_element_type=F32)


def _dot_nt(a, b):
    return lax.dot_general(a, b, (((1,), (1,)), ((), ())), preferred_element_type=F32)


def _rms_rows(x, g):
    return x * lax.rsqrt(jnp.mean(x * x, axis=-1, keepdims=True) + RMS_EPS) * g


def _group_norm_tiles(y, ones_mat, inv_n):
    outs = []
    for j in range(y.shape[1] // LANES):
        yj = y[:, j * LANES:(j + 1) * LANES]
        ss = _dot((yj * yj).astype(BF16), ones_mat)
        outs.append(yj * lax.rsqrt(ss * inv_n + RMS_EPS))
    return outs


def _rope_tile(y, half, c, s_dn, s_up):
    return y * c + pltpu.roll(y, half, 1) * s_dn + pltpu.roll(y, LANES - half, 1) * s_up


def _interleave(gens):
    live = list(gens)
    while live:
        for g in list(live):
            try:
                next(g)
            except StopIteration:
                live.remove(g)


def _stagger(gens):
    pending = list(gens)
    live = []
    while pending or live:
        if pending:
            live.append(pending.pop(0))
        for g in list(live):
            try:
                next(g)
            except StopIteration:
                live.remove(g)


def _inproj_kernel(x_ref, gmix_ref, w_ref, bmisc_ref, tri_ref, bd64_ref, ones_ref,
                   gqa_ref, gka_ref, gcq_ref, gckv_ref, wuq_ref, wuk_ref, wuv_ref, rep_ref,
                   gqb_ref, gkb_ref, gqc_ref, gkc_ref,
                   cm_ref, sdm_ref, sum_ref, cd_ref, sdd_ref, sud_ref,
                   qa_ref, ka_ref, va_ref, c_ref, qb_ref, kb_ref, vb_ref, qc_ref, kc_ref, vc_ref,
                   carry_ref, *, tiles_per_batch):
    i = pl.program_id(0)
    tm = x_ref.shape[0]
    hm = tm // INPROJ_SPLIT
    bd64 = bd64_ref[...]
    ones = ones_ref[...]
    tri = tri_ref[...]
    half_b = MLA_ROPE_DIM // 2
    half_c = ROT_DIM // 2
    offs = np.cumsum((0, FOX_WIDTH, FOX_WIDTH, FOX_WIDTH, MLA_Q_RANK, MLA_KV_RANK,
                      DIL_WIDTH, DIL_WIDTH, DIL_WIDTH, LANES))

    @pl.when(i % tiles_per_batch == 0)
    def _():
        carry_ref[...] = jnp.zeros_like(carry_ref)

    carry = {"c": carry_ref[...]}

    def rows(g):
        rs = slice(g * hm, (g + 1) * hm)
        h = _rms_rows(x_ref[rs, :], gmix_ref[...]).astype(BF16)
        seg = lambda j: _dot(h, w_ref[:, offs[j]:offs[j + 1]])
        yield
        qa, ka, va, cq = seg(0), seg(1), seg(2), seg(3)
        yield
        qa_ref[rs, :] = (jnp.concatenate(_group_norm_tiles(qa, bd64, 1.0 / HEAD_DIM), axis=1)
                         * gqa_ref[...]).astype(BF16)
        ka_ref[rs, :] = (jnp.concatenate(_group_norm_tiles(ka, bd64, 1.0 / HEAD_DIM), axis=1)
                         * gka_ref[...]).astype(BF16)
        va_ref[rs, :] = va.astype(BF16)
        cqn = _rms_rows(cq, gcq_ref[...]).astype(BF16)
        yield
        ckv, qc = seg(4), seg(5)
        qb_pre = _dot(cqn, wuq_ref[...])
        yield
        ckvn = _rms_rows(ckv, gckv_ref[...]).astype(BF16)
        cd, sdd, sud = cd_ref[rs, :], sdd_ref[rs, :], sud_ref[rs, :]
        qc_t = _group_norm_tiles(qc, bd64, 1.0 / HEAD_DIM)
        for j in range(DIL_WIDTH // LANES):
            sl = slice(j * LANES, (j + 1) * LANES)
            qc_ref[rs, sl] = _rope_tile(qc_t[j] * gqc_ref[:, sl], half_c, cd, sdd, sud)
        yield
        kc, vc, misc = seg(6), seg(7), seg(8)
        kb_nope = _dot(ckvn, wuk_ref[...])
        vb_ref[rs, :] = _dot(ckvn, wuv_ref[...]).astype(BF16)
        yield
        cm, sdm, sum_ = cm_ref[rs, :], sdm_ref[rs, :], sum_ref[rs, :]
        qb_t = _group_norm_tiles(qb_pre, ones, 1.0 / MLA_QK_DIM)
        for j in range(MLA_HEADS):
            sl = slice(j * LANES, (j + 1) * LANES)
            qb_ref[rs, sl] = _rope_tile(qb_t[j] * gqb_ref[:, sl], half_b, cm, sdm, sum_).astype(BF16)
        kc_t = _group_norm_tiles(kc, bd64, 1.0 / HEAD_DIM)
        for j in range(DIL_WIDTH // LANES):
            sl = slice(j * LANES, (j + 1) * LANES)
            kc_ref[rs, sl] = _rope_tile(kc_t[j] * gkc_ref[:, sl], half_c, cd, sdd, sud)
        vc_ref[rs, :] = vc
        z = misc + bmisc_ref[...]
        logf = jnp.minimum(z, 0.0) - jnp.log(1.0 + jnp.exp(-jnp.abs(z)))
        lane = lax.broadcasted_iota(jnp.int32, logf.shape, 1)
        logf = jnp.where(lane < FOX_HEADS, logf, 0.0)
        p0 = logf.astype(BF16)
        r0 = logf - p0.astype(F32)
        p1 = r0.astype(BF16)
        p2 = (r0 - p1.astype(F32)).astype(BF16)
        yield
        csum = _dot(tri, p0) + _dot(tri, p1) + _dot(tri, p2)
        kb_pre = kb_nope + _dot(misc.astype(BF16), rep_ref[...])
        yield
        kb_t = _group_norm_tiles(kb_pre, ones, 1.0 / MLA_QK_DIM)
        for j in range(MLA_HEADS):
            sl = slice(j * LANES, (j + 1) * LANES)
            kb_ref[rs, sl] = _rope_tile(kb_t[j] * gkb_ref[:, sl], half_b, cm, sdm, sum_).astype(BF16)
        csum = csum + carry["c"]
        c_ref[rs, :] = csum
        carry["c"] = csum[hm - 1:, :]

    _interleave([rows(g) for g in range(INPROJ_SPLIT)])
    carry_ref[...] = carry["c"]


def _inproj(x2d, prm, tabs, seq, tm):
    T, D = x2d.shape
    nm = prm["w_main"].shape[1]
    hm = tm // INPROJ_SPLIT
    row = lambda w: pl.BlockSpec((tm, w), lambda i: (i, 0))
    in_specs = [row(D), _full((1, D)), _full((D, nm)), _full((1, LANES)), _full((hm, hm)),
                _full((LANES, LANES)), _full((LANES, LANES)),
                _full((1, FOX_WIDTH)), _full((1, FOX_WIDTH)), _full((1, MLA_Q_RANK)), _full((1, MLA_KV_RANK)),
                _full((MLA_Q_RANK, MLA_HEADS * MLA_PAD)), _full((MLA_KV_RANK, MLA_HEADS * MLA_PAD)),
                _full((MLA_KV_RANK, MLA_WIDTH)), _full((LANES, MLA_HEADS * MLA_PAD)),
                _full((1, MLA_HEADS * MLA_PAD)), _full((1, MLA_HEADS * MLA_PAD)),
                _full((1, DIL_WIDTH)), _full((1, DIL_WIDTH))] + [row(LANES)] * 6
    outs = [(FOX_WIDTH, BF16), (FOX_WIDTH, BF16), (FOX_WIDTH, BF16), (LANES, F32),
            (MLA_HEADS * MLA_PAD, BF16), (MLA_HEADS * MLA_PAD, BF16), (MLA_WIDTH, BF16),
            (DIL_WIDTH, F32), (DIL_WIDTH, F32), (DIL_WIDTH, F32)]
    return pl.pallas_call(
        functools.partial(_inproj_kernel, tiles_per_batch=seq // tm),
        out_shape=[jax.ShapeDtypeStruct((T, w), dt) for w, dt in outs],
        grid=(T // tm,),
        in_specs=in_specs,
        out_specs=[row(w) for w, _ in outs],
        scratch_shapes=[pltpu.VMEM((1, LANES), F32)],
        compiler_params=_cparams(("arbitrary",)),
        name="inproj",
    )(x2d, prm["g_mix"], prm["w_main"], prm["b_misc"], tabs["tri_incl"], tabs["bd64"], tabs["ones"],
      prm["g_qa"], prm["g_ka"], prm["g_cq"], prm["g_ckv"], prm["w_uq"], prm["w_uk"], prm["w_uv"], tabs["rep"],
      prm["g_qb"], prm["g_kb"], prm["g_qc"], prm["g_kc"],
      tabs["cm"], tabs["sdm"], tabs["sum"], tabs["cd"], tabs["sdd"], tabs["sud"])


def _causal_kernel(*refs, seq, tq, split_heads, with_bias):
    if with_bias:
        q_ref, k_ref, v_ref, b_ref, o_ref = refs
    else:
        q_ref, k_ref, v_ref, o_ref = refs
        b_ref = None
    lane = lax.broadcasted_iota(jnp.int32, (tq, LANES), 1)
    lo_half = lane < HEAD_DIM
    r = lax.broadcasted_iota(jnp.int32, (tq, tq), 0)
    c = lax.broadcasted_iota(jnp.int32, (tq, tq), 1)
    keep = c <= r
    streams = []
    for qi in range(seq // tq):
        q = q_ref[0, qi * tq:(qi + 1) * tq, :]
        n_keys = (qi + 1) * tq
        v = v_ref[0, :n_keys, :]
        res = [None, None]

        def head(hh, qi=qi, q=q, n_keys=n_keys, v=v, res=res):
            if split_heads:
                qh = q[:, hh * LANES:(hh + 1) * LANES]
                k = k_ref[0, :n_keys, hh * LANES:(hh + 1) * LANES]
            else:
                zero = jnp.zeros_like(q)
                qh = jnp.where(lo_half, q, zero) if hh == 0 else jnp.where(lo_half, zero, q)
                k = k_ref[0, :n_keys, :]
            s = _dot_nt(qh, k)
            yield
            if with_bias:
                s = s + b_ref[0, 0, hh:hh + 1, :n_keys]
            s_diag = jnp.where(keep, s[:, n_keys - tq:], NEG_BIG)
            s = jnp.concatenate([s[:, :n_keys - tq], s_diag], axis=1) if qi else s_diag
            p = jnp.exp2(s - jnp.max(s, axis=-1, keepdims=True))
            l = jnp.sum(p, axis=-1, keepdims=True)
            yield
            res[hh] = _dot(p.astype(BF16), v) / l
            if hh == 1:
                o_ref[0, qi * tq:(qi + 1) * tq, :] = jnp.where(lo_half, res[0], res[1]).astype(o_ref.dtype)

        streams += [head(0), head(1)]
    if split_heads:
        _stagger(streams)
    else:
        for j in range(0, len(streams), 2):
            _interleave(streams[j:j + 2])


def _causal_attention(q, k, v, kbias, *, tq, split_heads):
    B, S, _ = q.shape
    P = v.shape[2] // LANES
    wq = 2 * LANES if split_heads else LANES
    in_specs = [pl.BlockSpec((1, S, wq), lambda b, p: (b, 0, p)),
                pl.BlockSpec((1, S, wq), lambda b, p: (b, 0, p)),
                pl.BlockSpec((1, S, LANES), lambda b, p: (b, 0, p))]
    args = [q, k, v]
    if kbias is not None:
        in_specs.append(pl.BlockSpec((1, 1, 2, S), lambda b, p: (b, p, 0, 0)))
        args.append(kbias)
    return pl.pallas_call(
        functools.partial(_causal_kernel, seq=S, tq=tq, split_heads=split_heads, with_bias=kbias is not None),
        out_shape=jax.ShapeDtypeStruct((B, S, P * LANES), BF16),
        grid=(B, P),
        in_specs=in_specs,
        out_specs=pl.BlockSpec((1, S, LANES), lambda b, p: (b, 0, p)),
        compiler_params=_cparams(("parallel", "parallel")),
        name="causal_split" if split_heads else "causal_shared",
    )(*args)


def _dilated_kernel(q_ref, k_ref, v_ref, o_ref, acc_ref, m_ref, l_ref, *, seq, n, dilations):
    lane = lax.broadcasted_iota(jnp.int32, (1, 1, LANES), 2)
    lo_half = lane < HEAD_DIM
    row = lax.broadcasted_iota(jnp.int32, (n, n), 0)
    col = lax.broadcasted_iota(jnp.int32, (n, n), 1)
    own_keep = (col <= row)[None]
    prev_keep = (col >= row)[None]

    nblk = seq // n
    for pi, d in enumerate(dilations):
        L = seq // d
        nb = L // n

        def load(ref, d=d, L=L, nb=nb):
            if d == 1:
                return ref[0].reshape(nblk, n, LANES).astype(BF16)
            parts = [ref[0, pl.ds(r, L, stride=d), :].reshape(nb, n, LANES) for r in range(d)]
            return jnp.concatenate(parts, axis=0).astype(BF16)

        qb, kb, vb = load(q_ref), load(k_ref), load(v_ref)
        if nb > 1:
            zero = jnp.zeros((1, n, LANES), BF16)
            kp = jnp.concatenate([zero, kb[:nblk - 1]], axis=0)
            vp = jnp.concatenate([zero, vb[:nblk - 1]], axis=0)
            blk = lax.broadcasted_iota(jnp.int32, (nblk, n, n), 0)
            prev_ok = jnp.logical_and(prev_keep, lax.rem(blk, nb) != 0)
        zq = jnp.zeros_like(qb)
        heads = []
        for hh in range(2):
            qh = jnp.where(lo_half, qb, zq) if hh == 0 else jnp.where(lo_half, zq, qb)
            s_own = jnp.einsum("bqd,bkd->bqk", qh, kb, preferred_element_type=F32)
            s_own = jnp.where(own_keep, s_own, NEG_BIG)
            m = jnp.max(s_own, axis=-1, keepdims=True)
            if nb > 1:
                s_prev = jnp.einsum("bqd,bkd->bqk", qh, kp, preferred_element_type=F32)
                s_prev = jnp.where(prev_ok, s_prev, NEG_BIG)
                m = jnp.maximum(m, jnp.max(s_prev, axis=-1, keepdims=True))
            p_own = jnp.exp2(s_own - m)
            l = jnp.sum(p_own, axis=-1, keepdims=True)
            pv = jnp.einsum("bqk,bkd->bqd", p_own.astype(BF16), vb, preferred_element_type=F32)
            if nb > 1:
                p_prev = jnp.exp2(s_prev - m)
                l = l + jnp.sum(p_prev, axis=-1, keepdims=True)
                pv = pv + jnp.einsum("bqk,bkd->bqd", p_prev.astype(BF16), vp, preferred_element_type=F32)
            heads.append((m, l, pv))
        m_f = jnp.where(lo_half, heads[0][0], heads[1][0])
        l_f = jnp.where(lo_half, heads[0][1], heads[1][1])
        a_f = jnp.where(lo_half, heads[0][2], heads[1][2])
        for dst, val in ((acc_ref, a_f), (m_ref, m_f), (l_ref, l_f)):
            if d == 1:
                dst[pi] = val.reshape(seq, LANES)
            else:
                for r in range(d):
                    dst[pi, pl.ds(r, L, stride=d), :] = val[r * nb:(r + 1) * nb].reshape(L, LANES)

    np_ = len(dilations)
    m_all = m_ref[0]
    for pi in range(1, np_):
        m_all = jnp.maximum(m_all, m_ref[pi])
    num = jnp.zeros((seq, LANES), F32)
    den = jnp.zeros((seq, LANES), F32)
    for pi in range(np_):
        w = jnp.exp2(m_ref[pi] - m_all)
        num = num + w * acc_ref[pi]
        den = den + w * l_ref[pi]
    o_ref[0] = num / den


def _dilated_attention(qc, kc, vc, seq):
    B = qc.shape[0]
    n = DILATED_PATTERNS[0][0] // DILATED_PATTERNS[0][1]
    dil = tuple(d for _, d in DILATED_PATTERNS)
    for w, d in DILATED_PATTERNS:
        assert w // d == n and seq % (d * n) == 0
    spec = pl.BlockSpec((1, seq, LANES), lambda b, p: (b, 0, p))
    np_ = len(dil)
    return pl.pallas_call(
        functools.partial(_dilated_kernel, seq=seq, n=n, dilations=dil),
        out_shape=jax.ShapeDtypeStruct((B, seq, DIL_WIDTH), F32),
        grid=(B, DIL_WIDTH // LANES),
        in_specs=[spec, spec, spec],
        out_specs=spec,
        scratch_shapes=[pltpu.VMEM((np_, seq, LANES), F32)] * 3,
        compiler_params=_cparams(("parallel", "parallel")),
        name="dilated",
    )(qc, kc, vc)


def _memkv_kernel(mem_ref, g_ref, w_ref, gk_ref, bd64_ref, k_ref, v_ref):
    mn = _rms_rows(mem_ref[0], g_ref[...]).astype(BF16)
    kv = _dot(mn, w_ref[...])
    k = jnp.concatenate(_group_norm_tiles(kv[:, :MEM_WIDTH], bd64_ref[...], 1.0 / HEAD_DIM), axis=1)
    k_ref[0] = (k * gk_ref[...]).astype(BF16)
    v_ref[0] = kv[:, MEM_WIDTH:].astype(BF16)


def _memkv(mem, prm, tabs):
    B, M, D = mem.shape
    return pl.pallas_call(
        _memkv_kernel,
        out_shape=[jax.ShapeDtypeStruct((B, M, MEM_WIDTH), BF16)] * 2,
        grid=(B,),
        in_specs=[pl.BlockSpec((1, M, D), lambda b: (b, 0, 0)), _full((1, D)), _full((D, 2 * MEM_WIDTH)),
                  _full((1, MEM_WIDTH)), _full((LANES, LANES))],
        out_specs=[pl.BlockSpec((1, M, MEM_WIDTH), lambda b: (b, 0, 0))] * 2,
        compiler_params=_cparams(("parallel",)),
        name="memkv",
    )(mem, prm["g_mem"], prm["w_mkv"], prm["g_km"], tabs["bd64"])


def _postattn_kernel(x_ref, oa_ref, ob_ref, oc_ref, gg_ref, wout_ref, gx_ref, wmq_ref, gqm_ref, bd64_ref,
                     km_ref, vm_ref, wmo_ref, gf_ref, wrh_ref, wrl_ref, br_ref, tri_ref,
                     x2_ref, ti_ref, tw_ref, rk_ref, cnt_ref, carry_ref, *, n_experts):
    i = pl.program_id(0)
    tm = x_ref.shape[0]
    hm = tm // POSTATTN_SPLIT
    a0, a1 = FOX_WIDTH, FOX_WIDTH + MLA_WIDTH
    lane = lax.broadcasted_iota(jnp.int32, (hm, LANES), 1)
    lane_f = lane.astype(F32)
    lo_half = lane < HEAD_DIM
    tri = tri_ref[...]

    @pl.when(i == 0)
    def _():
        carry_ref[...] = jnp.zeros_like(carry_ref)

    carry = {"n": carry_ref[...]}

    def rows(g):
        rs = slice(g * hm, (g + 1) * hm)
        ma = _rms_rows(oa_ref[rs, :].astype(F32), gg_ref[:, :a0]).astype(BF16)
        mb = _rms_rows(ob_ref[rs, :].astype(F32), gg_ref[:, a0:a1]).astype(BF16)
        mc = _rms_rows(oc_ref[rs, :], gg_ref[:, a1:]).astype(BF16)
        yield
        x1 = (x_ref[rs, :] + _dot(ma, wout_ref[:a0, :]) + _dot(mb, wout_ref[a0:a1, :])
              + _dot(mc, wout_ref[a1:, :]))
        yield
        h2 = _rms_rows(x1, gx_ref[...]).astype(BF16)
        yield
        qm = _dot(h2, wmq_ref[...])
        yield
        qm_t = _group_norm_tiles(qm, bd64_ref[...], 1.0 / HEAD_DIM)
        outs = []
        for p in range(MEM_WIDTH // LANES):
            sl = slice(p * LANES, (p + 1) * LANES)
            q2 = (qm_t[p] * gqm_ref[:, sl]).astype(BF16)
            k2 = km_ref[0, :, sl]
            v2 = vm_ref[0, :, sl]
            zero = jnp.zeros_like(q2)
            res = []
            for hh in range(2):
                qh = jnp.where(lo_half, q2, zero) if hh == 0 else jnp.where(lo_half, zero, q2)
                s = _dot_nt(qh, k2)
                pr = jnp.exp2(s - jnp.max(s, axis=-1, keepdims=True))
                l = jnp.sum(pr, axis=-1, keepdims=True)
                res.append(_dot(pr.astype(BF16), v2) / l)
            outs.append(jnp.where(lo_half, res[0], res[1]))
            yield
        om = jnp.concatenate(outs, axis=1).astype(BF16)
        x2 = x1 + _dot(om, wmo_ref[...])
        x2_ref[rs, :] = x2
        yield
        h3 = _rms_rows(x2, gf_ref[...])
        h3_hi = h3.astype(BF16)
        h3_lo = (h3 - h3_hi.astype(F32)).astype(BF16)
        yield
        logits = _dot(h3_hi, wrh_ref[...]) + _dot(h3_hi, wrl_ref[...]) + _dot(h3_lo, wrh_ref[...]) + br_ref[...]
        yield
        logits = jnp.where(lane < n_experts, logits, NEG_BIG)
        work = logits
        sel = jnp.zeros((hm, LANES), F32)
        vals, idxs = [], []
        for _ in range(TOP_K):
            mv = jnp.max(work, axis=-1, keepdims=True)
            ix = jnp.min(jnp.where(work == mv, lane_f, float(LANES)), axis=-1, keepdims=True)
            hit = lane_f == ix
            work = jnp.where(hit, NEG_BIG, work)
            sel = jnp.where(hit, 1.0, sel)
            vals.append(mv)
            idxs.append(ix)
        es = [jnp.exp(v - vals[0]) for v in vals]
        den = es[0] + es[1] + es[2] + es[3]
        yield
        before = _dot(tri, sel.astype(BF16))
        yield
        before = before + carry["n"]
        carry["n"] = before[hm - 1:, :] + sel[hm - 1:, :]
        ti = jnp.zeros((hm, LANES), F32)
        tw = jnp.zeros((hm, LANES), F32)
        rk = jnp.zeros((hm, LANES), F32)
        for kk in range(TOP_K):
            at = lane == kk
            ti = jnp.where(at, idxs[kk], ti)
            tw = jnp.where(at, es[kk] / den, tw)
            r_k = jnp.sum(jnp.where(lane_f == idxs[kk], before, 0.0), axis=-1, keepdims=True)
            rk = jnp.where(at, r_k, rk)
        ti_ref[rs, :] = ti.astype(jnp.int32)
        tw_ref[rs, :] = tw
        rk_ref[rs, :] = rk.astype(jnp.int32)

    _interleave([rows(g) for g in range(POSTATTN_SPLIT)])
    carry_ref[...] = carry["n"]
    cnt_ref[...] = carry["n"]


def _postattn(x2d, oa, ob, oc, kmem, vmem, prm, tabs, seq, tm, n_experts):
    T, D = x2d.shape
    M = kmem.shape[1]
    tpb = seq // tm
    hm = tm // POSTATTN_SPLIT
    row = lambda w: pl.BlockSpec((tm, w), lambda i: (i, 0))
    memspec = pl.BlockSpec((1, M, MEM_WIDTH), lambda i: (i // tpb, 0, 0))
    mix = FOX_WIDTH + MLA_WIDTH + DIL_WIDTH
    return pl.pallas_call(
        functools.partial(_postattn_kernel, n_experts=n_experts),
        out_shape=[jax.ShapeDtypeStruct((T, D), F32),
                   jax.ShapeDtypeStruct((T, LANES), jnp.int32), jax.ShapeDtypeStruct((T, LANES), F32),
                   jax.ShapeDtypeStruct((T, LANES), jnp.int32), jax.ShapeDtypeStruct((1, LANES), F32)],
        grid=(T // tm,),
        in_specs=[row(D), row(FOX_WIDTH), row(MLA_WIDTH), row(DIL_WIDTH), _full((1, mix)), _full((mix, D)),
                  _full((1, D)), _full((D, MEM_WIDTH)), _full((1, MEM_WIDTH)), _full((LANES, LANES)),
                  memspec, memspec, _full((MEM_WIDTH, D)), _full((1, D)),
                  _full((D, LANES)), _full((D, LANES)), _full((1, LANES)), _full((hm, hm))],
        out_specs=[row(D), row(LANES), row(LANES), row(LANES), _full((1, LANES))],
        scratch_shapes=[pltpu.VMEM((1, LANES), F32)],
        compiler_params=_cparams(("arbitrary",)),
        name="postattn",
    )(x2d, oa, ob, oc, prm["g_group"], prm["w_out"], prm["g_xmem"], prm["w_mq"], prm["g_qm"], tabs["bd64"],
      kmem, vmem, prm["w_mo"], prm["g_ffn"], prm["w_r_hi"], prm["w_r_lo"], prm["b_router"], tabs["tri_strict"])


def _row_copy(src_ref, src_row, dst_ref, dst_row, sem):
    return pltpu.make_async_copy(src_ref.at[pl.ds(src_row, 1), :], dst_ref.at[pl.ds(dst_row, 1), :], sem)


def _dispatch_kernel(fill_ref, pos_ref, x_ref, xs_ref, zero_ref, sem, zsem, *, tm, tr):
    @pl.when(pl.program_id(0) == 0)
    def _():
        n_experts = fill_ref.shape[0] - 1
        n_rows = xs_ref.shape[0]
        zero_ref[...] = jnp.zeros_like(zero_ref)
        starts = [(fill_ref[e], fill_ref[e] >= 0) for e in range(n_experts)]
        for j in range(n_experts + 1):
            start = fill_ref[n_experts] + j * tr
            starts.append((start, start < n_rows))

        def fill(start):
            return pltpu.make_async_copy(zero_ref, xs_ref.at[pl.ds(pl.multiple_of(start, tr), tr), :], zsem)

        for start, live in starts:
            @pl.when(live)
            def _(start=start):
                fill(start).start()
        for start, live in starts:
            @pl.when(live)
            def _(start=start):
                fill(start).wait()

    def issue(r, _):
        for kk in range(TOP_K):
            _row_copy(x_ref, r, xs_ref, pos_ref[0, 0, r * TOP_K + kk], sem).start(priority=kk % 2)
        return 0
    lax.fori_loop(0, tm, issue, 0, unroll=ROW_DMA_UNROLL)
    n = tm * TOP_K
    pltpu.make_async_copy(xs_ref.at[pl.ds(0, n), :], xs_ref.at[pl.ds(0, n), :], sem).wait()


def _dispatch(x2, pos, pad_start, n_rows, tm, tr):
    T, D = x2.shape
    return pl.pallas_call(
        functools.partial(_dispatch_kernel, tm=tm, tr=tr),
        out_shape=jax.ShapeDtypeStruct((n_rows, D), x2.dtype),
        grid_spec=pltpu.PrefetchScalarGridSpec(
            num_scalar_prefetch=1,
            grid=(T // tm,),
            in_specs=[pl.BlockSpec((1, 1, tm * TOP_K), lambda i, pad: (i, 0, 0), memory_space=pltpu.SMEM),
                      pl.BlockSpec((tm, D), lambda i, pad: (i, 0))],
            out_specs=pl.BlockSpec(memory_space=pl.ANY),
            scratch_shapes=[pltpu.VMEM((tr, D), x2.dtype), pltpu.SemaphoreType.DMA, pltpu.SemaphoreType.DMA],
        ),
        compiler_params=_cparams(("arbitrary",)),
        name="dispatch",
    )(pad_start, pos.reshape(T // tm, 1, tm * TOP_K), x2)


def _experts_kernel(te_ref, nu_ref, xs_ref, gf_ref, w1_ref, b1g_ref, b1l_ref, w2_ref, b2_ref, perm_ref,
                    ys_ref, w1s_ref, w2s_ref):
    i = pl.program_id(0)
    two_f = w1_ref.shape[3]
    n_blk = two_f // DEINT_BLOCK
    half = DEINT_BLOCK // 2

    @pl.when(i < nu_ref[0])
    def _():
        @pl.when(jnp.logical_or(i == 0, te_ref[i] != te_ref[jnp.maximum(i - 1, 0)]))
        def _():
            perm = perm_ref[...]
            for b in range(n_blk):
                sl = slice(b * DEINT_BLOCK, (b + 1) * DEINT_BLOCK)
                w1s_ref[:, sl] = _dot(w1_ref[0, 0, :, sl].astype(BF16), perm).astype(BF16)
            w2s_ref[...] = w2_ref[0, 0].astype(BF16)

        hm = xs_ref.shape[0] // EXPERT_SPLIT

        def rows(g):
            rs = slice(g * hm, (g + 1) * hm)
            x = _rms_rows(xs_ref[rs, :], gf_ref[...]).astype(BF16)
            yield
            h = _dot(x, w1s_ref[...])
            yield
            acts = []
            for b in range(n_blk):
                glu = h[:, b * DEINT_BLOCK:b * DEINT_BLOCK + half] + b1g_ref[0, :, b * half:(b + 1) * half]
                lin = h[:, b * DEINT_BLOCK + half:(b + 1) * DEINT_BLOCK] + b1l_ref[0, :, b * half:(b + 1) * half]
                glu = jnp.minimum(glu, SWIGLU_LIMIT)
                lin = jnp.clip(lin, -SWIGLU_LIMIT, SWIGLU_LIMIT)
                acts.append((glu * (1.0 / (1.0 + jnp.exp(-SWIGLU_ALPHA * glu))) * (lin + 1.0)).astype(BF16))
            act = jnp.concatenate(acts, axis=1) if n_blk > 1 else acts[0]
            yield
            ys_ref[rs, :] = _dot(act, w2s_ref[...]) + b2_ref[0]

        _interleave([rows(g) for g in range(EXPERT_SPLIT)])

    @pl.when(i >= nu_ref[0])
    def _():
        ys_ref[...] = jnp.zeros_like(ys_ref)


def _experts(tile_expert, n_used, xs, w_e1, w_e2, layer, prm, tabs, tr):
    n_rows, D = xs.shape
    two_f = w_e1.shape[3]
    F = two_f // 2
    assert two_f % DEINT_BLOCK == 0
    n_tiles = n_rows // tr
    rows = pl.BlockSpec((tr, D), lambda i, te, nu: (i, 0))
    wspec = lambda s: pl.BlockSpec((1,) + s, lambda i, te, nu: (te[i], 0, 0))
    lspec = lambda s: pl.BlockSpec((1, 1) + s, lambda i, te, nu: (layer, te[i], 0, 0))
    return pl.pallas_call(
        _experts_kernel,
        out_shape=jax.ShapeDtypeStruct((n_rows, D), F32),
        grid_spec=pltpu.PrefetchScalarGridSpec(
            num_scalar_prefetch=2,
            grid=(n_tiles,),
            in_specs=[rows, pl.BlockSpec((1, D), lambda i, te, nu: (0, 0)),
                      lspec((D, two_f)), wspec((1, F)), wspec((1, F)), lspec((F, D)), wspec((1, D)),
                      pl.BlockSpec((DEINT_BLOCK, DEINT_BLOCK), lambda i, te, nu: (0, 0))],
            out_specs=rows,
            scratch_shapes=[pltpu.VMEM((D, two_f), BF16), pltpu.VMEM((F, D), BF16)],
        ),
        compiler_params=_cparams(("arbitrary",)),
        name="experts",
    )(tile_expert, n_used, xs, prm["g_ffn"], w_e1, prm["b1g"], prm["b1l"], w_e2, prm["b2"], tabs["deint"])


def _combine_kernel(pos_ref, x_ref, tw_ref, ys_ref, o_ref, buf, sem, *, tm):
    def issue(r, _):
        for kk in range(TOP_K):
            _row_copy(ys_ref, pos_ref[0, 0, r * TOP_K + kk], buf.at[kk], r, sem).start(priority=kk % 2)
        return 0
    lax.fori_loop(0, tm, issue, 0, unroll=ROW_DMA_UNROLL)
    pltpu.make_async_copy(buf, buf, sem).wait()

    tw = tw_ref[...]
    acc = x_ref[...]
    for kk in range(TOP_K):
        acc = acc + tw[:, kk:kk + 1] * buf[kk]
    o_ref[...] = acc


def _combine(x2, tw, ys, pos, tm):
    T, D = x2.shape
    W = ys.shape[1]
    return pl.pallas_call(
        functools.partial(_combine_kernel, tm=tm),
        out_shape=jax.ShapeDtypeStruct((T, D), F32),
        grid=(T // tm,),
        in_specs=[pl.BlockSpec((1, 1, tm * TOP_K), lambda i: (i, 0, 0), memory_space=pltpu.SMEM),
                  pl.BlockSpec((tm, D), lambda i: (i, 0)),
                  pl.BlockSpec((tm, LANES), lambda i: (i, 0)),
                  pl.BlockSpec(memory_space=pl.ANY)],
        out_specs=pl.BlockSpec((tm, D), lambda i: (i, 0)),
        scratch_shapes=[pltpu.VMEM((TOP_K, tm, W), ys.dtype), pltpu.SemaphoreType.DMA],
        compiler_params=_cparams(("arbitrary",)),
        name="combine",
    )(pos.reshape(T // tm, 1, tm * TOP_K), x2, tw, ys)


def _tile_gain(g, reps, scale=1.0):
    return (jnp.tile(g.astype(F32), reps) * scale)[None, :]


def _prep_layer(l, D, w_in, b_forget, g_mix, g_qa, g_ka, g_cq, g_ckv, w_uq, w_ukv, g_qb, g_kb, g_qc, g_kc,
                g_group, w_out, g_xmem, g_mem, w_mq, w_mkv, g_qm, g_km, w_mo, g_ffn, w_router, b_router,
                w_e1, b_e1, w_e2, b_e2):
    wi = w_in[l]
    sizes = (FOX_WIDTH, FOX_WIDTH, FOX_WIDTH, FOX_HEADS, MLA_Q_RANK, MLA_KV_RANK, MLA_ROPE_DIM,
             DIL_WIDTH, DIL_WIDTH, DIL_WIDTH)
    pts = np.cumsum((0,) + sizes)
    qa, ka, va, fa, cq, ckv, kr, qc, kc, vc = [wi[:, pts[j]:pts[j + 1]] for j in range(len(sizes))]
    misc = jnp.zeros((D, LANES), F32).at[:, :FOX_HEADS].set(fa).at[:, 32:32 + MLA_ROPE_DIM].set(kr)
    w_main = jnp.concatenate([qa, ka, va, cq, ckv, qc, kc, vc, misc], axis=1).astype(BF16)
    b_misc = jnp.zeros((1, LANES), F32).at[0, :FOX_HEADS].set(b_forget[l])

    uq = w_uq[l].reshape(MLA_Q_RANK, MLA_HEADS, MLA_QK_DIM)
    uq = jnp.pad(uq, ((0, 0), (0, 0), (0, MLA_PAD - MLA_QK_DIM))).reshape(MLA_Q_RANK, MLA_HEADS * MLA_PAD)
    ukv = w_ukv[l].reshape(MLA_KV_RANK, MLA_HEADS, MLA_NOPE_DIM + MLA_V_DIM)
    uk = jnp.pad(ukv[:, :, :MLA_NOPE_DIM], ((0, 0), (0, 0), (0, MLA_PAD - MLA_NOPE_DIM)))
    uk = uk.reshape(MLA_KV_RANK, MLA_HEADS * MLA_PAD)
    uv = ukv[:, :, MLA_NOPE_DIM:].reshape(MLA_KV_RANK, MLA_WIDTH)
    pad_gain = lambda g: jnp.pad(g.astype(F32), (0, MLA_PAD - MLA_QK_DIM))

    E = w_e1.shape[1]
    wr = jnp.zeros((D, LANES), F32).at[:, :E].set(w_router[l])
    wr_hi = wr.astype(BF16)
    wr_lo = (wr - wr_hi.astype(F32)).astype(BF16)
    q_scale = lambda dim: dim ** -0.5 * LOG2E
    return dict(
        g_mix=g_mix[l][None], w_main=w_main, b_misc=b_misc,
        g_qa=_tile_gain(g_qa[l], FOX_HEADS, q_scale(HEAD_DIM)), g_ka=_tile_gain(g_ka[l], FOX_HEADS),
        g_cq=g_cq[l][None], g_ckv=g_ckv[l][None],
        w_uq=uq.astype(BF16), w_uk=uk.astype(BF16), w_uv=uv.astype(BF16),
        g_qb=_tile_gain(pad_gain(g_qb[l]), MLA_HEADS, q_scale(MLA_QK_DIM)),
        g_kb=_tile_gain(pad_gain(g_kb[l]), MLA_HEADS),
        g_qc=_tile_gain(g_qc[l], DIL_HEADS, q_scale(HEAD_DIM)), g_kc=_tile_gain(g_kc[l], DIL_HEADS),
        g_group=g_group[l][None], w_out=w_out[l].astype(BF16),
        g_xmem=g_xmem[l][None], g_mem=g_mem[l][None],
        w_mq=w_mq[l].astype(BF16), w_mkv=w_mkv[l].astype(BF16),
        g_qm=_tile_gain(g_qm[l], MEM_HEADS, q_scale(HEAD_DIM)), g_km=_tile_gain(g_km[l], MEM_HEADS),
        w_mo=w_mo[l].astype(BF16), g_ffn=g_ffn[l][None],
        w_r_hi=wr_hi, w_r_lo=wr_lo,
        b_router=jnp.zeros((1, LANES), F32).at[0, :E].set(b_router[l]),
        b1g=b_e1[l][:, None, 0::2], b1l=b_e1[l][:, None, 1::2], b2=b_e2[l][:, None, :],
    )


def _rope_lane_tables(positions, rot_dim, period, offset):
    half = rot_dim // 2
    assert offset % half == 0 and period % half == 0 and LANES % half == 0
    inv = ROPE_THETA ** (-jnp.arange(0, rot_dim, 2, dtype=F32) / rot_dim)
    ang = positions.astype(F32).reshape(-1)[:, None] * inv
    cos = jnp.tile(jnp.cos(ang), (1, LANES // half))
    sin = jnp.tile(jnp.sin(ang), (1, LANES // half))
    lane = np.arange(LANES) % period - offset
    first = (lane >= 0) & (lane < half)
    second = (lane >= half) & (lane < rot_dim)
    c = jnp.where(first | second, cos, 1.0)
    s_dn = jnp.where(second, sin, 0.0)
    s_up = jnp.where(first, -sin, 0.0)
    return c, s_dn, s_up


def _tables(positions, n_incl, n_strict):
    ri, rs = np.arange(n_incl), np.arange(n_strict)
    bd = (np.arange(LANES)[:, None] // HEAD_DIM) == (np.arange(LANES)[None, :] // HEAD_DIM)
    rep = np.zeros((LANES, MLA_HEADS * MLA_PAD), np.float32)
    for h in range(MLA_HEADS):
        for j in range(MLA_ROPE_DIM):
            rep[32 + j, h * MLA_PAD + MLA_NOPE_DIM + j] = 1.0
    half = DEINT_BLOCK // 2
    deint = np.zeros((DEINT_BLOCK, DEINT_BLOCK), np.float32)
    deint[2 * np.arange(half), np.arange(half)] = 1.0
    deint[2 * np.arange(half) + 1, half + np.arange(half)] = 1.0
    cm, sdm, sum_ = _rope_lane_tables(positions, MLA_ROPE_DIM, LANES, MLA_NOPE_DIM)
    cd, sdd, sud = _rope_lane_tables(positions, ROT_DIM, HEAD_DIM, 0)
    return dict(
        tri_incl=jnp.asarray(ri[:, None] >= ri[None, :], BF16),
        tri_strict=jnp.asarray(rs[:, None] > rs[None, :], BF16),
        bd64=jnp.asarray(bd, BF16), ones=jnp.ones((LANES, LANES), BF16), rep=jnp.asarray(rep, BF16),
        deint=jnp.asarray(deint, BF16),
        cm=cm, sdm=sdm, sum=sum_, cd=cd, sdd=sdd, sud=sud,
    )


def _routing_tables(ti, rk, cnt, n_experts, tr, n_tiles):
    counts = cnt[0, :n_experts].astype(jnp.int32)
    padded = ((counts + tr - 1) // tr) * tr
    ends = jnp.cumsum(padded)
    starts = ends - padded
    idx = ti[:, :TOP_K]
    hit = idx[:, :, None] == jnp.arange(n_experts, dtype=jnp.int32)
    pos = rk[:, :TOP_K] + jnp.sum(jnp.where(hit, starts, 0), axis=-1)
    tile_start = jnp.arange(n_tiles, dtype=jnp.int32) * tr
    tile_expert = jnp.sum((ends[None, :] <= tile_start[:, None]).astype(jnp.int32), axis=1)
    tile_expert = jnp.minimum(tile_expert, n_experts - 1)
    n_used = (ends[n_experts - 1:] // tr).astype(jnp.int32)
    pad_fill = jnp.where(padded > 0, ends - tr, -1)
    fill_start = jnp.concatenate([pad_fill, ends[n_experts - 1:]])
    return pos.astype(jnp.int32), tile_expert.astype(jnp.int32), fill_start.astype(jnp.int32), n_used


def _pick_tile(seq, want):
    t = min(want, seq)
    while seq % t:
        t //= 2
    return t


def kernel(x, mem, positions, g_mix, w_in, b_forget, g_qa, g_ka, g_cq, g_ckv, w_uq, w_ukv, g_qb, g_kb, g_qc, g_kc, g_group, w_out, g_xmem, g_mem, w_mq, w_mkv, g_qm, g_km, w_mo, g_ffn, w_router, b_router, w_e1, b_e1, w_e2, b_e2):
    B, S, D = x.shape
    T = B * S
    depth = w_in.shape[0]
    E = w_e1.shape[1]
    tm_in = _pick_tile(S, INPROJ_TILE)
    tm_post = _pick_tile(S, POSTATTN_TILE)
    tq = _pick_tile(S, ATTN_TILE)
    tr = EXPERT_ROW_TILE
    tmoe = _pick_tile(S, MOE_TOKEN_TILE)
    n_rows = T * TOP_K + (E + 1) * tr
    n_tiles = n_rows // tr

    tabs = _tables(positions, tm_in // INPROJ_SPLIT, tm_post // POSTATTN_SPLIT)
    x2d = x.reshape(T, D)
    for l in range(depth):
        prm = _prep_layer(l, D, w_in, b_forget, g_mix, g_qa, g_ka, g_cq, g_ckv, w_uq, w_ukv, g_qb, g_kb,
                          g_qc, g_kc, g_group, w_out, g_xmem, g_mem, w_mq, w_mkv, g_qm, g_km, w_mo, g_ffn,
                          w_router, b_router, w_e1, b_e1, w_e2, b_e2)
        qa, ka, va, csum, qb, kb, vb, qc, kc, vc = _inproj(x2d, prm, tabs, S, tm_in)
        kbias = -LOG2E * csum[:, :FOX_HEADS].reshape(B, S, FOX_HEADS // 2, 2).transpose(0, 2, 3, 1)
        r3 = lambda a: a.reshape(B, S, a.shape[-1])
        oa = _causal_attention(r3(qa), r3(ka), r3(va), kbias, tq=tq, split_heads=False)
        ob = _causal_attention(r3(qb), r3(kb), r3(vb), None, tq=tq, split_heads=True)
        oc = _dilated_attention(r3(qc), r3(kc), r3(vc), S)
        kmem, vmem = _memkv(mem, prm, tabs)
        x2, ti, tw, rk, cnt = _postattn(x2d, oa.reshape(T, -1), ob.reshape(T, -1), oc.reshape(T, -1),
                                        kmem, vmem, prm, tabs, S, tm_post, E)
        pos, tile_expert, pad_start, n_used = _routing_tables(ti, rk, cnt, E, tr, n_tiles)
        xs = _dispatch(x2, pos, pad_start, n_rows, tmoe, tr)
        ys = _experts(tile_expert, n_used, xs, w_e1, w_e2, l, prm, tabs, tr)
        x2d = _combine(x2, tw, ys, pos, tmoe)
    return x2d.reshape(B, S, D)
```

```python
import functools
import math

import jax
import jax.numpy as jnp
import numpy as np
from jax import lax
from jax.experimental import pallas as pl
from jax.experimental.pallas import tpu as pltpu

HEAD_DIM = 64
FOX_HEADS = 4
MLA_HEADS = 6
DIL_HEADS = 6
MLA_Q_RANK = 384
MLA_KV_RANK = 256
MLA_NOPE_DIM = 64
MLA_ROPE_DIM = 32
MLA_QK_DIM = MLA_NOPE_DIM + MLA_ROPE_DIM
MLA_V_DIM = 64
ROT_DIM = HEAD_DIM // 4
ROPE_THETA = 500000.0
DILATED_PATTERNS = ((128, 1), (512, 4), (2048, 16))
FOX_WIDTH = FOX_HEADS * HEAD_DIM
MLA_WIDTH = MLA_HEADS * MLA_V_DIM
DIL_WIDTH = DIL_HEADS * HEAD_DIM
MEM_HEADS = 4
MEM_WIDTH = MEM_HEADS * HEAD_DIM
TOP_K = 4
SWIGLU_ALPHA = 1.702
SWIGLU_LIMIT = 7.0
RMS_EPS = 1e-6
LOG2E = math.log2(math.e)

LANES = 128
VMEM_LIMIT_BYTES = 56 * 1024 * 1024

INPROJ_TILE = 512
INPROJ_SPLIT = 1
POSTATTN_TILE = 1024
POSTATTN_SPLIT = 2
ATTN_TILE = 512
EXPERT_ROW_TILE = 512
EXPERT_SPLIT = 2
DISPATCH_TILE = 1024
COMBINE_TILE = 512
ROW_DMA_UNROLL = 8

NEG_BIG = -1e30
MLA_PAD = 128
DEINT_BLOCK = 256

F32 = jnp.float32
BF16 = jnp.bfloat16


def _cparams(sem):
    return pltpu.CompilerParams(dimension_semantics=sem, vmem_limit_bytes=VMEM_LIMIT_BYTES)


def _full(shape):
    nd = len(shape)
    return pl.BlockSpec(shape, lambda *_: (0,) * nd)


def _dot(a, b):
    return jnp.dot(a, b, preferred_element_type=F32)


def _dot_nt(a, b):
    return lax.dot_general(a, b, (((1,), (1,)), ((), ())), preferred_element_type=F32)


def _rms_rows(x, g):
    return x * lax.rsqrt(jnp.mean(x * x, axis=-1, keepdims=True) + RMS_EPS) * g


def _group_norm_tiles(y, ones_mat, inv_n):
    outs = []
    for j in range(y.shape[1] // LANES):
        yj = y[:, j * LANES:(j + 1) * LANES]
        ss = _dot((yj * yj).astype(BF16), ones_mat)
        outs.append(yj * lax.rsqrt(ss * inv_n + RMS_EPS))
    return outs


def _rope_tile(y, half, c, s_dn, s_up):
    return y * c + pltpu.roll(y, half, 1) * s_dn + pltpu.roll(y, LANES - half, 1) * s_up


def _interleave(gens):
    live = list(gens)
    while live:
        for g in list(live):
            try:
                next(g)
            except StopIteration:
                live.remove(g)


def _stagger(gens):
    pending = list(gens)
    live = []
    while pending or live:
        if pending:
            live.append(pending.pop(0))
        for g in list(live):
            try:
                next(g)
            except StopIteration:
                live.remove(g)


def _inproj_kernel(x_ref, gmix_ref, w_ref, bmisc_ref, tri_ref, bd64_ref, ones_ref,
                   gqa_ref, gka_ref, gcq_ref, gckv_ref, wuq_ref, wuk_ref, wuv_ref, rep_ref,
                   gqb_ref, gkb_ref, gqc_ref, gkc_ref,
                   cm_ref, sdm_ref, sum_ref, cd_ref, sdd_ref, sud_ref,
                   qa_ref, ka_ref, va_ref, c_ref, qb_ref, kb_ref, vb_ref, qc_ref, kc_ref, vc_ref,
                   carry_ref, *, tiles_per_batch):
    i = pl.program_id(0)
    tm = x_ref.shape[0]
    hm = tm // INPROJ_SPLIT
    bd64 = bd64_ref[...]
    ones = ones_ref[...]
    tri = tri_ref[...]
    half_b = MLA_ROPE_DIM // 2
    half_c = ROT_DIM // 2
    offs = np.cumsum((0, FOX_WIDTH, FOX_WIDTH, FOX_WIDTH, MLA_Q_RANK, MLA_KV_RANK,
                      DIL_WIDTH, DIL_WIDTH, DIL_WIDTH, LANES))

    @pl.when(i % tiles_per_batch == 0)
    def _():
        carry_ref[...] = jnp.zeros_like(carry_ref)

    carry = {"c": carry_ref[...]}

    def rows(g):
        rs = slice(g * hm, (g + 1) * hm)
        h = _rms_rows(x_ref[rs, :], gmix_ref[...]).astype(BF16)
        seg = lambda j: _dot(h, w_ref[:, offs[j]:offs[j + 1]])
        yield
        qa, ka, va, cq = seg(0), seg(1), seg(2), seg(3)
        yield
        qa_ref[rs, :] = (jnp.concatenate(_group_norm_tiles(qa, bd64, 1.0 / HEAD_DIM), axis=1)
                         * gqa_ref[...]).astype(BF16)
        ka_ref[rs, :] = (jnp.concatenate(_group_norm_tiles(ka, bd64, 1.0 / HEAD_DIM), axis=1)
                         * gka_ref[...]).astype(BF16)
        va_ref[rs, :] = va.astype(BF16)
        cqn = _rms_rows(cq, gcq_ref[...]).astype(BF16)
        yield
        ckv, qc = seg(4), seg(5)
        qb_pre = _dot(cqn, wuq_ref[...])
        yield
        ckvn = _rms_rows(ckv, gckv_ref[...]).astype(BF16)
        cd, sdd, sud = cd_ref[rs, :], sdd_ref[rs, :], sud_ref[rs, :]
        qc_t = _group_norm_tiles(qc, bd64, 1.0 / HEAD_DIM)
        for j in range(DIL_WIDTH // LANES):
            sl = slice(j * LANES, (j + 1) * LANES)
            qc_ref[rs, sl] = _rope_tile(qc_t[j] * gqc_ref[:, sl], half_c, cd, sdd, sud)
        yield
        kc, vc, misc = seg(6), seg(7), seg(8)
        kb_nope = _dot(ckvn, wuk_ref[...])
        vb_ref[rs, :] = _dot(ckvn, wuv_ref[...]).astype(BF16)
        yield
        cm, sdm, sum_ = cm_ref[rs, :], sdm_ref[rs, :], sum_ref[rs, :]
        qb_t = _group_norm_tiles(qb_pre, ones, 1.0 / MLA_QK_DIM)
        for j in range(MLA_HEADS):
            sl = slice(j * LANES, (j + 1) * LANES)
            qb_ref[rs, sl] = _rope_tile(qb_t[j] * gqb_ref[:, sl], half_b, cm, sdm, sum_).astype(BF16)
        kc_t = _group_norm_tiles(kc, bd64, 1.0 / HEAD_DIM)
        for j in range(DIL_WIDTH // LANES):
            sl = slice(j * LANES, (j + 1) * LANES)
            kc_ref[rs, sl] = _rope_tile(kc_t[j] * gkc_ref[:, sl], half_c, cd, sdd, sud)
        vc_ref[rs, :] = vc
        z = misc + bmisc_ref[...]
        logf = jnp.minimum(z, 0.0) - jnp.log(1.0 + jnp.exp(-jnp.abs(z)))
        lane = lax.broadcasted_iota(jnp.int32, logf.shape, 1)
        logf = jnp.where(lane < FOX_HEADS, logf, 0.0)
        p0 = logf.astype(BF16)
        r0 = logf - p0.astype(F32)
        p1 = r0.astype(BF16)
        p2 = (r0 - p1.astype(F32)).astype(BF16)
        yield
        csum = _dot(tri, p0) + _dot(tri, p1) + _dot(tri, p2)
        kb_pre = kb_nope + _dot(misc.astype(BF16), rep_ref[...])
        yield
        kb_t = _group_norm_tiles(kb_pre, ones, 1.0 / MLA_QK_DIM)
        for j in range(MLA_HEADS):
            sl = slice(j * LANES, (j + 1) * LANES)
            kb_ref[rs, sl] = _rope_tile(kb_t[j] * gkb_ref[:, sl], half_b, cm, sdm, sum_).astype(BF16)
        csum = csum + carry["c"]
        c_ref[rs, :] = csum
        carry["c"] = csum[hm - 1:, :]

    _interleave([rows(g) for g in range(INPROJ_SPLIT)])
    carry_ref[...] = carry["c"]


def _inproj(x2d, prm, tabs, seq, tm):
    T, D = x2d.shape
    nm = prm["w_main"].shape[1]
    hm = tm // INPROJ_SPLIT
    row = lambda w: pl.BlockSpec((tm, w), lambda i: (i, 0))
    in_specs = [row(D), _full((1, D)), _full((D, nm)), _full((1, LANES)), _full((hm, hm)),
                _full((LANES, LANES)), _full((LANES, LANES)),
                _full((1, FOX_WIDTH)), _full((1, FOX_WIDTH)), _full((1, MLA_Q_RANK)), _full((1, MLA_KV_RANK)),
                _full((MLA_Q_RANK, MLA_HEADS * MLA_PAD)), _full((MLA_KV_RANK, MLA_HEADS * MLA_PAD)),
                _full((MLA_KV_RANK, MLA_WIDTH)), _full((LANES, MLA_HEADS * MLA_PAD)),
                _full((1, MLA_HEADS * MLA_PAD)), _full((1, MLA_HEADS * MLA_PAD)),
                _full((1, DIL_WIDTH)), _full((1, DIL_WIDTH))] + [row(LANES)] * 6
    outs = [(FOX_WIDTH, BF16), (FOX_WIDTH, BF16), (FOX_WIDTH, BF16), (LANES, F32),
            (MLA_HEADS * MLA_PAD, BF16), (MLA_HEADS * MLA_PAD, BF16), (MLA_WIDTH, BF16),
            (DIL_WIDTH, F32), (DIL_WIDTH, F32), (DIL_WIDTH, F32)]
    return pl.pallas_call(
        functools.partial(_inproj_kernel, tiles_per_batch=seq // tm),
        out_shape=[jax.ShapeDtypeStruct((T, w), dt) for w, dt in outs],
        grid=(T // tm,),
        in_specs=in_specs,
        out_specs=[row(w) for w, _ in outs],
        scratch_shapes=[pltpu.VMEM((1, LANES), F32)],
        compiler_params=_cparams(("arbitrary",)),
        name="inproj",
    )(x2d, prm["g_mix"], prm["w_main"], prm["b_misc"], tabs["tri_incl"], tabs["bd64"], tabs["ones"],
      prm["g_qa"], prm["g_ka"], prm["g_cq"], prm["g_ckv"], prm["w_uq"], prm["w_uk"], prm["w_uv"], tabs["rep"],
      prm["g_qb"], prm["g_kb"], prm["g_qc"], prm["g_kc"],
      tabs["cm"], tabs["sdm"], tabs["sum"], tabs["cd"], tabs["sdd"], tabs["sud"])


def _causal_kernel(*refs, seq, tq, split_heads, with_bias):
    if with_bias:
        q_ref, k_ref, v_ref, b_ref, o_ref = refs
    else:
        q_ref, k_ref, v_ref, o_ref = refs
        b_ref = None
    lane = lax.broadcasted_iota(jnp.int32, (tq, LANES), 1)
    lo_half = lane < HEAD_DIM
    r = lax.broadcasted_iota(jnp.int32, (tq, tq), 0)
    c = lax.broadcasted_iota(jnp.int32, (tq, tq), 1)
    keep = c <= r
    streams = []
    for qi in range(seq // tq):
        q = q_ref[0, qi * tq:(qi + 1) * tq, :]
        n_keys = (qi + 1) * tq
        v = v_ref[0, :n_keys, :]
        res = [None, None]

        def head(hh, qi=qi, q=q, n_keys=n_keys, v=v, res=res):
            if split_heads:
                qh = q[:, hh * LANES:(hh + 1) * LANES]
                k = k_ref[0, :n_keys, hh * LANES:(hh + 1) * LANES]
            else:
                zero = jnp.zeros_like(q)
                qh = jnp.where(lo_half, q, zero) if hh == 0 else jnp.where(lo_half, zero, q)
                k = k_ref[0, :n_keys, :]
            s = _dot_nt(qh, k)
            yield
            if with_bias:
                s = s + b_ref[0, 0, hh:hh + 1, :n_keys]
            s_diag = jnp.where(keep, s[:, n_keys - tq:], NEG_BIG)
            s = jnp.concatenate([s[:, :n_keys - tq], s_diag], axis=1) if qi else s_diag
            p = jnp.exp2(s - jnp.max(s, axis=-1, keepdims=True))
            l = jnp.sum(p, axis=-1, keepdims=True)
            yield
            res[hh] = _dot(p.astype(BF16), v) / l
            if hh == 1:
                o_ref[0, qi * tq:(qi + 1) * tq, :] = jnp.where(lo_half, res[0], res[1]).astype(o_ref.dtype)

        streams += [head(0), head(1)]
    if split_heads:
        _stagger(streams)
    else:
        for j in range(0, len(streams), 2):
            _interleave(streams[j:j + 2])


def _causal_attention(q, k, v, kbias, *, tq, split_heads):
    B, S, _ = q.shape
    P = v.shape[2] // LANES
    wq = 2 * LANES if split_heads else LANES
    in_specs = [pl.BlockSpec((1, S, wq), lambda b, p: (b, 0, p)),
                pl.BlockSpec((1, S, wq), lambda b, p: (b, 0, p)),
                pl.BlockSpec((1, S, LANES), lambda b, p: (b, 0, p))]
    args = [q, k, v]
    if kbias is not None:
        in_specs.append(pl.BlockSpec((1, 1, 2, S), lambda b, p: (b, p, 0, 0)))
        args.append(kbias)
    return pl.pallas_call(
        functools.partial(_causal_kernel, seq=S, tq=tq, split_heads=split_heads, with_bias=kbias is not None),
        out_shape=jax.ShapeDtypeStruct((B, S, P * LANES), BF16),
        grid=(B, P),
        in_specs=in_specs,
        out_specs=pl.BlockSpec((1, S, LANES), lambda b, p: (b, 0, p)),
        compiler_params=_cparams(("parallel", "parallel")),
        name="causal_split" if split_heads else "causal_shared",
    )(*args)


def _dilated_kernel(q_ref, k_ref, v_ref, o_ref, acc_ref, m_ref, l_ref, *, seq, n, dilations):
    lane = lax.broadcasted_iota(jnp.int32, (1, 1, LANES), 2)
    lo_half = lane < HEAD_DIM
    row = lax.broadcasted_iota(jnp.int32, (n, n), 0)
    col = lax.broadcasted_iota(jnp.int32, (n, n), 1)
    own_keep = (col <= row)[None]
    prev_keep = (col >= row)[None]

    nblk = seq // n
    for pi, d in enumerate(dilations):
        L = seq // d
        nb = L // n

        def load(ref, d=d, L=L, nb=nb):
            if d == 1:
                return ref[0].reshape(nblk, n, LANES).astype(BF16)
            parts = [ref[0, pl.ds(r, L, stride=d), :].reshape(nb, n, LANES) for r in range(d)]
            return jnp.concatenate(parts, axis=0).astype(BF16)

        qb, kb, vb = load(q_ref), load(k_ref), load(v_ref)
        if nb > 1:
            zero = jnp.zeros((1, n, LANES), BF16)
            kp = jnp.concatenate([zero, kb[:nblk - 1]], axis=0)
            vp = jnp.concatenate([zero, vb[:nblk - 1]], axis=0)
            blk = lax.broadcasted_iota(jnp.int32, (nblk, n, n), 0)
            prev_ok = jnp.logical_and(prev_keep, lax.rem(blk, nb) != 0)
        zq = jnp.zeros_like(qb)
        heads = []
        for hh in range(2):
            qh = jnp.where(lo_half, qb, zq) if hh == 0 else jnp.where(lo_half, zq, qb)
            s_own = jnp.einsum("bqd,bkd->bqk", qh, kb, preferred_element_type=F32)
            s_own = jnp.where(own_keep, s_own, NEG_BIG)
            m = jnp.max(s_own, axis=-1, keepdims=True)
            if nb > 1:
                s_prev = jnp.einsum("bqd,bkd->bqk", qh, kp, preferred_element_type=F32)
                s_prev = jnp.where(prev_ok, s_prev, NEG_BIG)
                m = jnp.maximum(m, jnp.max(s_prev, axis=-1, keepdims=True))
            p_own = jnp.exp2(s_own - m)
            l = jnp.sum(p_own, axis=-1, keepdims=True)
            pv = jnp.einsum("bqk,bkd->bqd", p_own.astype(BF16), vb, preferred_element_type=F32)
            if nb > 1:
                p_prev = jnp.exp2(s_prev - m)
                l = l + jnp.sum(p_prev, axis=-1, keepdims=True)
                pv = pv + jnp.einsum("bqk,bkd->bqd", p_prev.astype(BF16), vp, preferred_element_type=F32)
            heads.append((m, l, pv))
        m_f = jnp.where(lo_half, heads[0][0], heads[1][0])
        l_f = jnp.where(lo_half, heads[0][1], heads[1][1])
        a_f = jnp.where(lo_half, heads[0][2], heads[1][2])
        for dst, val in ((acc_ref, a_f), (m_ref, m_f), (l_ref, l_f)):
            if d == 1:
                dst[pi] = val.reshape(seq, LANES)
            else:
                for r in range(d):
                    dst[pi, pl.ds(r, L, stride=d), :] = val[r * nb:(r + 1) * nb].reshape(L, LANES)

    np_ = len(dilations)
    m_all = m_ref[0]
    for pi in range(1, np_):
        m_all = jnp.maximum(m_all, m_ref[pi])
    num = jnp.zeros((seq, LANES), F32)
    den = jnp.zeros((seq, LANES), F32)
    for pi in range(np_):
        w = jnp.exp2(m_ref[pi] - m_all)
        num = num + w * acc_ref[pi]
        den = den + w * l_ref[pi]
    o_ref[0] = num / den


def _dilated_attention(qc, kc, vc, seq):
    B = qc.shape[0]
    n = DILATED_PATTERNS[0][0] // DILATED_PATTERNS[0][1]
    dil = tuple(d for _, d in DILATED_PATTERNS)
    for w, d in DILATED_PATTERNS:
        assert w // d == n and seq % (d * n) == 0
    spec = pl.BlockSpec((1, seq, LANES), lambda b, p: (b, 0, p))
    np_ = len(dil)
    return pl.pallas_call(
        functools.partial(_dilated_kernel, seq=seq, n=n, dilations=dil),
        out_shape=jax.ShapeDtypeStruct((B, seq, DIL_WIDTH), F32),
        grid=(B, DIL_WIDTH // LANES),
        in_specs=[spec, spec, spec],
        out_specs=spec,
        scratch_shapes=[pltpu.VMEM((np_, seq, LANES), F32)] * 3,
        compiler_params=_cparams(("parallel", "parallel")),
        name="dilated",
    )(qc, kc, vc)


def _memkv_kernel(mem_ref, g_ref, w_ref, gk_ref, bd64_ref, k_ref, v_ref):
    mn = _rms_rows(mem_ref[0], g_ref[...]).astype(BF16)
    kv = _dot(mn, w_ref[...])
    k = jnp.concatenate(_group_norm_tiles(kv[:, :MEM_WIDTH], bd64_ref[...], 1.0 / HEAD_DIM), axis=1)
    k_ref[0] = (k * gk_ref[...]).astype(BF16)
    v_ref[0] = kv[:, MEM_WIDTH:].astype(BF16)


def _memkv(mem, prm, tabs):
    B, M, D = mem.shape
    return pl.pallas_call(
        _memkv_kernel,
        out_shape=[jax.ShapeDtypeStruct((B, M, MEM_WIDTH), BF16)] * 2,
        grid=(B,),
        in_specs=[pl.BlockSpec((1, M, D), lambda b: (b, 0, 0)), _full((1, D)), _full((D, 2 * MEM_WIDTH)),
                  _full((1, MEM_WIDTH)), _full((LANES, LANES))],
        out_specs=[pl.BlockSpec((1, M, MEM_WIDTH), lambda b: (b, 0, 0))] * 2,
        compiler_params=_cparams(("parallel",)),
        name="memkv",
    )(mem, prm["g_mem"], prm["w_mkv"], prm["g_km"], tabs["bd64"])


def _postattn_kernel(x_ref, oa_ref, ob_ref, oc_ref, gg_ref, wout_ref, gx_ref, wmq_ref, gqm_ref, bd64_ref,
                     km_ref, vm_ref, wmo_ref, gf_ref, wrh_ref, wrl_ref, br_ref, tri_ref,
                     x2_ref, ti_ref, tw_ref, rk_ref, cnt_ref, carry_ref, *, n_experts):
    i = pl.program_id(0)
    tm = x_ref.shape[0]
    hm = tm // POSTATTN_SPLIT
    a0, a1 = FOX_WIDTH, FOX_WIDTH + MLA_WIDTH
    lane = lax.broadcasted_iota(jnp.int32, (hm, LANES), 1)
    lane_f = lane.astype(F32)
    lo_half = lane < HEAD_DIM
    tri = tri_ref[...]

    @pl.when(i == 0)
    def _():
        carry_ref[...] = jnp.zeros_like(carry_ref)

    carry = {"n": carry_ref[...]}

    def rows(g):
        rs = slice(g * hm, (g + 1) * hm)
        ma = _rms_rows(oa_ref[rs, :].astype(F32), gg_ref[:, :a0]).astype(BF16)
        mb = _rms_rows(ob_ref[rs, :].astype(F32), gg_ref[:, a0:a1]).astype(BF16)
        mc = _rms_rows(oc_ref[rs, :], gg_ref[:, a1:]).astype(BF16)
        yield
        x1 = (x_ref[rs, :] + _dot(ma, wout_ref[:a0, :]) + _dot(mb, wout_ref[a0:a1, :])
              + _dot(mc, wout_ref[a1:, :]))
        yield
        h2 = _rms_rows(x1, gx_ref[...]).astype(BF16)
        yield
        qm = _dot(h2, wmq_ref[...])
        yield
        qm_t = _group_norm_tiles(qm, bd64_ref[...], 1.0 / HEAD_DIM)
        outs = []
        for p in range(MEM_WIDTH // LANES):
            sl = slice(p * LANES, (p + 1) * LANES)
            q2 = (qm_t[p] * gqm_ref[:, sl]).astype(BF16)
            k2 = km_ref[0, :, sl]
            v2 = vm_ref[0, :, sl]
            zero = jnp.zeros_like(q2)
            res = []
            for hh in range(2):
                qh = jnp.where(lo_half, q2, zero) if hh == 0 else jnp.where(lo_half, zero, q2)
                s = _dot_nt(qh, k2)
                pr = jnp.exp2(s - jnp.max(s, axis=-1, keepdims=True))
                l = jnp.sum(pr, axis=-1, keepdims=True)
                res.append(_dot(pr.astype(BF16), v2) / l)
            outs.append(jnp.where(lo_half, res[0], res[1]))
            yield
        om = jnp.concatenate(outs, axis=1).astype(BF16)
        x2 = x1 + _dot(om, wmo_ref[...])
        x2_ref[rs, :] = x2
        yield
        h3 = _rms_rows(x2, gf_ref[...])
        h3_hi = h3.astype(BF16)
        h3_lo = (h3 - h3_hi.astype(F32)).astype(BF16)
        yield
        logits = _dot(h3_hi, wrh_ref[...]) + _dot(h3_hi, wrl_ref[...]) + _dot(h3_lo, wrh_ref[...]) + br_ref[...]
        yield
        logits = jnp.where(lane < n_experts, logits, NEG_BIG)
        work = logits
        sel = jnp.zeros((hm, LANES), F32)
        vals, idxs = [], []
        for _ in range(TOP_K):
            mv = jnp.max(work, axis=-1, keepdims=True)
            ix = jnp.min(jnp.where(work == mv, lane_f, float(LANES)), axis=-1, keepdims=True)
            hit = lane_f == ix
            work = jnp.where(hit, NEG_BIG, work)
            sel = jnp.where(hit, 1.0, sel)
            vals.append(mv)
            idxs.append(ix)
        es = [jnp.exp(v - vals[0]) for v in vals]
        den = es[0] + es[1] + es[2] + es[3]
        yield
        before = _dot(tri, sel.astype(BF16))
        yield
        before = before + carry["n"]
        carry["n"] = before[hm - 1:, :] + sel[hm - 1:, :]
        ti = jnp.zeros((hm, LANES), F32)
        tw = jnp.zeros((hm, LANES), F32)
        rk = jnp.zeros((hm, LANES), F32)
        for kk in range(TOP_K):
            at = lane == kk
            ti = jnp.where(at, idxs[kk], ti)
            tw = jnp.where(at, es[kk] / den, tw)
            r_k = jnp.sum(jnp.where(lane_f == idxs[kk], before, 0.0), axis=-1, keepdims=True)
            rk = jnp.where(at, r_k, rk)
        ti_ref[rs, :] = ti.astype(jnp.int32)
        tw_ref[rs, :] = tw
        rk_ref[rs, :] = rk.astype(jnp.int32)

    _interleave([rows(g) for g in range(POSTATTN_SPLIT)])
    carry_ref[...] = carry["n"]
    cnt_ref[...] = carry["n"]


def _postattn(x2d, oa, ob, oc, kmem, vmem, prm, tabs, seq, tm, n_experts):
    T, D = x2d.shape
    M = kmem.shape[1]
    tpb = seq // tm
    hm = tm // POSTATTN_SPLIT
    row = lambda w: pl.BlockSpec((tm, w), lambda i: (i, 0))
    memspec = pl.BlockSpec((1, M, MEM_WIDTH), lambda i: (i // tpb, 0, 0))
    mix = FOX_WIDTH + MLA_WIDTH + DIL_WIDTH
    return pl.pallas_call(
        functools.partial(_postattn_kernel, n_experts=n_experts),
        out_shape=[jax.ShapeDtypeStruct((T, D), F32),
                   jax.ShapeDtypeStruct((T, LANES), jnp.int32), jax.ShapeDtypeStruct((T, LANES), F32),
                   jax.ShapeDtypeStruct((T, LANES), jnp.int32), jax.ShapeDtypeStruct((1, LANES), F32)],
        grid=(T // tm,),
        in_specs=[row(D), row(FOX_WIDTH), row(MLA_WIDTH), row(DIL_WIDTH), _full((1, mix)), _full((mix, D)),
                  _full((1, D)), _full((D, MEM_WIDTH)), _full((1, MEM_WIDTH)), _full((LANES, LANES)),
                  memspec, memspec, _full((MEM_WIDTH, D)), _full((1, D)),
                  _full((D, LANES)), _full((D, LANES)), _full((1, LANES)), _full((hm, hm))],
        out_specs=[row(D), row(LANES), row(LANES), row(LANES), _full((1, LANES))],
        scratch_shapes=[pltpu.VMEM((1, LANES), F32)],
        compiler_params=_cparams(("arbitrary",)),
        name="postattn",
    )(x2d, oa, ob, oc, prm["g_group"], prm["w_out"], prm["g_xmem"], prm["w_mq"], prm["g_qm"], tabs["bd64"],
      kmem, vmem, prm["w_mo"], prm["g_ffn"], prm["w_r_hi"], prm["w_r_lo"], prm["b_router"], tabs["tri_strict"])


def _row_copy(src_ref, src_row, dst_ref, dst_row, sem):
    return pltpu.make_async_copy(src_ref.at[pl.ds(src_row, 1), :], dst_ref.at[pl.ds(dst_row, 1), :], sem)


def _dispatch_kernel(fill_ref, pos_ref, x_ref, xs_ref, zero_ref, sem, zsem, *, tm, tr):
    @pl.when(pl.program_id(0) == 0)
    def _():
        n_experts = fill_ref.shape[0] - 1
        n_rows = xs_ref.shape[0]
        zero_ref[...] = jnp.zeros_like(zero_ref)
        starts = [(fill_ref[e], fill_ref[e] >= 0) for e in range(n_experts)]
        for j in range(n_experts + 1):
            start = fill_ref[n_experts] + j * tr
            starts.append((start, start < n_rows))

        def fill(start):
            return pltpu.make_async_copy(zero_ref, xs_ref.at[pl.ds(pl.multiple_of(start, tr), tr), :], zsem)

        for start, live in starts:
            @pl.when(live)
            def _(start=start):
                fill(start).start()
        for start, live in starts:
            @pl.when(live)
            def _(start=start):
                fill(start).wait()

    def issue(r, _):
        for kk in range(TOP_K):
            _row_copy(x_ref, r, xs_ref, pos_ref[0, 0, r * TOP_K + kk], sem).start(priority=kk % 2)
        return 0
    lax.fori_loop(0, tm, issue, 0, unroll=ROW_DMA_UNROLL)
    n = tm * TOP_K
    pltpu.make_async_copy(xs_ref.at[pl.ds(0, n), :], xs_ref.at[pl.ds(0, n), :], sem).wait()


def _dispatch(x2, pos, pad_start, n_rows, tm, tr):
    T, D = x2.shape
    return pl.pallas_call(
        functools.partial(_dispatch_kernel, tm=tm, tr=tr),
        out_shape=jax.ShapeDtypeStruct((n_rows, D), x2.dtype),
        grid_spec=pltpu.PrefetchScalarGridSpec(
            num_scalar_prefetch=1,
            grid=(T // tm,),
            in_specs=[pl.BlockSpec((1, 1, tm * TOP_K), lambda i, pad: (i, 0, 0), memory_space=pltpu.SMEM),
                      pl.BlockSpec((tm, D), lambda i, pad: (i, 0))],
            out_specs=pl.BlockSpec(memory_space=pl.ANY),
            scratch_shapes=[pltpu.VMEM((tr, D), x2.dtype), pltpu.SemaphoreType.DMA, pltpu.SemaphoreType.DMA],
        ),
        compiler_params=_cparams(("arbitrary",)),
        name="dispatch",
    )(pad_start, pos.reshape(T // tm, 1, tm * TOP_K), x2)


def _experts_kernel(te_ref, nu_ref, xs_ref, gf_ref, w1_ref, b1g_ref, b1l_ref, w2_ref, b2_ref, perm_ref,
                    ys_ref, w1s_ref, w2s_ref):
    i = pl.program_id(0)
    two_f = w1_ref.shape[3]
    n_blk = two_f // DEINT_BLOCK
    half = DEINT_BLOCK // 2

    @pl.when(i < nu_ref[0])
    def _():
        @pl.when(jnp.logical_or(i == 0, te_ref[i] != te_ref[jnp.maximum(i - 1, 0)]))
        def _():
            perm = perm_ref[...]
            for b in range(n_blk):
                sl = slice(b * DEINT_BLOCK, (b + 1) * DEINT_BLOCK)
                w1s_ref[:, sl] = _dot(w1_ref[0, 0, :, sl].astype(BF16), perm).astype(BF16)
            w2s_ref[...] = w2_ref[0, 0].astype(BF16)

        hm = xs_ref.shape[0] // EXPERT_SPLIT

        def rows(g):
            rs = slice(g * hm, (g + 1) * hm)
            x = _rms_rows(xs_ref[rs, :], gf_ref[...]).astype(BF16)
            yield
            h = _dot(x, w1s_ref[...])
            yield
            acts = []
            for b in range(n_blk):
                glu = h[:, b * DEINT_BLOCK:b * DEINT_BLOCK + half] + b1g_ref[0, :, b * half:(b + 1) * half]
                lin = h[:, b * DEINT_BLOCK + half:(b + 1) * DEINT_BLOCK] + b1l_ref[0, :, b * half:(b + 1) * half]
                glu = jnp.minimum(glu, SWIGLU_LIMIT)
                lin = jnp.clip(lin, -SWIGLU_LIMIT, SWIGLU_LIMIT)
                acts.append((glu * (1.0 / (1.0 + jnp.exp(-SWIGLU_ALPHA * glu))) * (lin + 1.0)).astype(BF16))
            act = jnp.concatenate(acts, axis=1) if n_blk > 1 else acts[0]
            yield
            ys_ref[rs, :] = _dot(act, w2s_ref[...]) + b2_ref[0]

        _interleave([rows(g) for g in range(EXPERT_SPLIT)])

    @pl.when(i >= nu_ref[0])
    def _():
        ys_ref[...] = jnp.zeros_like(ys_ref)


def _experts(tile_expert, n_used, xs, w_e1, w_e2, layer, prm, tabs, tr):
    n_rows, D = xs.shape
    two_f = w_e1.shape[3]
    F = two_f // 2
    assert two_f % DEINT_BLOCK == 0
    n_tiles = n_rows // tr
    rows = pl.BlockSpec((tr, D), lambda i, te, nu: (i, 0))
    wspec = lambda s: pl.BlockSpec((1,) + s, lambda i, te, nu: (te[i], 0, 0))
    lspec = lambda s: pl.BlockSpec((1, 1) + s, lambda i, te, nu: (layer, te[i], 0, 0))
    return pl.pallas_call(
        _experts_kernel,
        out_shape=jax.ShapeDtypeStruct((n_rows, D), F32),
        grid_spec=pltpu.PrefetchScalarGridSpec(
            num_scalar_prefetch=2,
            grid=(n_tiles,),
            in_specs=[rows, pl.BlockSpec((1, D), lambda i, te, nu: (0, 0)),
                      lspec((D, two_f)), wspec((1, F)), wspec((1, F)), lspec((F, D)), wspec((1, D)),
                      pl.BlockSpec((DEINT_BLOCK, DEINT_BLOCK), lambda i, te, nu: (0, 0))],
            out_specs=rows,
            scratch_shapes=[pltpu.VMEM((D, two_f), BF16), pltpu.VMEM((F, D), BF16)],
        ),
        compiler_params=_cparams(("arbitrary",)),
        name="experts",
    )(tile_expert, n_used, xs, prm["g_ffn"], w_e1, prm["b1g"], prm["b1l"], w_e2, prm["b2"], tabs["deint"])


def _combine_kernel(pos_ref, pos_next_ref, x_ref, tw_ref, ys_ref, o_ref, buf, sem, *, tm, n_steps):
    i = pl.program_id(0)
    slot = i % 2

    def gather(idx_ref, s):
        def issue(r, _):
            for kk in range(TOP_K):
                _row_copy(ys_ref, idx_ref[0, 0, r * TOP_K + kk], buf.at[s, kk], r,
                          sem.at[s]).start(priority=kk % 2)
            return 0
        lax.fori_loop(0, tm, issue, 0, unroll=ROW_DMA_UNROLL)

    @pl.when(i == 0)
    def _():
        gather(pos_ref, 0)

    @pl.when(i + 1 < n_steps)
    def _():
        gather(pos_next_ref, 1 - slot)

    pltpu.make_async_copy(buf.at[slot], buf.at[slot], sem.at[slot]).wait()

    tw = tw_ref[...]
    acc = x_ref[...]
    for kk in range(TOP_K):
        acc = acc + tw[:, kk:kk + 1] * buf[slot, kk]
    o_ref[...] = acc


def _combine(x2, tw, ys, pos, tm):
    T, D = x2.shape
    W = ys.shape[1]
    n_steps = T // tm
    idx = lambda off: pl.BlockSpec((1, 1, tm * TOP_K), lambda i: (jnp.minimum(i + off, n_steps - 1), 0, 0),
                                   memory_space=pltpu.SMEM)
    pos3 = pos.reshape(n_steps, 1, tm * TOP_K)
    return pl.pallas_call(
        functools.partial(_combine_kernel, tm=tm, n_steps=n_steps),
        out_shape=jax.ShapeDtypeStruct((T, D), F32),
        grid=(n_steps,),
        in_specs=[idx(0), idx(1),
                  pl.BlockSpec((tm, D), lambda i: (i, 0)),
                  pl.BlockSpec((tm, LANES), lambda i: (i, 0)),
                  pl.BlockSpec(memory_space=pl.ANY)],
        out_specs=pl.BlockSpec((tm, D), lambda i: (i, 0)),
        scratch_shapes=[pltpu.VMEM((2, TOP_K, tm, W), ys.dtype), pltpu.SemaphoreType.DMA((2,))],
        compiler_params=_cparams(("arbitrary",)),
        name="combine",
    )(pos3, pos3, x2, tw, ys)


def _tile_gain(g, reps, scale=1.0):
    return (jnp.tile(g.astype(F32), reps) * scale)[None, :]


def _prep_layer(l, D, w_in, b_forget, g_mix, g_qa, g_ka, g_cq, g_ckv, w_uq, w_ukv, g_qb, g_kb, g_qc, g_kc,
                g_group, w_out, g_xmem, g_mem, w_mq, w_mkv, g_qm, g_km, w_mo, g_ffn, w_router, b_router,
                w_e1, b_e1, w_e2, b_e2):
    wi = w_in[l]
    sizes = (FOX_WIDTH, FOX_WIDTH, FOX_WIDTH, FOX_HEADS, MLA_Q_RANK, MLA_KV_RANK, MLA_ROPE_DIM,
             DIL_WIDTH, DIL_WIDTH, DIL_WIDTH)
    pts = np.cumsum((0,) + sizes)
    qa, ka, va, fa, cq, ckv, kr, qc, kc, vc = [wi[:, pts[j]:pts[j + 1]] for j in range(len(sizes))]
    misc = jnp.zeros((D, LANES), F32).at[:, :FOX_HEADS].set(fa).at[:, 32:32 + MLA_ROPE_DIM].set(kr)
    w_main = jnp.concatenate([qa, ka, va, cq, ckv, qc, kc, vc, misc], axis=1).astype(BF16)
    b_misc = jnp.zeros((1, LANES), F32).at[0, :FOX_HEADS].set(b_forget[l])

    uq = w_uq[l].reshape(MLA_Q_RANK, MLA_HEADS, MLA_QK_DIM)
    uq = jnp.pad(uq, ((0, 0), (0, 0), (0, MLA_PAD - MLA_QK_DIM))).reshape(MLA_Q_RANK, MLA_HEADS * MLA_PAD)
    ukv = w_ukv[l].reshape(MLA_KV_RANK, MLA_HEADS, MLA_NOPE_DIM + MLA_V_DIM)
    uk = jnp.pad(ukv[:, :, :MLA_NOPE_DIM], ((0, 0), (0, 0), (0, MLA_PAD - MLA_NOPE_DIM)))
    uk = uk.reshape(MLA_KV_RANK, MLA_HEADS * MLA_PAD)
    uv = ukv[:, :, MLA_NOPE_DIM:].reshape(MLA_KV_RANK, MLA_WIDTH)
    pad_gain = lambda g: jnp.pad(g.astype(F32), (0, MLA_PAD - MLA_QK_DIM))

    E = w_e1.shape[1]
    wr = jnp.zeros((D, LANES), F32).at[:, :E].set(w_router[l])
    wr_hi = wr.astype(BF16)
    wr_lo = (wr - wr_hi.astype(F32)).astype(BF16)
    q_scale = lambda dim: dim ** -0.5 * LOG2E
    return dict(
        g_mix=g_mix[l][None], w_main=w_main, b_misc=b_misc,
        g_qa=_tile_gain(g_qa[l], FOX_HEADS, q_scale(HEAD_DIM)), g_ka=_tile_gain(g_ka[l], FOX_HEADS),
        g_cq=g_cq[l][None], g_ckv=g_ckv[l][None],
        w_uq=uq.astype(BF16), w_uk=uk.astype(BF16), w_uv=uv.astype(BF16),
        g_qb=_tile_gain(pad_gain(g_qb[l]), MLA_HEADS, q_scale(MLA_QK_DIM)),
        g_kb=_tile_gain(pad_gain(g_kb[l]), MLA_HEADS),
        g_qc=_tile_gain(g_qc[l], DIL_HEADS, q_scale(HEAD_DIM)), g_kc=_tile_gain(g_kc[l], DIL_HEADS),
        g_group=g_group[l][None], w_out=w_out[l].astype(BF16),
        g_xmem=g_xmem[l][None], g_mem=g_mem[l][None],
        w_mq=w_mq[l].astype(BF16), w_mkv=w_mkv[l].astype(BF16),
        g_qm=_tile_gain(g_qm[l], MEM_HEADS, q_scale(HEAD_DIM)), g_km=_tile_gain(g_km[l], MEM_HEADS),
        w_mo=w_mo[l].astype(BF16), g_ffn=g_ffn[l][None],
        w_r_hi=wr_hi, w_r_lo=wr_lo,
        b_router=jnp.zeros((1, LANES), F32).at[0, :E].set(b_router[l]),
        b1g=b_e1[l][:, None, 0::2], b1l=b_e1[l][:, None, 1::2], b2=b_e2[l][:, None, :],
    )


def _rope_lane_tables(positions, rot_dim, period, offset):
    half = rot_dim // 2
    assert offset % half == 0 and period % half == 0 and LANES % half == 0
    inv = ROPE_THETA ** (-jnp.arange(0, rot_dim, 2, dtype=F32) / rot_dim)
    ang = positions.astype(F32).reshape(-1)[:, None] * inv
    cos = jnp.tile(jnp.cos(ang), (1, LANES // half))
    sin = jnp.tile(jnp.sin(ang), (1, LANES // half))
    lane = np.arange(LANES) % period - offset
    first = (lane >= 0) & (lane < half)
    second = (lane >= half) & (lane < rot_dim)
    c = jnp.where(first | second, cos, 1.0)
    s_dn = jnp.where(second, sin, 0.0)
    s_up = jnp.where(first, -sin, 0.0)
    return c, s_dn, s_up


def _tables(positions, n_incl, n_strict):
    ri, rs = np.arange(n_incl), np.arange(n_strict)
    bd = (np.arange(LANES)[:, None] // HEAD_DIM) == (np.arange(LANES)[None, :] // HEAD_DIM)
    rep = np.zeros((LANES, MLA_HEADS * MLA_PAD), np.float32)
    for h in range(MLA_HEADS):
        for j in range(MLA_ROPE_DIM):
            rep[32 + j, h * MLA_PAD + MLA_NOPE_DIM + j] = 1.0
    half = DEINT_BLOCK // 2
    deint = np.zeros((DEINT_BLOCK, DEINT_BLOCK), np.float32)
    deint[2 * np.arange(half), np.arange(half)] = 1.0
    deint[2 * np.arange(half) + 1, half + np.arange(half)] = 1.0
    cm, sdm, sum_ = _rope_lane_tables(positions, MLA_ROPE_DIM, LANES, MLA_NOPE_DIM)
    cd, sdd, sud = _rope_lane_tables(positions, ROT_DIM, HEAD_DIM, 0)
    return dict(
        tri_incl=jnp.asarray(ri[:, None] >= ri[None, :], BF16),
        tri_strict=jnp.asarray(rs[:, None] > rs[None, :], BF16),
        bd64=jnp.asarray(bd, BF16), ones=jnp.ones((LANES, LANES), BF16), rep=jnp.asarray(rep, BF16),
        deint=jnp.asarray(deint, BF16),
        cm=cm, sdm=sdm, sum=sum_, cd=cd, sdd=sdd, sud=sud,
    )


def _routing_tables(ti, rk, cnt, n_experts, tr, n_tiles):
    counts = cnt[0, :n_experts].astype(jnp.int32)
    padded = ((counts + tr - 1) // tr) * tr
    ends = jnp.cumsum(padded)
    starts = ends - padded
    idx = ti[:, :TOP_K]
    hit = idx[:, :, None] == jnp.arange(n_experts, dtype=jnp.int32)
    pos = rk[:, :TOP_K] + jnp.sum(jnp.where(hit, starts, 0), axis=-1)
    tile_start = jnp.arange(n_tiles, dtype=jnp.int32) * tr
    tile_expert = jnp.sum((ends[None, :] <= tile_start[:, None]).astype(jnp.int32), axis=1)
    tile_expert = jnp.minimum(tile_expert, n_experts - 1)
    n_used = (ends[n_experts - 1:] // tr).astype(jnp.int32)
    pad_fill = jnp.where(padded > 0, ends - tr, -1)
    fill_start = jnp.concatenate([pad_fill, ends[n_experts - 1:]])
    return pos.astype(jnp.int32), tile_expert.astype(jnp.int32), fill_start.astype(jnp.int32), n_used


def _pick_tile(seq, want):
    t = min(want, seq)
    while seq % t:
        t //= 2
    return t


def kernel(x, mem, positions, g_mix, w_in, b_forget, g_qa, g_ka, g_cq, g_ckv, w_uq, w_ukv, g_qb, g_kb, g_qc, g_kc, g_group, w_out, g_xmem, g_mem, w_mq, w_mkv, g_qm, g_km, w_mo, g_ffn, w_router, b_router, w_e1, b_e1, w_e2, b_e2):
    B, S, D = x.shape
    T = B * S
    depth = w_in.shape[0]
    E = w_e1.shape[1]
    tm_in = _pick_tile(S, INPROJ_TILE)
    tm_post = _pick_tile(S, POSTATTN_TILE)
    tq = _pick_tile(S, ATTN_TILE)
    tr = EXPERT_ROW_TILE
    t_disp = _pick_tile(S, DISPATCH_TILE)
    t_comb = _pick_tile(S, COMBINE_TILE)
    n_rows = T * TOP_K + (E + 1) * tr
    n_tiles = n_rows // tr

    tabs = _tables(positions, tm_in // INPROJ_SPLIT, tm_post // POSTATTN_SPLIT)
    x2d = x.reshape(T, D)
    for l in range(depth):
        prm = _prep_layer(l, D, w_in, b_forget, g_mix, g_qa, g_ka, g_cq, g_ckv, w_uq, w_ukv, g_qb, g_kb,
                          g_qc, g_kc, g_group, w_out, g_xmem, g_mem, w_mq, w_mkv, g_qm, g_km, w_mo, g_ffn,
                          w_router, b_router, w_e1, b_e1, w_e2, b_e2)
        qa, ka, va, csum, qb, kb, vb, qc, kc, vc = _inproj(x2d, prm, tabs, S, tm_in)
        kbias = -LOG2E * csum[:, :FOX_HEADS].reshape(B, S, FOX_HEADS // 2, 2).transpose(0, 2, 3, 1)
        r3 = lambda a: a.reshape(B, S, a.shape[-1])
        oa = _causal_attention(r3(qa), r3(ka), r3(va), kbias, tq=tq, split_heads=False)
        ob = _causal_attention(r3(qb), r3(kb), r3(vb), None, tq=tq, split_heads=True)
        oc = _dilated_attention(r3(qc), r3(kc), r3(vc), S)
        kmem, vmem = _memkv(mem, prm, tabs)
        x2, ti, tw, rk, cnt = _postattn(x2d, oa.reshape(T, -1), ob.reshape(T, -1), oc.reshape(T, -1),
                                        kmem, vmem, prm, tabs, S, tm_post, E)
        pos, tile_expert, pad_start, n_used = _routing_tables(ti, rk, cnt, E, tr, n_tiles)
        xs = _dispatch(x2, pos, pad_start, n_rows, t_disp, tr)
        ys = _experts(tile_expert, n_used, xs, w_e1, w_e2, l, prm, tabs, tr)
        x2d = _combine(x2, tw, ys, pos, t_comb)
    return x2d.reshape(B, S, D)
```
